```python
import math
import jax, jax.numpy as jnp
from jax import lax
import numpy as np

D_MODEL = 4096
BATCH = 4
SEQ = 2048
DEPTH = 1
DEC_BATCH = 128
DEC_SEQ = 8
PAST_LEN = 16384
PAGE_SIZE = 128

RET_HEADS = 16
RET_DK = 128
RET_DV = 256
RET_QK = RET_HEADS * RET_DK
RET_V = RET_HEADS * RET_DV
RET_CHUNK = 128
ROPE_BASE = 10000.0

S5_WIDTH = D_MODEL // 2
S5_GROUP = 16
S5_GROUPS = S5_WIDTH // S5_GROUP
S5_STATE = 64

X_HEADS = 4
X_WIDTH = D_MODEL // 2
X_HD = X_WIDTH // X_HEADS
MEM_LEN = 256

DN_ALPHA = (2.0 * DEPTH) ** 0.25
DN_BETA = (8.0 * DEPTH) ** -0.25
LN_EPS = 1e-5
GN_EPS = 1e-5

IN_SPLIT_SIZES = (RET_QK, RET_QK, RET_V, RET_V, S5_WIDTH, S5_WIDTH, X_WIDTH, X_WIDTH, D_MODEL, D_MODEL, D_MODEL)
IN_WIDTH = 2 * RET_QK + 2 * RET_V + 2 * S5_WIDTH + 2 * X_WIDTH + 3 * D_MODEL

kernel_name = "retnet_s5_memory_hybrid_step"

F32 = jnp.float32


def _split_points():
    pts, acc = [], 0
    for s in IN_SPLIT_SIZES[:-1]:
        acc += s
        pts.append(acc)
    return pts


def layer_norm(x, g, b):
    xf = x.astype(F32)
    mu = jnp.mean(xf, -1, keepdims=True)
    var = jnp.mean(jnp.square(xf - mu), -1, keepdims=True)
    return ((xf - mu) * lax.rsqrt(var + LN_EPS) * g.astype(F32) + b.astype(F32)).astype(x.dtype)


def head_norm(o):
    of = o.astype(F32)
    mu = jnp.mean(of, -1, keepdims=True)
    var = jnp.mean(jnp.square(of - mu), -1, keepdims=True)
    return ((of - mu) * lax.rsqrt(var + GN_EPS)).astype(o.dtype)


def rotary(x, pos):
    half = RET_DK // 2
    inv = 1.0 / (ROPE_BASE ** (jnp.arange(half, dtype=F32) / half))
    ang = pos.astype(F32)[:, None] * inv[None, :]
    cos = jnp.cos(ang)[None, :, None, :]
    sin = jnp.sin(ang)[None, :, None, :]
    xr = x.astype(F32).reshape(x.shape[:-1] + (half, 2))
    x0, x1 = xr[..., 0], xr[..., 1]
    out = jnp.stack([x0 * cos - x1 * sin, x0 * sin + x1 * cos], -1)
    return out.reshape(x.shape).astype(x.dtype)


def ret_log_decay():
    return jnp.log1p(-jnp.exp2(-5.0 - jnp.arange(RET_HEADS, dtype=F32)))


def retention(q, k, v, s0, chunk):
    B, L = q.shape[0], q.shape[1]
    nc = L // chunk
    lg = ret_log_decay()
    idx = jnp.arange(chunk, dtype=F32)
    rel = idx[:, None] - idx[None, :]
    inner = jnp.where(rel[None] >= 0, jnp.exp(lg[:, None, None] * jnp.maximum(rel, 0.0)[None]), 0.0)
    xi = jnp.exp(lg[:, None] * (idx + 1.0)).T[None, :, :, None]
    zeta = jnp.exp(lg[:, None] * (chunk - 1.0 - idx))
    g_chunk = jnp.exp(lg * chunk)[None, :, None, None]

    def to_chunks(t):
        t = t.astype(F32)
        return t.reshape((B, nc, chunk) + t.shape[2:]).swapaxes(0, 1)

    def step(s, inp):
        qc, kc, vc = inp
        sc = jnp.einsum('bihd,bjhd->bhij', qc, kc) * inner[None]
        o = jnp.einsum('bhij,bjhe->bihe', sc, vc) + jnp.einsum('bihd,bhde->bihe', qc, s) * xi
        s_new = g_chunk * s + jnp.einsum('bjhd,bjhe,hj->bhde', kc, vc, zeta)
        return s_new, o

    s_fin, o = lax.scan(step, s0.astype(F32), (to_chunks(q), to_chunks(k), to_chunks(v)))
    o = o.swapaxes(0, 1).reshape(B, L, RET_HEADS, RET_DV)
    return o.astype(v.dtype), s_fin.astype(s0.dtype)


def s5_discretize(a_re, a_im, log_step, b_re, b_im):
    dt = jnp.exp(log_step.astype(F32))[:, None]
    ar, ai = a_re.astype(F32), a_im.astype(F32)
    mag = jnp.exp(dt * ar)
    abar_re = mag * jnp.cos(dt * ai)
    abar_im = mag * jnp.sin(dt * ai)
    den = ar * ar + ai * ai
    x_re = abar_re - 1.0
    f_re = (x_re * ar + abar_im * ai) / den
    f_im = (abar_im * ar - x_re * ai) / den
    br, bi = b_re.astype(F32), b_im.astype(F32)
    bbar_re = f_re[..., None] * br - f_im[..., None] * bi
    bbar_im = f_re[..., None] * bi + f_im[..., None] * br
    return abar_re, abar_im, bbar_re, bbar_im


def s5_branch(u, h0_re, h0_im, a_re, a_im, log_step, b_re, b_im, c_re, c_im, d_skip):
    B, L = u.shape[0], u.shape[1]
    abar_re, abar_im, bbar_re, bbar_im = s5_discretize(a_re, a_im, log_step, b_re, b_im)
    uf = u.astype(F32)
    ug = uf.reshape(B, L, S5_GROUPS, S5_GROUP)
    bu_re = jnp.einsum('blgp,gnp->blgn', ug, bbar_re)
    bu_im = jnp.einsum('blgp,gnp->blgn', ug, bbar_im)
    h0r, h0i = h0_re.astype(F32), h0_im.astype(F32)
    bu_re = bu_re.at[:, 0].add(abar_re * h0r - abar_im * h0i)
    bu_im = bu_im.at[:, 0].add(abar_re * h0i + abar_im * h0r)
    a_r = jnp.broadcast_to(abar_re, bu_re.shape)
    a_i = jnp.broadcast_to(abar_im, bu_im.shape)

    def combine(e1, e2):
        a1r, a1i, b1r, b1i = e1
        a2r, a2i, b2r, b2i = e2
        return (a1r * a2r - a1i * a2i,
                a1r * a2i + a1i * a2r,
                a2r * b1r - a2i * b1i + b2r,
                a2r * b1i + a2i * b1r + b2i)

    _, _, h_re, h_im = lax.associative_scan(combine, (a_r, a_i, bu_re, bu_im), axis=1)
    y = (jnp.einsum('blgn,gpn->blgp', h_re, c_re.astype(F32))
         - jnp.einsum('blgn,gpn->blgp', h_im, c_im.astype(F32)))
    y = y.reshape(B, L, S5_WIDTH) + d_skip.astype(F32) * uf
    return y.astype(u.dtype), h_re[:, -1].astype(h0_re.dtype), h_im[:, -1].astype(h0_im.dtype)


def cross_attend(q, mk, mv):
    s = jnp.einsum('blhd,bmhd->bhlm', q, mk).astype(F32) * (X_HD ** -0.5)
    p = jax.nn.softmax(s, axis=-1).astype(mv.dtype)
    return jnp.einsum('bhlm,bmhd->blhd', p, mv)


def memory_kv(mem, w_mem_kv):
    B = mem.shape[0]
    kv = mem @ w_mem_kv
    mk, mv = jnp.split(kv, 2, axis=-1)
    return mk.reshape(B, MEM_LEN, X_HEADS, X_HD), mv.reshape(B, MEM_LEN, X_HEADS, X_HD)


def mixer_layer(x, pos, s_ret, h_re, h_im, mem_k, mem_v, chunk,
                w_in, a_re, a_im, log_step, b_re, b_im, c_re, c_im, d_skip, w_glu,
                w_proj_a, w_proj_b, w_proj_c, w_out, ln_g, ln_b):
    B, L = x.shape[0], x.shape[1]
    z = x @ w_in
    q, k, v, g_ret, u, g_s5, qx, g_x, m_a, m_b, m_c = jnp.split(z, _split_points(), axis=-1)

    q = rotary(q.reshape(B, L, RET_HEADS, RET_DK), pos)
    k = rotary(k.reshape(B, L, RET_HEADS, RET_DK), pos) * (RET_DK ** -0.5)
    v = v.reshape(B, L, RET_HEADS, RET_DV)
    o_ret, s_ret_new = retention(q, k, v, s_ret, chunk)
    o_ret = head_norm(o_ret).reshape(B, L, RET_V) * jax.nn.silu(g_ret)

    y_s5, h_re_new, h_im_new = s5_branch(u, h_re, h_im, a_re, a_im, log_step, b_re, b_im, c_re, c_im, d_skip)
    gl = jax.nn.gelu(y_s5)
    glu_a, glu_b = jnp.split(gl @ w_glu, 2, axis=-1)
    o_s5 = glu_a * jax.nn.sigmoid(glu_b) * jax.nn.silu(g_s5)

    o_x = cross_attend(qx.reshape(B, L, X_HEADS, X_HD), mem_k, mem_v).reshape(B, L, X_WIDTH) * jax.nn.silu(g_x)

    merged = (jax.nn.sigmoid(m_a) * (o_ret @ w_proj_a)
              + jax.nn.sigmoid(m_b) * (o_s5 @ w_proj_b)
              + jax.nn.sigmoid(m_c) * (o_x @ w_proj_c))
    out = merged @ w_out
    x_new = layer_norm(DN_ALPHA * x + out, ln_g, ln_b)
    return x_new, s_ret_new, h_re_new, h_im_new


def setup_inputs(seed: int = 0) -> dict:
    key = jax.random.key(seed)
    ks = jax.random.split(key, 32)
    nrm = jax.random.normal
    Dp = DEPTH
    inp = {}
    inp["x_prompt"] = nrm(ks[0], (BATCH, SEQ, D_MODEL), F32)
    inp["x_sample"] = nrm(ks[1], (DEC_BATCH, DEC_SEQ, D_MODEL), F32)
    inp["mem_prompt"] = nrm(ks[2], (BATCH, MEM_LEN, D_MODEL), F32)
    inp["state_ret"] = nrm(ks[3], (Dp, DEC_BATCH, RET_HEADS, RET_DK, RET_DV), F32)
    inp["state_s5_re"] = 0.1 * nrm(ks[4], (Dp, DEC_BATCH, S5_GROUPS, S5_STATE), F32)
    inp["state_s5_im"] = 0.1 * nrm(ks[5], (Dp, DEC_BATCH, S5_GROUPS, S5_STATE), F32)
    inp["cache_mem_k"] = nrm(ks[6], (Dp, DEC_BATCH, MEM_LEN, X_HEADS, X_HD), F32)
    inp["cache_mem_v"] = nrm(ks[7], (Dp, DEC_BATCH, MEM_LEN, X_HEADS, X_HD), F32)
    inp["w_in"] = nrm(ks[8], (Dp, D_MODEL, IN_WIDTH), F32) * (D_MODEL ** -0.5)
    inp["w_mem_kv"] = nrm(ks[9], (Dp, D_MODEL, 2 * X_WIDTH), F32) * (D_MODEL ** -0.5)
    inp["s5_a_re"] = -0.5 + 0.01 * nrm(ks[10], (Dp, S5_GROUPS, S5_STATE), F32)
    inp["s5_a_im"] = (jnp.pi * jnp.arange(S5_STATE, dtype=F32))[None, None, :] + 0.01 * nrm(ks[11], (Dp, S5_GROUPS, S5_STATE), F32)
    inp["s5_log_step"] = jax.random.uniform(ks[12], (Dp, S5_GROUPS), F32, minval=math.log(1e-3), maxval=math.log(1e-1))
    inp["s5_b_re"] = nrm(ks[13], (Dp, S5_GROUPS, S5_STATE, S5_GROUP), F32) * ((2.0 * S5_GROUP) ** -0.5)
    inp["s5_b_im"] = nrm(ks[14], (Dp, S5_GROUPS, S5_STATE, S5_GROUP), F32) * ((2.0 * S5_GROUP) ** -0.5)
    inp["s5_c_re"] = nrm(ks[15], (Dp, S5_GROUPS, S5_GROUP, S5_STATE), F32) * ((2.0 * S5_STATE) ** -0.5)
    inp["s5_c_im"] = nrm(ks[16], (Dp, S5_GROUPS, S5_GROUP, S5_STATE), F32) * ((2.0 * S5_STATE) ** -0.5)
    inp["s5_d"] = nrm(ks[17], (Dp, S5_WIDTH), F32)
    inp["w_glu"] = nrm(ks[18], (Dp, S5_WIDTH, 2 * S5_WIDTH), F32) * (S5_WIDTH ** -0.5)
    inp["w_proj_a"] = nrm(ks[19], (Dp, RET_V, D_MODEL), F32) * (RET_V ** -0.5) * DN_BETA
    inp["w_proj_b"] = nrm(ks[20], (Dp, S5_WIDTH, D_MODEL), F32) * (S5_WIDTH ** -0.5) * DN_BETA
    inp["w_proj_c"] = nrm(ks[21], (Dp, X_WIDTH, D_MODEL), F32) * (X_WIDTH ** -0.5) * DN_BETA
    inp["w_out"] = nrm(ks[22], (Dp, D_MODEL, D_MODEL), F32) * (D_MODEL ** -0.5) * DN_BETA
    inp["ln_g"] = 1.0 + 0.01 * nrm(ks[23], (Dp, D_MODEL), F32)
    inp["ln_b"] = 0.01 * nrm(ks[24], (Dp, D_MODEL), F32)
    return inp


def reference(x_prompt, x_sample, mem_prompt, state_ret, state_s5_re, state_s5_im, cache_mem_k, cache_mem_v,
              w_in, w_mem_kv, s5_a_re, s5_a_im, s5_log_step, s5_b_re, s5_b_im, s5_c_re, s5_c_im, s5_d, w_glu,
              w_proj_a, w_proj_b, w_proj_c, w_out, ln_g, ln_b):
    pos_p = jnp.arange(SEQ, dtype=jnp.int32)
    pos_s = PAST_LEN + jnp.arange(DEC_SEQ, dtype=jnp.int32)
    chunk_p = min(RET_CHUNK, SEQ)
    yp, ys = x_prompt, x_sample
    ret_p, hre_p, him_p, mk_p_all, mv_p_all = [], [], [], [], []
    ret_s, hre_s, him_s = [], [], []
    for l in range(DEPTH):
        lw = (w_in[l], s5_a_re[l], s5_a_im[l], s5_log_step[l], s5_b_re[l], s5_b_im[l], s5_c_re[l], s5_c_im[l],
              s5_d[l], w_glu[l], w_proj_a[l], w_proj_b[l], w_proj_c[l], w_out[l], ln_g[l], ln_b[l])
        mk_p, mv_p = memory_kv(mem_prompt, w_mem_kv[l])
        s0 = jnp.zeros((BATCH, RET_HEADS, RET_DK, RET_DV), x_prompt.dtype)
        h0 = jnp.zeros((BATCH, S5_GROUPS, S5_STATE), x_prompt.dtype)
        yp, sr, hr, hi = mixer_layer(yp, pos_p, s0, h0, h0, mk_p, mv_p, chunk_p, *lw)
        ret_p.append(sr); hre_p.append(hr); him_p.append(hi); mk_p_all.append(mk_p); mv_p_all.append(mv_p)
        ys, sr, hr, hi = mixer_layer(ys, pos_s, state_ret[l], state_s5_re[l], state_s5_im[l],
                                     cache_mem_k[l], cache_mem_v[l], DEC_SEQ, *lw)
        ret_s.append(sr); hre_s.append(hr); him_s.append(hi)
    return (yp, ys,
            jnp.stack(ret_p), jnp.stack(hre_p), jnp.stack(him_p), jnp.stack(mk_p_all), jnp.stack(mv_p_all),
            jnp.stack(ret_s), jnp.stack(hre_s), jnp.stack(him_s))
```

```python
import functools
import math

import jax
import jax.numpy as jnp
from jax import lax
from jax.experimental import pallas as pl
from jax.experimental.pallas import tpu as pltpu

F32 = jnp.float32
BF16 = jnp.bfloat16

D_MODEL = 4096
BATCH = 4
SEQ = 2048
DEC_BATCH = 128
DEC_SEQ = 8
PAST_LEN = 16384

RET_HEADS = 16
RET_DK = 128
RET_DV = 256
RET_QK = RET_HEADS * RET_DK
RET_V = RET_HEADS * RET_DV
RET_CHUNK = 128
ROPE_BASE = 10000.0

S5_WIDTH = D_MODEL // 2
S5_GROUP = 16
S5_GROUPS = S5_WIDTH // S5_GROUP
S5_STATE = 64

X_HEADS = 4
X_WIDTH = D_MODEL // 2
X_HD = X_WIDTH // X_HEADS
MEM_LEN = 256

DN_ALPHA = 2.0 ** 0.25
LN_EPS = 1e-5
GN_EPS = 1e-5

IN_WIDTH = 2 * RET_QK + 2 * RET_V + 2 * S5_WIDTH + 2 * X_WIDTH + 3 * D_MODEL

Z_Q = 0
Z_K = Z_Q + RET_QK
Z_V = Z_K + RET_QK
Z_GRET = Z_V + RET_V
Z_U = Z_GRET + RET_V
Z_GS5 = Z_U + S5_WIDTH
Z_QX = Z_GS5 + S5_WIDTH
Z_GX = Z_QX + X_WIDTH
Z_MA = Z_GX + X_WIDTH
Z_MB = Z_MA + D_MODEL
Z_MC = Z_MB + D_MODEL

SUBLANES = 8
LANES = 128
VMEM_LIMIT_BYTES = 56 * 1024 * 1024

S5_BLK_GROUPS = LANES // S5_GROUP
S5_BLK_STATE = S5_BLK_GROUPS * S5_STATE
S5_NBLK = S5_GROUPS // S5_BLK_GROUPS
SCAN_LEVELS = (1, 2, 4)


def _cparams(*sem):
    return pltpu.CompilerParams(dimension_semantics=sem, vmem_limit_bytes=VMEM_LIMIT_BYTES)


def _mm_kernel(x_ref, w_ref, o_ref):
    o_ref[...] = jnp.dot(x_ref[...], w_ref[...], preferred_element_type=F32).astype(o_ref.dtype)


def _matmul(x, w, *, bm, bn, out_dtype, n_out, col_blk_off=0, name):
    m, k = x.shape
    return pl.pallas_call(
        _mm_kernel,
        grid=(n_out // bn, m // bm),
        in_specs=[
            pl.BlockSpec((bm, k), lambda j, i: (i, 0)),
            pl.BlockSpec((k, bn), lambda j, i: (0, j + col_blk_off)),
        ],
        out_specs=pl.BlockSpec((bm, bn), lambda j, i: (i, j)),
        out_shape=jax.ShapeDtypeStruct((m, n_out), out_dtype),
        compiler_params=_cparams("parallel", "parallel"),
        name=name,
    )(x, w)


def _rotate(x, cos, sin_next, sin_prev):
    return x * cos + pltpu.roll(x, LANES - 1, 1) * sin_next + pltpu.roll(x, 1, 1) * sin_prev


def _ret_block(q, k, v, g, cos, sin_next, sin_prev, mask, xi, zeta, gc, states, slen):
    nseq = len(states)
    rows = q.shape[0]
    qr = _rotate(q, cos, sin_next, sin_prev)
    kr = _rotate(k, cos, sin_next, sin_prev) * (RET_DK ** -0.5)
    qb = qr.astype(BF16)
    kb = kr.astype(BF16)
    sc = lax.dot_general(qb, kb, (((1,), (1,)), ((), ())), preferred_element_type=F32) * mask
    o = jnp.dot(sc.astype(BF16), v, preferred_element_type=F32)
    qx = qr * xi
    kzt = (kr * zeta).T.astype(BF16)
    new_states = []
    if nseq == 1:
        s = states[0]
        o = o + jnp.dot(qx.astype(BF16), s.astype(BF16), preferred_element_type=F32)
        new_states.append(gc * s + jnp.dot(kzt, v, preferred_element_type=F32))
    else:
        pair = 2 * slen
        assert pair == 2 * SUBLANES and nseq % 2 == 0
        row_in_pair = lax.broadcasted_iota(jnp.int32, (pair, RET_DV), 0)
        row_seq = lax.broadcasted_iota(jnp.int32, (rows, RET_DV), 0) // slen
        parts = []
        for m in range(nseq // 2):
            qpair = qx[m * pair:(m + 1) * pair].astype(BF16)
            o0 = jnp.dot(qpair, states[2 * m].astype(BF16), preferred_element_type=F32)
            o1 = jnp.dot(qpair, states[2 * m + 1].astype(BF16), preferred_element_type=F32)
            parts.append(jnp.where(row_in_pair < slen, o0, o1))
        o = o + jnp.concatenate(parts, axis=0)
        vf = v.astype(F32)
        for n in range(nseq):
            vn = jnp.where(row_seq == n, vf, 0.0).astype(BF16)
            new_states.append(gc * states[n] + jnp.dot(kzt, vn, preferred_element_type=F32))
    mu = jnp.mean(o, axis=-1, keepdims=True)
    d = o - mu
    var = jnp.mean(d * d, axis=-1, keepdims=True)
    on = d * lax.rsqrt(var + GN_EPS)
    out = (on * (g * jax.nn.sigmoid(g))).astype(BF16)
    return out, new_states


def _ret_prompt_kernel(gc_ref, q_ref, k_ref, v_ref, g_ref, cos_ref, sn_ref, sp_ref, mask_ref, xi_ref, zeta_ref,
                       o_ref, sfin_ref, s_scr, *, chunk, nchunks):
    gc = gc_ref[pl.program_id(1)]
    s_scr[...] = jnp.zeros_like(s_scr)
    mask = mask_ref[...]
    xi = xi_ref[...]
    zeta = zeta_ref[...]

    def body(c, carry):
        rows = pl.ds(pl.multiple_of(c * chunk, chunk), chunk)
        out, (s_new,) = _ret_block(
            q_ref[rows, :].astype(F32), k_ref[rows, :].astype(F32), v_ref[rows, :], g_ref[rows, :].astype(F32),
            cos_ref[rows, :], sn_ref[rows, :], sp_ref[rows, :], mask, xi, zeta, gc, [s_scr[...]], chunk)
        o_ref[rows, :] = out
        s_scr[...] = s_new
        return carry

    lax.fori_loop(0, nchunks, body, 0)
    sfin_ref[...] = s_scr[...]


def _ret_sample_kernel(gc_ref, q_ref, k_ref, v_ref, g_ref, cos_ref, sn_ref, sp_ref, mask_ref, xi_ref, zeta_ref,
                       s0_ref, o_ref, sfin_ref, *, slen, nseq):
    gc = gc_ref[pl.program_id(1)]
    states = [s0_ref[n] for n in range(nseq)]
    out, new_states = _ret_block(
        q_ref[...].astype(F32), k_ref[...].astype(F32), v_ref[...], g_ref[...].astype(F32),
        cos_ref[...], sn_ref[...], sp_ref[...], mask_ref[...], xi_ref[...], zeta_ref[...], gc, states, slen)
    o_ref[...] = out
    for n in range(nseq):
        sfin_ref[n] = new_states[n]


def _rope_tables(pos):
    half = RET_DK // 2
    inv = 1.0 / (ROPE_BASE ** (jnp.arange(half, dtype=F32) / half))
    ang = pos.astype(F32)[:, None] * inv[None, :]
    cos = jnp.repeat(jnp.cos(ang), 2, axis=1)
    sin = jnp.repeat(jnp.sin(ang), 2, axis=1)
    even = (jnp.arange(RET_DK) % 2) == 0
    return cos, jnp.where(even, -sin, 0.0), jnp.where(even, 0.0, sin)


def _decay_tables(slen, nseq):
    lg = jnp.log1p(-jnp.exp2(-5.0 - jnp.arange(RET_HEADS, dtype=F32)))
    idx = jnp.arange(slen, dtype=F32)
    rel = idx[:, None] - idx[None, :]
    inner = jnp.where(rel[None] >= 0, jnp.exp(lg[:, None, None] * jnp.maximum(rel, 0.0)[None]), 0.0)
    xi = jnp.exp(lg[:, None] * (idx + 1.0))
    zeta = jnp.exp(lg[:, None] * (slen - 1.0 - idx))
    gc = jnp.exp(lg * slen)
    eye = jnp.eye(nseq, dtype=F32)
    mask = jnp.einsum("nm,hij->hnimj", eye, inner).reshape(RET_HEADS, nseq * slen, nseq * slen)
    xi_t = jnp.broadcast_to(jnp.tile(xi, (1, nseq))[:, :, None], (RET_HEADS, nseq * slen, RET_DK))
    zeta_t = jnp.broadcast_to(jnp.tile(zeta, (1, nseq))[:, :, None], (RET_HEADS, nseq * slen, RET_DK))
    return mask, xi_t, zeta_t, gc


def _retention_prompt(z, nbatch, seqlen):
    chunk = RET_CHUNK
    cos, sn, sp = _rope_tables(jnp.arange(seqlen, dtype=jnp.int32))
    mask, xi, zeta, gc = _decay_tables(chunk, 1)
    tab = lambda: pl.BlockSpec((seqlen, RET_DK), lambda b, h: (0, 0))
    head_tab = lambda w: pl.BlockSpec((None, chunk, w), lambda b, h: (h, 0, 0))
    kern = functools.partial(_ret_prompt_kernel, chunk=chunk, nchunks=seqlen // chunk)
    return pl.pallas_call(
        kern,
        grid=(nbatch, RET_HEADS),
        in_specs=[
            pl.BlockSpec(memory_space=pltpu.SMEM),
            pl.BlockSpec((seqlen, RET_DK), lambda b, h: (b, Z_Q // RET_DK + h)),
            pl.BlockSpec((seqlen, RET_DK), lambda b, h: (b, Z_K // RET_DK + h)),
            pl.BlockSpec((seqlen, RET_DV), lambda b, h: (b, Z_V // RET_DV + h)),
            pl.BlockSpec((seqlen, RET_DV), lambda b, h: (b, Z_GRET // RET_DV + h)),
            tab(), tab(), tab(),
            head_tab(chunk), head_tab(RET_DK), head_tab(RET_DK),
        ],
        out_specs=[
            pl.BlockSpec((seqlen, RET_DV), lambda b, h: (b, h)),
            pl.BlockSpec((None, None, RET_DK, RET_DV), lambda b, h: (b, h, 0, 0)),
        ],
        out_shape=[
            jax.ShapeDtypeStruct((nbatch * seqlen, RET_V), BF16),
            jax.ShapeDtypeStruct((nbatch, RET_HEADS, RET_DK, RET_DV), F32),
        ],
        scratch_shapes=[pltpu.VMEM((RET_DK, RET_DV), F32)],
        compiler_params=_cparams("parallel", "parallel"),
        name="retention_prompt",
    )(gc, z, z, z, z, cos, sn, sp, mask, xi, zeta)


def _retention_sample(z, s0, row_off, nbatch, slen, pos0):
    nseq = 16
    rows = nseq * slen
    cos, sn, sp = (jnp.tile(t, (nseq, 1)) for t in _rope_tables(pos0 + jnp.arange(slen, dtype=jnp.int32)))
    mask, xi, zeta, gc = _decay_tables(slen, nseq)
    rb = row_off // rows
    tab = lambda: pl.BlockSpec((rows, RET_DK), lambda i, h: (0, 0))
    head_tab = lambda w: pl.BlockSpec((None, rows, w), lambda i, h: (h, 0, 0))
    kern = functools.partial(_ret_sample_kernel, slen=slen, nseq=nseq)
    return pl.pallas_call(
        kern,
        grid=(nbatch // nseq, RET_HEADS),
        in_specs=[
            pl.BlockSpec(memory_space=pltpu.SMEM),
            pl.BlockSpec((rows, RET_DK), lambda i, h: (rb + i, Z_Q // RET_DK + h)),
            pl.BlockSpec((rows, RET_DK), lambda i, h: (rb + i, Z_K // RET_DK + h)),
            pl.BlockSpec((rows, RET_DV), lambda i, h: (rb + i, Z_V // RET_DV + h)),
            pl.BlockSpec((rows, RET_DV), lambda i, h: (rb + i, Z_GRET // RET_DV + h)),
            tab(), tab(), tab(),
            head_tab(rows), head_tab(RET_DK), head_tab(RET_DK),
            pl.BlockSpec((nseq, None, RET_DK, RET_DV), lambda i, h: (i, h, 0, 0)),
        ],
        out_specs=[
            pl.BlockSpec((rows, RET_DV), lambda i, h: (i, h)),
            pl.BlockSpec((nseq, None, RET_DK, RET_DV), lambda i, h: (i, h, 0, 0)),
        ],
        out_shape=[
            jax.ShapeDtypeStruct((nbatch * slen, RET_V), BF16),
            jax.ShapeDtypeStruct((nbatch, RET_HEADS, RET_DK, RET_DV), F32),
        ],
        compiler_params=_cparams("parallel", "parallel"),
        name="retention_sample",
    )(gc, z, z, z, z, cos, sn, sp, mask, xi, zeta, s0)


def _s5_prep_kernel(are_ref, aim_ref, dt_ref, are_w_ref, aim_w_ref, dt_w_ref, bre_ref, bim_ref, cim_ref,
                    pre_ref, pim_ref, bbre_ref, bbim_ref, ncim_ref):
    def abar(ar, ai, dt):
        mag = jnp.exp(dt * ar)
        return mag * jnp.cos(dt * ai), mag * jnp.sin(dt * ai)

    ar, ai = abar(are_ref[...], aim_ref[...], dt_ref[...])
    pr, pi = ar, ai
    pre_ref[0] = pr
    pim_ref[0] = pi
    for t in range(1, SUBLANES):
        pr, pi = pr * ar - pi * ai, pr * ai + pi * ar
        pre_ref[t] = pr
        pim_ref[t] = pi

    a_r, a_i = are_w_ref[...], aim_w_ref[...]
    w_r, w_i = abar(a_r, a_i, dt_w_ref[...])
    den = a_r * a_r + a_i * a_i
    x_re = w_r - 1.0
    f_re = (x_re * a_r + w_i * a_i) / den
    f_im = (w_i * a_r - x_re * a_i) / den
    br, bi = bre_ref[...], bim_ref[...]
    bbre_ref[...] = f_re * br - f_im * bi
    bbim_ref[...] = f_re * bi + f_im * br
    ncim_ref[...] = -cim_ref[...]


def _s5_prepare(a_re, a_im, log_step, b_re, b_im, c_re, c_im):
    g, n, p = S5_GROUPS, S5_STATE, S5_GROUP
    dt = jnp.broadcast_to(jnp.exp(log_step.astype(F32))[:, None], (g, n))
    wide = lambda t: jnp.repeat(t, p, axis=1)
    vm = lambda: pl.BlockSpec(memory_space=pltpu.VMEM)
    pre, pim, bbre, bbim, ncim = pl.pallas_call(
        _s5_prep_kernel,
        in_specs=[vm() for _ in range(9)],
        out_specs=[vm() for _ in range(5)],
        out_shape=[
            jax.ShapeDtypeStruct((SUBLANES, g, n), F32),
            jax.ShapeDtypeStruct((SUBLANES, g, n), F32),
            jax.ShapeDtypeStruct((g, n * p), F32),
            jax.ShapeDtypeStruct((g, n * p), F32),
            jax.ShapeDtypeStruct((g, p * n), F32),
        ],
        name="s5_discretize",
    )(a_re.astype(F32), a_im.astype(F32), dt, wide(a_re.astype(F32)), wide(a_im.astype(F32)), wide(dt),
      b_re.astype(F32).reshape(g, n * p), b_im.astype(F32).reshape(g, n * p), c_im.astype(F32).reshape(g, p * n))

    nb, bg = S5_NBLK, S5_BLK_GROUPS
    eye = jnp.eye(bg, dtype=bool)

    def in_blockdiag(t):
        t = t.reshape(nb, bg, n, p).transpose(0, 1, 3, 2)
        return jnp.where(eye[None, :, None, :, None], t[:, :, :, None, :], 0.0).reshape(nb, bg * p, bg * n)

    def out_blockdiag(t):
        t = t.reshape(nb, bg, p, n).transpose(0, 1, 3, 2)
        return jnp.where(eye[None, :, None, :, None], t[:, :, :, None, :], 0.0).reshape(nb, bg * n, bg * p)

    w_in = jnp.concatenate([in_blockdiag(bbre), in_blockdiag(bbim)], axis=-1).astype(BF16)
    w_out = jnp.concatenate([out_blockdiag(c_re.astype(F32).reshape(g, p * n)), out_blockdiag(ncim)],
                            axis=1).astype(BF16)

    pw_re = pre.reshape(SUBLANES, nb, S5_BLK_STATE).transpose(1, 0, 2)
    pw_im = pim.reshape(SUBLANES, nb, S5_BLK_STATE).transpose(1, 0, 2)
    t_idx = jnp.arange(SUBLANES)[None, :, None]
    tabs = []
    for lvl in SCAN_LEVELS:
        tabs.append(jnp.where(t_idx >= lvl, pw_re[:, lvl - 1:lvl, :], 0.0))
        tabs.append(jnp.where(t_idx >= lvl, pw_im[:, lvl - 1:lvl, :], 0.0))
    tabs.append(pw_re)
    tabs.append(pw_im)
    return w_in, w_out, jnp.stack(tabs, axis=1)


def _s5_scan_tile(xr, xi, tabs_ref, cr, ci):
    for lvl_i, lvl in enumerate(SCAN_LEVELS):
        pr = tabs_ref[2 * lvl_i]
        pi = tabs_ref[2 * lvl_i + 1]
        sr = pltpu.roll(xr, lvl, 0)
        si = pltpu.roll(xi, lvl, 0)
        xr, xi = xr + pr * sr - pi * si, xi + pr * si + pi * sr
    rr = tabs_ref[2 * len(SCAN_LEVELS)]
    ri = tabs_ref[2 * len(SCAN_LEVELS) + 1]
    crb = jnp.broadcast_to(cr, xr.shape)
    cib = jnp.broadcast_to(ci, xi.shape)
    return xr + rr * crb - ri * cib, xi + rr * cib + ri * crb


def _s5_kernel(u_ref, win_ref, wout_ref, d_ref, tabs_ref, *rest, tc, carry_rows):
    if carry_rows:
        gl_ref, hre_ref, him_ref, bu_scr, h_scr, carry_scr = rest
    else:
        h0re_ref, h0im_ref, gl_ref, hre_ref, him_ref, bu_scr, h_scr = rest
    ns = S5_BLK_STATE
    u = u_ref[...]
    bu_scr[...] = jnp.dot(u, win_ref[...], preferred_element_type=F32)
    ntiles = tc // SUBLANES

    if carry_rows:
        @pl.when(pl.program_id(2) == 0)
        def _():
            carry_scr[...] = jnp.zeros_like(carry_scr)

        def body(i, carry):
            cr, ci = carry
            rows = pl.ds(pl.multiple_of(i * SUBLANES, SUBLANES), SUBLANES)
            hr, hi = _s5_scan_tile(bu_scr[rows, :ns], bu_scr[rows, ns:], tabs_ref, cr, ci)
            h_scr[rows, :ns] = hr
            h_scr[rows, ns:] = hi
            return hr[SUBLANES - 1:, :], hi[SUBLANES - 1:, :]

        cr, ci = lax.fori_loop(0, ntiles, body, (carry_scr[0:1, :ns], carry_scr[0:1, ns:]))
        carry_scr[0:1, :ns] = cr
        carry_scr[0:1, ns:] = ci
        hre_ref[...] = cr
        him_ref[...] = ci
    else:
        def body(i, carry):
            rows = pl.ds(pl.multiple_of(i * SUBLANES, SUBLANES), SUBLANES)
            hr, hi = _s5_scan_tile(bu_scr[rows, :ns], bu_scr[rows, ns:], tabs_ref,
                                   h0re_ref[pl.ds(i, 1), :], h0im_ref[pl.ds(i, 1), :])
            h_scr[rows, :ns] = hr
            h_scr[rows, ns:] = hi
            hre_ref[pl.ds(i, 1), :] = hr[SUBLANES - 1:, :]
            him_ref[pl.ds(i, 1), :] = hi[SUBLANES - 1:, :]
            return carry

        lax.fori_loop(0, ntiles, body, 0)

    y = jnp.dot(h_scr[...].astype(BF16), wout_ref[...], preferred_element_type=F32)
    y = y + d_ref[...] * u.astype(F32)
    gl_ref[...] = jax.nn.gelu(y).astype(BF16)


def _s5_common_specs(tc, row_map):
    nst = S5_BLK_STATE
    return [
        pl.BlockSpec((tc, LANES), row_map(Z_U // LANES)),
        pl.BlockSpec((None, LANES, 2 * nst), lambda j, *_: (j, 0, 0)),
        pl.BlockSpec((None, 2 * nst, LANES), lambda j, *_: (j, 0, 0)),
        pl.BlockSpec((1, LANES), lambda j, *_: (0, j)),
        pl.BlockSpec((None, 2 * len(SCAN_LEVELS) + 2, SUBLANES, nst), lambda j, *_: (j, 0, 0, 0)),
    ]


def _s5_prompt(z, w_in, w_out, d_skip, tabs, nbatch, seqlen):
    tc = 256
    nt = seqlen // tc
    nst = S5_BLK_STATE
    kern = functools.partial(_s5_kernel, tc=tc, carry_rows=True)
    row_map = lambda cb: (lambda j, b, t: (b * nt + t, cb + j))
    gl, hre, him = pl.pallas_call(
        kern,
        grid=(S5_NBLK, nbatch, nt),
        in_specs=_s5_common_specs(tc, row_map),
        out_specs=[
            pl.BlockSpec((tc, LANES), lambda j, b, t: (b * nt + t, j)),
            pl.BlockSpec((None, 1, nst), lambda j, b, t: (b, 0, j)),
            pl.BlockSpec((None, 1, nst), lambda j, b, t: (b, 0, j)),
        ],
        out_shape=[
            jax.ShapeDtypeStruct((nbatch * seqlen, S5_WIDTH), BF16),
            jax.ShapeDtypeStruct((nbatch, 1, S5_GROUPS * S5_STATE), F32),
            jax.ShapeDtypeStruct((nbatch, 1, S5_GROUPS * S5_STATE), F32),
        ],
        scratch_shapes=[pltpu.VMEM((tc, 2 * nst), F32), pltpu.VMEM((tc, 2 * nst), F32),
                        pltpu.VMEM((SUBLANES, 2 * nst), F32)],
        compiler_params=_cparams("parallel", "parallel", "arbitrary"),
        name="s5_prompt",
    )(z, w_in, w_out, d_skip, tabs)
    return gl, hre.reshape(nbatch, S5_GROUPS, S5_STATE), him.reshape(nbatch, S5_GROUPS, S5_STATE)


def _s5_sample(z, w_in, w_out, d_skip, tabs, h0_re, h0_im, row_off, nbatch, slen):
    assert slen == SUBLANES
    nseq = 16
    tc = nseq * slen
    rb = row_off // tc
    nst = S5_BLK_STATE
    kern = functools.partial(_s5_kernel, tc=tc, carry_rows=False)
    row_map = lambda cb: (lambda j, i: (rb + i, cb + j))
    state_spec = lambda: pl.BlockSpec((nseq, nst), lambda j, i: (i, j))
    gl, hre, him = pl.pallas_call(
        kern,
        grid=(S5_NBLK, nbatch // nseq),
        in_specs=_s5_common_specs(tc, row_map) + [state_spec(), state_spec()],
        out_specs=[pl.BlockSpec((tc, LANES), lambda j, i: (i, j)), state_spec(), state_spec()],
        out_shape=[
            jax.ShapeDtypeStruct((nbatch * slen, S5_WIDTH), BF16),
            jax.ShapeDtypeStruct((nbatch, S5_GROUPS * S5_STATE), F32),
            jax.ShapeDtypeStruct((nbatch, S5_GROUPS * S5_STATE), F32),
        ],
        scratch_shapes=[pltpu.VMEM((tc, 2 * nst), F32), pltpu.VMEM((tc, 2 * nst), F32)],
        compiler_params=_cparams("parallel", "parallel"),
        name="s5_sample",
    )(z, w_in, w_out, d_skip, tabs, h0_re.reshape(nbatch, -1).astype(F32), h0_im.reshape(nbatch, -1).astype(F32))
    return gl, hre.reshape(nbatch, S5_GROUPS, S5_STATE), him.reshape(nbatch, S5_GROUPS, S5_STATE)


def _xattn_kernel(q_ref, g_ref, k_ref, v_ref, o_ref, *, q_per_seq, k_per_seq):
    tq = q_ref.shape[0]
    nk = k_ref.shape[0]
    kb = k_ref[...].astype(BF16)
    vb = v_ref[...].astype(BF16)
    if tq // q_per_seq > 1:
        same = (lax.broadcasted_iota(jnp.int32, (tq, nk), 0) // q_per_seq
                == lax.broadcasted_iota(jnp.int32, (tq, nk), 1) // k_per_seq)
    else:
        same = None
    for h in range(X_HEADS):
        cols = slice(h * X_HD, (h + 1) * X_HD)
        s = lax.dot_general(q_ref[:, cols], kb[:, cols], (((1,), (1,)), ((), ())),
                            preferred_element_type=F32) * (X_HD ** -0.5)
        if same is not None:
            s = jnp.where(same, s, -jnp.inf)
        e = jnp.exp(s - jnp.max(s, axis=-1, keepdims=True))
        oh = jnp.dot(e.astype(BF16), vb[:, cols], preferred_element_type=F32) / jnp.sum(e, axis=-1, keepdims=True)
        g = g_ref[:, cols].astype(F32)
        o_ref[:, cols] = (oh * (g * jax.nn.sigmoid(g))).astype(BF16)


def _xattn(z, mk, mv, *, row_off, nrows, tq, q_per_seq, seqs_per_step, kv_col_blk, name):
    nk = seqs_per_step * MEM_LEN
    steps_per_kv = (seqs_per_step * q_per_seq) // tq if tq < seqs_per_step * q_per_seq else 1
    rb = row_off // tq
    kern = functools.partial(_xattn_kernel, q_per_seq=q_per_seq, k_per_seq=MEM_LEN)
    kmap = lambda cb: (lambda i: (i // steps_per_kv, cb))
    return pl.pallas_call(
        kern,
        grid=(nrows // tq,),
        in_specs=[
            pl.BlockSpec((tq, X_WIDTH), lambda i: (rb + i, Z_QX // X_WIDTH)),
            pl.BlockSpec((tq, X_WIDTH), lambda i: (rb + i, Z_GX // X_WIDTH)),
            pl.BlockSpec((nk, X_WIDTH), kmap(kv_col_blk[0])),
            pl.BlockSpec((nk, X_WIDTH), kmap(kv_col_blk[1])),
        ],
        out_specs=pl.BlockSpec((tq, X_WIDTH), lambda i: (i, 0)),
        out_shape=jax.ShapeDtypeStruct((nrows, X_WIDTH), BF16),
        compiler_params=_cparams("parallel"),
        name=name,
    )(z, z, mk, mv)


def _glu_kernel(x_ref, wa_ref, wb_ref, g_ref, o_ref):
    x = x_ref[...]
    a = jnp.dot(x, wa_ref[...], preferred_element_type=F32)
    b = jnp.dot(x, wb_ref[...], preferred_element_type=F32)
    g = g_ref[...].astype(F32)
    o_ref[...] = (a * jax.nn.sigmoid(b) * (g * jax.nn.sigmoid(g))).astype(o_ref.dtype)


def _glu(gl, w_glu, z, row_off, *, bm, bn):
    m, k = gl.shape
    rb = row_off // bm
    return pl.pallas_call(
        _glu_kernel,
        grid=(S5_WIDTH // bn, m // bm),
        in_specs=[
            pl.BlockSpec((bm, k), lambda j, i: (i, 0)),
            pl.BlockSpec((k, bn), lambda j, i: (0, j)),
            pl.BlockSpec((k, bn), lambda j, i: (0, j + S5_WIDTH // bn)),
            pl.BlockSpec((bm, bn), lambda j, i: (rb + i, Z_GS5 // bn + j)),
        ],
        out_specs=pl.BlockSpec((bm, bn), lambda j, i: (i, j)),
        out_shape=jax.ShapeDtypeStruct((m, S5_WIDTH), BF16),
        compiler_params=_cparams("parallel", "parallel"),
        name="glu",
    )(gl, w_glu, w_glu, z)


def _merge_kernel(oa_ref, ob_ref, oc_ref, wa_ref, wb_ref, wc_ref, ma_ref, mb_ref, mc_ref, o_ref):
    def branch(o_r, w_r, m_r):
        return jax.nn.sigmoid(m_r[...].astype(F32)) * jnp.dot(o_r[...], w_r[...], preferred_element_type=F32)

    o_ref[...] = (branch(oa_ref, wa_ref, ma_ref) + branch(ob_ref, wb_ref, mb_ref)
                  + branch(oc_ref, wc_ref, mc_ref)).astype(o_ref.dtype)


def _merge(o_a, o_b, o_c, w_a, w_b, w_c, z, row_off, *, bm, bn):
    m = o_a.shape[0]
    rb = row_off // bm
    lhs = lambda t: pl.BlockSpec((bm, t.shape[1]), lambda j, i: (i, 0))
    rhs = lambda t: pl.BlockSpec((t.shape[0], bn), lambda j, i: (0, j))
    gate = lambda off: pl.BlockSpec((bm, bn), lambda j, i: (rb + i, off // bn + j))
    return pl.pallas_call(
        _merge_kernel,
        grid=(D_MODEL // bn, m // bm),
        in_specs=[lhs(o_a), lhs(o_b), lhs(o_c), rhs(w_a), rhs(w_b), rhs(w_c), gate(Z_MA), gate(Z_MB), gate(Z_MC)],
        out_specs=pl.BlockSpec((bm, bn), lambda j, i: (i, j)),
        out_shape=jax.ShapeDtypeStruct((m, D_MODEL), BF16),
        compiler_params=_cparams("parallel", "parallel"),
        name="merge",
    )(o_a, o_b, o_c, w_a, w_b, w_c, z, z, z)


def _out_ln_kernel(m_ref, w_ref, x_ref, g_ref, b_ref, o_ref, pre_scr, *, nj, bn):
    j = pl.program_id(1)
    pre_scr[j] = DN_ALPHA * x_ref[...] + jnp.dot(m_ref[...], w_ref[...], preferred_element_type=F32)

    @pl.when(j == nj - 1)
    def _():
        width = nj * bn
        tot = pre_scr[0].sum(axis=-1, keepdims=True)
        for t in range(1, nj):
            tot = tot + pre_scr[t].sum(axis=-1, keepdims=True)
        mu = tot / width
        sq = None
        for t in range(nj):
            d = pre_scr[t] - mu
            part = (d * d).sum(axis=-1, keepdims=True)
            sq = part if sq is None else sq + part
        rstd = lax.rsqrt(sq / width + LN_EPS)
        for t in range(nj):
            cols = slice(t * bn, (t + 1) * bn)
            o_ref[:, cols] = (pre_scr[t] - mu) * rstd * g_ref[:, cols] + b_ref[:, cols]


def _out_ln(merged, w_out, x, ln_g, ln_b, *, bm, bn):
    m, k = merged.shape
    nj = D_MODEL // bn
    kern = functools.partial(_out_ln_kernel, nj=nj, bn=bn)
    return pl.pallas_call(
        kern,
        grid=(m // bm, nj),
        in_specs=[
            pl.BlockSpec((bm, k), lambda i, j: (i, 0)),
            pl.BlockSpec((k, bn), lambda i, j: (0, j)),
            pl.BlockSpec((bm, bn), lambda i, j: (i, j)),
            pl.BlockSpec((1, D_MODEL), lambda i, j: (0, 0)),
            pl.BlockSpec((1, D_MODEL), lambda i, j: (0, 0)),
        ],
        out_specs=pl.BlockSpec((bm, D_MODEL), lambda i, j: (i, 0)),
        out_shape=jax.ShapeDtypeStruct((m, D_MODEL), F32),
        scratch_shapes=[pltpu.VMEM((nj, bm, bn), F32)],
        compiler_params=_cparams("parallel", "arbitrary"),
        name="out_ln",
    )(merged, w_out, x, ln_g.reshape(1, D_MODEL).astype(F32), ln_b.reshape(1, D_MODEL).astype(F32))


def _group_tail(z, row_off, o_ret, gl, o_x, x2d, w):
    o_s5 = _glu(gl, w["glu"], z, row_off, bm=512, bn=512)
    merged = _merge(o_ret, o_s5, o_x, w["proj_a"], w["proj_b"], w["proj_c"], z, row_off, bm=512, bn=512)
    return _out_ln(merged, w["out"], x2d, w["ln_g"], w["ln_b"], bm=512, bn=512)


def kernel(x_prompt, x_sample, mem_prompt, state_ret, state_s5_re, state_s5_im, cache_mem_k, cache_mem_v, w_in, w_mem_kv, s5_a_re, s5_a_im, s5_log_step, s5_b_re, s5_b_im, s5_c_re, s5_c_im, s5_d, w_glu, w_proj_a, w_proj_b, w_proj_c, w_out, ln_g, ln_b):
    depth = w_in.shape[0]
    assert depth == 1
    l = 0
    n_p = BATCH * SEQ
    n_s = DEC_BATCH * DEC_SEQ
    xp2 = x_prompt.reshape(n_p, D_MODEL)
    xs2 = x_sample.reshape(n_s, D_MODEL)

    xb = jnp.concatenate([xp2, xs2], axis=0).astype(BF16)
    w = dict(glu=w_glu[l].astype(BF16), proj_a=w_proj_a[l].astype(BF16), proj_b=w_proj_b[l].astype(BF16),
             proj_c=w_proj_c[l].astype(BF16), out=w_out[l].astype(BF16), ln_g=ln_g[l], ln_b=ln_b[l])

    z = _matmul(xb, w_in[l].astype(BF16), bm=1024, bn=1024, out_dtype=BF16, n_out=IN_WIDTH, name="in_proj")

    memb = mem_prompt.reshape(BATCH * MEM_LEN, D_MODEL).astype(BF16)
    wkv = w_mem_kv[l].astype(BF16)
    mk = _matmul(memb, wkv, bm=1024, bn=1024, out_dtype=F32, n_out=X_WIDTH, col_blk_off=0, name="mem_k")
    mv = _matmul(memb, wkv, bm=1024, bn=1024, out_dtype=F32, n_out=X_WIDTH, col_blk_off=X_WIDTH // 1024, name="mem_v")

    s5_win, s5_wout, s5_tabs = _s5_prepare(s5_a_re[l], s5_a_im[l], s5_log_step[l], s5_b_re[l], s5_b_im[l],
                                           s5_c_re[l], s5_c_im[l])
    d_skip = s5_d[l].reshape(1, S5_WIDTH).astype(F32)

    o_ret_p, ret_p = _retention_prompt(z, BATCH, SEQ)
    gl_p, hre_p, him_p = _s5_prompt(z, s5_win, s5_wout, d_skip, s5_tabs, BATCH, SEQ)
    o_x_p = _xattn(z, mk, mv, row_off=0, nrows=n_p, tq=512, q_per_seq=SEQ, seqs_per_step=1,
                   kv_col_blk=(0, 0), name="xattn_prompt")
    y_p = _group_tail(z, 0, o_ret_p, gl_p, o_x_p, xp2, w)

    o_ret_s, ret_s = _retention_sample(z, state_ret[l], n_p, DEC_BATCH, DEC_SEQ, PAST_LEN)
    gl_s, hre_s, him_s = _s5_sample(z, s5_win, s5_wout, d_skip, s5_tabs, state_s5_re[l], state_s5_im[l],
                                    n_p, DEC_BATCH, DEC_SEQ)
    ck = cache_mem_k[l].reshape(DEC_BATCH * MEM_LEN, X_WIDTH)
    cv = cache_mem_v[l].reshape(DEC_BATCH * MEM_LEN, X_WIDTH)
    o_x_s = _xattn(z, ck, cv, row_off=n_p, nrows=n_s, tq=2 * DEC_SEQ, q_per_seq=DEC_SEQ, seqs_per_step=2,
                   kv_col_blk=(0, 0), name="xattn_sample")
    y_s = _group_tail(z, n_p, o_ret_s, gl_s, o_x_s, xs2, w)

    return (y_p.reshape(BATCH, SEQ, D_MODEL), y_s.reshape(DEC_BATCH, DEC_SEQ, D_MODEL),
            ret_p[None], hre_p[None], him_p[None],
            mk.reshape(1, BATCH, MEM_LEN, X_HEADS, X_HD), mv.reshape(1, BATCH, MEM_LEN, X_HEADS, X_HD),
            ret_s[None], hre_s[None], him_s[None])
```

```python
import functools
import math

import jax
import jax.numpy as jnp
import numpy as np
from jax import lax
from jax.experimental import pallas as pl
from jax.experimental.pallas import tpu as pltpu

F32 = jnp.float32
BF16 = jnp.bfloat16

D_MODEL = 4096
BATCH = 4
SEQ = 2048
DEC_BATCH = 128
DEC_SEQ = 8
PAST_LEN = 16384

RET_HEADS = 16
RET_DK = 128
RET_DV = 256
RET_QK = RET_HEADS * RET_DK
RET_V = RET_HEADS * RET_DV
RET_CHUNK = 128
ROPE_BASE = 10000.0

S5_WIDTH = D_MODEL // 2
S5_GROUP = 16
S5_GROUPS = S5_WIDTH // S5_GROUP
S5_STATE = 64

X_HEADS = 4
X_WIDTH = D_MODEL // 2
X_HD = X_WIDTH // X_HEADS
MEM_LEN = 256

DN_ALPHA = 2.0 ** 0.25
LN_EPS = 1e-5
GN_EPS = 1e-5

IN_WIDTH = 2 * RET_QK + 2 * RET_V + 2 * S5_WIDTH + 2 * X_WIDTH + 3 * D_MODEL

Z_Q = 0
Z_K = Z_Q + RET_QK
Z_V = Z_K + RET_QK
Z_GRET = Z_V + RET_V
Z_U = Z_GRET + RET_V
Z_GS5 = Z_U + S5_WIDTH
Z_QX = Z_GS5 + S5_WIDTH
Z_GX = Z_QX + X_WIDTH
Z_MA = Z_GX + X_WIDTH
Z_MB = Z_MA + D_MODEL
Z_MC = Z_MB + D_MODEL

SUBLANES = 8
LANES = 128
VMEM_LIMIT_BYTES = 56 * 1024 * 1024

S5_BLK_GROUPS = LANES // S5_GROUP
S5_BLK_STATE = S5_BLK_GROUPS * S5_STATE
S5_NBLK = S5_GROUPS // S5_BLK_GROUPS
SCAN_LEVELS = (1, 2, 4)


def _cparams(*sem):
    return pltpu.CompilerParams(dimension_semantics=sem, vmem_limit_bytes=VMEM_LIMIT_BYTES)


def _mm_kernel(x_ref, w_ref, o_ref, wb_scr):
    @pl.when(pl.program_id(1) == 0)
    def _():
        wb_scr[...] = w_ref[...].astype(BF16)

    o_ref[...] = jnp.dot(x_ref[...], wb_scr[...], preferred_element_type=F32).astype(o_ref.dtype)


def _matmul(x, w, *, bm, bn, out_dtype, n_out, col_blk_off=0, name):
    m, k = x.shape
    return pl.pallas_call(
        _mm_kernel,
        grid=(n_out // bn, m // bm),
        in_specs=[
            pl.BlockSpec((bm, k), lambda j, i: (i, 0)),
            pl.BlockSpec((k, bn), lambda j, i: (0, j + col_blk_off)),
        ],
        out_specs=pl.BlockSpec((bm, bn), lambda j, i: (i, j)),
        out_shape=jax.ShapeDtypeStruct((m, n_out), out_dtype),
        scratch_shapes=[pltpu.VMEM((k, bn), BF16)],
        compiler_params=_cparams("parallel", "arbitrary"),
        name=name,
    )(x, w)


def _rotate(x, cos, sin_next, sin_prev):
    return x * cos + pltpu.roll(x, LANES - 1, 1) * sin_next + pltpu.roll(x, 1, 1) * sin_prev


def _ret_block(q, k, v, g, cos, sin_next, sin_prev, mask, xi, zeta, gc, states, slen):
    nseq = len(states)
    rows = q.shape[0]
    qr = _rotate(q, cos, sin_next, sin_prev)
    kr = _rotate(k, cos, sin_next, sin_prev) * (RET_DK ** -0.5)
    qb = qr.astype(BF16)
    kb = kr.astype(BF16)
    sc = lax.dot_general(qb, kb, (((1,), (1,)), ((), ())), preferred_element_type=F32) * mask
    o = jnp.dot(sc.astype(BF16), v, preferred_element_type=F32)
    qx = qr * xi
    kzt = (kr * zeta).T.astype(BF16)
    new_states = []
    if nseq == 1:
        s = states[0]
        o = o + jnp.dot(qx.astype(BF16), s.astype(BF16), preferred_element_type=F32)
        new_states.append(gc * s + jnp.dot(kzt, v, preferred_element_type=F32))
    else:
        pair = 2 * slen
        assert pair == 2 * SUBLANES and nseq % 2 == 0
        row_in_pair = lax.broadcasted_iota(jnp.int32, (pair, RET_DV), 0)
        row_seq = lax.broadcasted_iota(jnp.int32, (rows, RET_DV), 0) // slen
        parts = []
        for m in range(nseq // 2):
            qpair = qx[m * pair:(m + 1) * pair].astype(BF16)
            o0 = jnp.dot(qpair, states[2 * m].astype(BF16), preferred_element_type=F32)
            o1 = jnp.dot(qpair, states[2 * m + 1].astype(BF16), preferred_element_type=F32)
            parts.append(jnp.where(row_in_pair < slen, o0, o1))
        o = o + jnp.concatenate(parts, axis=0)
        vf = v.astype(F32)
        for n in range(nseq):
            vn = jnp.where(row_seq == n, vf, 0.0).astype(BF16)
            new_states.append(gc * states[n] + jnp.dot(kzt, vn, preferred_element_type=F32))
    mu = jnp.mean(o, axis=-1, keepdims=True)
    d = o - mu
    var = jnp.mean(d * d, axis=-1, keepdims=True)
    on = d * lax.rsqrt(var + GN_EPS)
    out = (on * (g * jax.nn.sigmoid(g))).astype(BF16)
    return out, new_states


def _ret_prompt_kernel(gc_ref, q_ref, k_ref, v_ref, g_ref, cos_ref, sn_ref, sp_ref, mask_ref, xi_ref, zeta_ref,
                       o_ref, sfin_ref, s_scr, *, chunk, nchunks, hb):
    head0 = pl.program_id(1) * hb
    s_scr[...] = jnp.zeros_like(s_scr)

    def body(c, carry):
        rows = pl.ds(pl.multiple_of(c * chunk, chunk), chunk)
        cos, sn, sp = cos_ref[rows, :], sn_ref[rows, :], sp_ref[rows, :]
        for hh in range(hb):
            qc = slice(hh * RET_DK, (hh + 1) * RET_DK)
            vc = slice(hh * RET_DV, (hh + 1) * RET_DV)
            out, (s_new,) = _ret_block(
                q_ref[rows, qc].astype(F32), k_ref[rows, qc].astype(F32), v_ref[rows, vc],
                g_ref[rows, vc].astype(F32), cos, sn, sp, mask_ref[hh], xi_ref[hh], zeta_ref[hh],
                gc_ref[head0 + hh], [s_scr[hh]], chunk)
            o_ref[rows, vc] = out
            s_scr[hh] = s_new
        return carry

    lax.fori_loop(0, nchunks, body, 0)
    sfin_ref[...] = s_scr[...]


def _ret_sample_kernel(gc_ref, q_ref, k_ref, v_ref, g_ref, cos_ref, sn_ref, sp_ref, mask_ref, xi_ref, zeta_ref,
                       s0_ref, o_ref, sfin_ref, *, slen, nseq):
    gc = gc_ref[pl.program_id(1)]
    states = [s0_ref[n] for n in range(nseq)]
    out, new_states = _ret_block(
        q_ref[...].astype(F32), k_ref[...].astype(F32), v_ref[...], g_ref[...].astype(F32),
        cos_ref[...], sn_ref[...], sp_ref[...], mask_ref[...], xi_ref[...], zeta_ref[...], gc, states, slen)
    o_ref[...] = out
    for n in range(nseq):
        sfin_ref[n] = new_states[n]


def _rope_tables(pos):
    half = RET_DK // 2
    inv = 1.0 / (ROPE_BASE ** (np.arange(half, dtype=np.float64) / half))
    ang = np.asarray(pos, np.float64)[:, None] * inv[None, :]
    cos = np.repeat(np.cos(ang), 2, axis=1)
    sin = np.repeat(np.sin(ang), 2, axis=1)
    even = (np.arange(RET_DK) % 2) == 0
    return (np.asarray(cos, np.float32), np.asarray(np.where(even, -sin, 0.0), np.float32),
            np.asarray(np.where(even, 0.0, sin), np.float32))


def _decay_tables(slen, nseq):
    lg = np.log1p(-np.exp2(-5.0 - np.arange(RET_HEADS, dtype=np.float64)))
    idx = np.arange(slen, dtype=np.float64)
    rel = idx[:, None] - idx[None, :]
    inner = np.where(rel[None] >= 0, np.exp(lg[:, None, None] * np.maximum(rel, 0.0)[None]), 0.0)
    xi = np.exp(lg[:, None] * (idx + 1.0))
    zeta = np.exp(lg[:, None] * (slen - 1.0 - idx))
    gc = np.exp(lg * slen)
    mask = np.einsum("nm,hij->hnimj", np.eye(nseq), inner).reshape(RET_HEADS, nseq * slen, nseq * slen)
    rows = nseq * slen
    xi_t = np.broadcast_to(np.tile(xi, (1, nseq))[:, :, None], (RET_HEADS, rows, RET_DK))
    zeta_t = np.broadcast_to(np.tile(zeta, (1, nseq))[:, :, None], (RET_HEADS, rows, RET_DK))
    f32 = lambda t: np.ascontiguousarray(t, dtype=np.float32)
    return f32(mask), f32(xi_t), f32(zeta_t), f32(gc)


def _retention_prompt(z, nbatch, seqlen):
    chunk = RET_CHUNK
    hb = 4
    cos, sn, sp = _rope_tables(np.arange(seqlen))
    mask, xi, zeta, gc = _decay_tables(chunk, 1)
    tab = lambda: pl.BlockSpec((seqlen, RET_DK), lambda b, h: (0, 0))
    head_tab = lambda w: pl.BlockSpec((hb, chunk, w), lambda b, h: (h, 0, 0))
    kern = functools.partial(_ret_prompt_kernel, chunk=chunk, nchunks=seqlen // chunk, hb=hb)
    qk_w, v_w = hb * RET_DK, hb * RET_DV
    return pl.pallas_call(
        kern,
        grid=(nbatch, RET_HEADS // hb),
        in_specs=[
            pl.BlockSpec(memory_space=pltpu.SMEM),
            pl.BlockSpec((seqlen, qk_w), lambda b, h: (b, Z_Q // qk_w + h)),
            pl.BlockSpec((seqlen, qk_w), lambda b, h: (b, Z_K // qk_w + h)),
            pl.BlockSpec((seqlen, v_w), lambda b, h: (b, Z_V // v_w + h)),
            pl.BlockSpec((seqlen, v_w), lambda b, h: (b, Z_GRET // v_w + h)),
            tab(), tab(), tab(),
            head_tab(chunk), head_tab(RET_DK), head_tab(RET_DK),
        ],
        out_specs=[
            pl.BlockSpec((seqlen, v_w), lambda b, h: (b, h)),
            pl.BlockSpec((None, hb, RET_DK, RET_DV), lambda b, h: (b, h, 0, 0)),
        ],
        out_shape=[
            jax.ShapeDtypeStruct((nbatch * seqlen, RET_V), BF16),
            jax.ShapeDtypeStruct((nbatch, RET_HEADS, RET_DK, RET_DV), F32),
        ],
        scratch_shapes=[pltpu.VMEM((hb, RET_DK, RET_DV), F32)],
        compiler_params=_cparams("parallel", "parallel"),
        name="retention_prompt",
    )(gc, z, z, z, z, cos, sn, sp, mask, xi, zeta)


def _retention_sample(z, s0, row_off, nbatch, slen, pos0):
    nseq = 16
    rows = nseq * slen
    cos, sn, sp = (np.tile(t, (nseq, 1)) for t in _rope_tables(pos0 + np.arange(slen)))
    mask, xi, zeta, gc = _decay_tables(slen, nseq)
    rb = row_off // rows
    tab = lambda: pl.BlockSpec((rows, RET_DK), lambda i, h: (0, 0))
    head_tab = lambda w: pl.BlockSpec((None, rows, w), lambda i, h: (h, 0, 0))
    kern = functools.partial(_ret_sample_kernel, slen=slen, nseq=nseq)
    return pl.pallas_call(
        kern,
        grid=(nbatch // nseq, RET_HEADS),
        in_specs=[
            pl.BlockSpec(memory_space=pltpu.SMEM),
            pl.BlockSpec((rows, RET_DK), lambda i, h: (rb + i, Z_Q // RET_DK + h)),
            pl.BlockSpec((rows, RET_DK), lambda i, h: (rb + i, Z_K // RET_DK + h)),
            pl.BlockSpec((rows, RET_DV), lambda i, h: (rb + i, Z_V // RET_DV + h)),
            pl.BlockSpec((rows, RET_DV), lambda i, h: (rb + i, Z_GRET // RET_DV + h)),
            tab(), tab(), tab(),
            head_tab(rows), head_tab(RET_DK), head_tab(RET_DK),
            pl.BlockSpec((nseq, None, RET_DK, RET_DV), lambda i, h: (i, h, 0, 0)),
        ],
        out_specs=[
            pl.BlockSpec((rows, RET_DV), lambda i, h: (i, h)),
            pl.BlockSpec((nseq, None, RET_DK, RET_DV), lambda i, h: (i, h, 0, 0)),
        ],
        out_shape=[
            jax.ShapeDtypeStruct((nbatch * slen, RET_V), BF16),
            jax.ShapeDtypeStruct((nbatch, RET_HEADS, RET_DK, RET_DV), F32),
        ],
        compiler_params=_cparams("parallel", "parallel"),
        name="retention_sample",
    )(gc, z, z, z, z, cos, sn, sp, mask, xi, zeta, s0)


def _s5_prep_kernel(are_ref, aim_ref, dt_ref, are_w_ref, aim_w_ref, dt_w_ref, bre_ref, bim_ref, cim_ref,
                    pre_ref, pim_ref, qre_ref, qim_ref, bbre_ref, bbim_ref, ncim_ref):
    def abar(ar, ai, dt):
        mag = jnp.exp(dt * ar)
        return mag * jnp.cos(dt * ai), mag * jnp.sin(dt * ai)

    def powers(ar, ai, re_ref, im_ref):
        pr, pi = ar, ai
        re_ref[0] = pr
        im_ref[0] = pi
        for t in range(1, SUBLANES):
            pr, pi = pr * ar - pi * ai, pr * ai + pi * ar
            re_ref[t] = pr
            im_ref[t] = pi
        return pr, pi

    ar, ai = abar(are_ref[...], aim_ref[...], dt_ref[...])
    ar8, ai8 = powers(ar, ai, pre_ref, pim_ref)
    powers(ar8, ai8, qre_ref, qim_ref)

    a_r, a_i = are_w_ref[...], aim_w_ref[...]
    w_r, w_i = abar(a_r, a_i, dt_w_ref[...])
    den = a_r * a_r + a_i * a_i
    x_re = w_r - 1.0
    f_re = (x_re * a_r + w_i * a_i) / den
    f_im = (w_i * a_r - x_re * a_i) / den
    br, bi = bre_ref[...], bim_ref[...]
    bbre_ref[...] = f_re * br - f_im * bi
    bbim_ref[...] = f_re * bi + f_im * br
    ncim_ref[...] = -cim_ref[...]


def _s5_prepare(a_re, a_im, log_step, b_re, b_im, c_re, c_im):
    g, n, p = S5_GROUPS, S5_STATE, S5_GROUP
    dt = jnp.broadcast_to(jnp.exp(log_step.astype(F32))[:, None], (g, n))
    wide = lambda t: jnp.repeat(t, p, axis=1)
    vm = lambda: pl.BlockSpec(memory_space=pltpu.VMEM)
    pre, pim, qre, qim, bbre, bbim, ncim = pl.pallas_call(
        _s5_prep_kernel,
        in_specs=[vm() for _ in range(9)],
        out_specs=[vm() for _ in range(7)],
        out_shape=[
            jax.ShapeDtypeStruct((SUBLANES, g, n), F32),
            jax.ShapeDtypeStruct((SUBLANES, g, n), F32),
            jax.ShapeDtypeStruct((SUBLANES, g, n), F32),
            jax.ShapeDtypeStruct((SUBLANES, g, n), F32),
            jax.ShapeDtypeStruct((g, n * p), F32),
            jax.ShapeDtypeStruct((g, n * p), F32),
            jax.ShapeDtypeStruct((g, p * n), F32),
        ],
        name="s5_discretize",
    )(a_re.astype(F32), a_im.astype(F32), dt, wide(a_re.astype(F32)), wide(a_im.astype(F32)), wide(dt),
      b_re.astype(F32).reshape(g, n * p), b_im.astype(F32).reshape(g, n * p), c_im.astype(F32).reshape(g, p * n))

    nb, bg = S5_NBLK, S5_BLK_GROUPS
    eye = jnp.eye(bg, dtype=bool)

    def in_blockdiag(t):
        t = t.reshape(nb, bg, n, p).transpose(0, 1, 3, 2)
        return jnp.where(eye[None, :, None, :, None], t[:, :, :, None, :], 0.0).reshape(nb, bg * p, bg * n)

    def out_blockdiag(t):
        t = t.reshape(nb, bg, p, n).transpose(0, 1, 3, 2)
        return jnp.where(eye[None, :, None, :, None], t[:, :, :, None, :], 0.0).reshape(nb, bg * n, bg * p)

    w_in = jnp.concatenate([in_blockdiag(bbre), in_blockdiag(bbim)], axis=-1).astype(BF16)
    w_out = jnp.concatenate([out_blockdiag(c_re.astype(F32).reshape(g, p * n)), out_blockdiag(ncim)],
                            axis=1).astype(BF16)

    blk = lambda t: t.reshape(SUBLANES, nb, S5_BLK_STATE).transpose(1, 0, 2)
    pw_re, pw_im, qw_re, qw_im = blk(pre), blk(pim), blk(qre), blk(qim)
    every_row = lambda t, i: jnp.broadcast_to(t[:, i:i + 1, :], (nb, SUBLANES, S5_BLK_STATE))
    k_idx = jnp.arange(SUBLANES)[None, :, None]
    tabs = [every_row(pw_re, 0), every_row(pw_im, 0)]
    for i in range(SUBLANES):
        tabs += [every_row(pw_re, i), every_row(pw_im, i)]
    for lvl in SCAN_LEVELS:
        tabs += [jnp.where(k_idx >= lvl, qw_re[:, lvl - 1:lvl, :], 0.0), jnp.where(k_idx >= lvl, qw_im[:, lvl - 1:lvl, :], 0.0)]
    tabs += [qw_re, qw_im]
    return w_in, w_out, jnp.stack(tabs, axis=1)


T_A = 0
T_POW = 2
T_LVL = T_POW + 2 * SUBLANES
T_CARRY = T_LVL + 2 * len(SCAN_LEVELS)
N_TABS = T_CARRY + 2
S5_SUB = SUBLANES * SUBLANES


def _segment_scan(er, ei, tabs_ref, cr, ci):
    for l, lvl in enumerate(SCAN_LEVELS):
        pr = tabs_ref[T_LVL + 2 * l]
        pi = tabs_ref[T_LVL + 2 * l + 1]
        sr = pltpu.roll(er, lvl, 0)
        si = pltpu.roll(ei, lvl, 0)
        er, ei = er + pr * sr - pi * si, ei + pr * si + pi * sr
    rr = tabs_ref[T_CARRY]
    ri = tabs_ref[T_CARRY + 1]
    crb = jnp.broadcast_to(cr, er.shape)
    cib = jnp.broadcast_to(ci, ei.shape)
    return er + rr * crb - ri * cib, ei + rr * cib + ri * crb


def _transpose_tiles(ref, nsub):
    return jnp.concatenate([ref[pl.ds(S5_SUB * m + i, SUBLANES, stride=SUBLANES), :]
                            for m in range(nsub) for i in range(SUBLANES)], axis=0)


def _s5_stream(u_ref, win_ref, wout_ref, d_ref, tabs_ref, uf_scr, bu_scr, h_scr, y_scr, init, long_seq):
    tc = u_ref.shape[0]
    nsub = tc // S5_SUB
    ns = S5_BLK_STATE
    uf = u_ref[...].astype(F32)
    uf_scr[...] = uf
    up = _transpose_tiles(uf_scr, nsub).astype(BF16)
    bu_scr[...] = jnp.dot(up, win_ref[...], preferred_element_type=F32)
    ar = tabs_ref[T_A]
    ai = tabs_ref[T_A + 1]
    finals = []
    for m in range(nsub):
        tile = lambda i: slice(S5_SUB * m + SUBLANES * i, S5_SUB * m + SUBLANES * (i + 1))
        if long_seq:
            hr = hi = None
        else:
            hr = init[0][SUBLANES * m:SUBLANES * (m + 1), :]
            hi = init[1][SUBLANES * m:SUBLANES * (m + 1), :]
        for i in range(SUBLANES):
            xr = bu_scr[tile(i), :ns]
            xi = bu_scr[tile(i), ns:]
            if hr is None:
                hr, hi = xr, xi
            else:
                hr, hi = ar * hr - ai * hi + xr, ar * hi + ai * hr + xi
            h_scr[tile(i), :ns] = hr
            h_scr[tile(i), ns:] = hi
        if long_seq:
            cr, ci = init
            fr, fi = _segment_scan(hr, hi, tabs_ref, cr, ci)
            first = lax.broadcasted_iota(jnp.int32, fr.shape, 0) == 0
            sr = jnp.where(first, jnp.broadcast_to(cr, fr.shape), pltpu.roll(fr, 1, 0))
            si = jnp.where(first, jnp.broadcast_to(ci, fi.shape), pltpu.roll(fi, 1, 0))
            init = (fr[SUBLANES - 1:, :], fi[SUBLANES - 1:, :])
            for i in range(SUBLANES):
                pr = tabs_ref[T_POW + 2 * i]
                pi = tabs_ref[T_POW + 2 * i + 1]
                h_scr[tile(i), :ns] = h_scr[tile(i), :ns] + pr * sr - pi * si
                h_scr[tile(i), ns:] = h_scr[tile(i), ns:] + pr * si + pi * sr
        else:
            finals.append((hr, hi))
    y_scr[...] = jnp.dot(h_scr[...].astype(BF16), wout_ref[...], preferred_element_type=F32)
    y = _transpose_tiles(y_scr, nsub) + d_ref[...] * uf
    return jax.nn.gelu(y).astype(BF16), (init if long_seq else finals)


def _s5_prompt_kernel(*refs, nb):
    u_refs = refs[:nb]
    win_ref, wout_ref, d_ref, tabs_ref, gl_ref, hre_ref, him_ref, uf_scr, bu_scr, h_scr, y_scr, carry_scr = refs[nb:]
    ns = S5_BLK_STATE

    @pl.when(pl.program_id(2) == 0)
    def _():
        carry_scr[...] = jnp.zeros_like(carry_scr)

    for s in range(nb):
        init = (carry_scr[s, 0:1, :ns], carry_scr[s, 0:1, ns:])
        out, (cr, ci) = _s5_stream(u_refs[s], win_ref, wout_ref, d_ref, tabs_ref, uf_scr.at[s], bu_scr.at[s],
                                   h_scr.at[s], y_scr.at[s], init, True)
        gl_ref[s] = out
        carry_scr[s, 0:1, :ns] = cr
        carry_scr[s, 0:1, ns:] = ci
        hre_ref[s] = cr
        him_ref[s] = ci


def _s5_sample_kernel(u_ref, win_ref, wout_ref, d_ref, tabs_ref, h0re_ref, h0im_ref, gl_ref, hre_ref, him_ref,
                      uf_scr, bu_scr, h_scr, y_scr):
    out, finals = _s5_stream(u_ref, win_ref, wout_ref, d_ref, tabs_ref, uf_scr, bu_scr, h_scr, y_scr,
                             (h0re_ref, h0im_ref), False)
    gl_ref[...] = out
    for m, (hr, hi) in enumerate(finals):
        hre_ref[SUBLANES * m:SUBLANES * (m + 1), :] = hr
        him_ref[SUBLANES * m:SUBLANES * (m + 1), :] = hi


def _s5_weight_specs():
    nst = S5_BLK_STATE
    return [
        pl.BlockSpec((None, LANES, 2 * nst), lambda j, *_: (j, 0, 0)),
        pl.BlockSpec((None, 2 * nst, LANES), lambda j, *_: (j, 0, 0)),
        pl.BlockSpec((1, LANES), lambda j, *_: (0, j)),
        pl.BlockSpec((None, N_TABS, SUBLANES, nst), lambda j, *_: (j, 0, 0, 0)),
    ]


def _s5_scratch(lead, tc):
    nst = S5_BLK_STATE
    return [pltpu.VMEM(lead + (tc, LANES), F32), pltpu.VMEM(lead + (tc, 2 * nst), F32),
            pltpu.VMEM(lead + (tc, 2 * nst), F32), pltpu.VMEM(lead + (tc, LANES), F32)]


def _s5_prompt(z, w_in, w_out, d_skip, tabs, nbatch, seqlen):
    tc = 256
    nb = 2
    nt = seqlen // tc
    nst = S5_BLK_STATE
    kern = functools.partial(_s5_prompt_kernel, nb=nb)
    u_spec = lambda s: pl.BlockSpec((tc, LANES), lambda j, bb, t: ((bb * nb + s) * nt + t, Z_U // LANES + j))
    state_spec = lambda: pl.BlockSpec((nb, 1, nst), lambda j, bb, t: (bb, 0, j))
    gl, hre, him = pl.pallas_call(
        kern,
        grid=(S5_NBLK, nbatch // nb, nt),
        in_specs=[u_spec(s) for s in range(nb)] + _s5_weight_specs(),
        out_specs=[pl.BlockSpec((nb, tc, LANES), lambda j, bb, t: (bb, t, j)), state_spec(), state_spec()],
        out_shape=[
            jax.ShapeDtypeStruct((nbatch, seqlen, S5_WIDTH), BF16),
            jax.ShapeDtypeStruct((nbatch, 1, S5_GROUPS * S5_STATE), F32),
            jax.ShapeDtypeStruct((nbatch, 1, S5_GROUPS * S5_STATE), F32),
        ],
        scratch_shapes=_s5_scratch((nb,), tc) + [pltpu.VMEM((nb, SUBLANES, 2 * nst), F32)],
        compiler_params=_cparams("parallel", "parallel", "arbitrary"),
        name="s5_prompt",
    )(*([z] * nb), w_in, w_out, d_skip, tabs)
    return (gl.reshape(nbatch * seqlen, S5_WIDTH), hre.reshape(nbatch, S5_GROUPS, S5_STATE),
            him.reshape(nbatch, S5_GROUPS, S5_STATE))


def _s5_sample(z, w_in, w_out, d_skip, tabs, h0_re, h0_im, row_off, nbatch, slen):
    assert slen == SUBLANES
    tc = 256
    nseq = tc // slen
    rb = row_off // tc
    nst = S5_BLK_STATE
    state_spec = lambda: pl.BlockSpec((nseq, nst), lambda j, i: (i, j))
    gl, hre, him = pl.pallas_call(
        _s5_sample_kernel,
        grid=(S5_NBLK, nbatch // nseq),
        in_specs=([pl.BlockSpec((tc, LANES), lambda j, i: (rb + i, Z_U // LANES + j))] + _s5_weight_specs()
                  + [state_spec(), state_spec()]),
        out_specs=[pl.BlockSpec((tc, LANES), lambda j, i: (i, j)), state_spec(), state_spec()],
        out_shape=[
            jax.ShapeDtypeStruct((nbatch * slen, S5_WIDTH), BF16),
            jax.ShapeDtypeStruct((nbatch, S5_GROUPS * S5_STATE), F32),
            jax.ShapeDtypeStruct((nbatch, S5_GROUPS * S5_STATE), F32),
        ],
        scratch_shapes=_s5_scratch((), tc),
        compiler_params=_cparams("parallel", "parallel"),
        name="s5_sample",
    )(z, w_in, w_out, d_skip, tabs, h0_re.reshape(nbatch, -1).astype(F32), h0_im.reshape(nbatch, -1).astype(F32))
    return gl, hre.reshape(nbatch, S5_GROUPS, S5_STATE), him.reshape(nbatch, S5_GROUPS, S5_STATE)


def _xattn_kernel(q_ref, g_ref, k_ref, v_ref, o_ref, *, q_per_seq, k_per_seq):
    tq = q_ref.shape[0]
    nk = k_ref.shape[0]
    kb = k_ref[...].astype(BF16)
    vb = v_ref[...].astype(BF16)
    if tq // q_per_seq > 1:
        same = (lax.broadcasted_iota(jnp.int32, (tq, nk), 0) // q_per_seq
                == lax.broadcasted_iota(jnp.int32, (tq, nk), 1) // k_per_seq)
    else:
        same = None
    for h in range(X_HEADS):
        cols = slice(h * X_HD, (h + 1) * X_HD)
        s = lax.dot_general(q_ref[:, cols], kb[:, cols], (((1,), (1,)), ((), ())),
                            preferred_element_type=F32) * (X_HD ** -0.5)
        if same is not None:
            s = jnp.where(same, s, -jnp.inf)
        e = jnp.exp(s - jnp.max(s, axis=-1, keepdims=True))
        oh = jnp.dot(e.astype(BF16), vb[:, cols], preferred_element_type=F32) / jnp.sum(e, axis=-1, keepdims=True)
        g = g_ref[:, cols].astype(F32)
        o_ref[:, cols] = (oh * (g * jax.nn.sigmoid(g))).astype(BF16)


def _xattn(z, mk, mv, *, row_off, nrows, tq, q_per_seq, seqs_per_step, kv_col_blk, name):
    nk = seqs_per_step * MEM_LEN
    steps_per_kv = (seqs_per_step * q_per_seq) // tq if tq < seqs_per_step * q_per_seq else 1
    rb = row_off // tq
    kern = functools.partial(_xattn_kernel, q_per_seq=q_per_seq, k_per_seq=MEM_LEN)
    kmap = lambda cb: (lambda i: (i // steps_per_kv, cb))
    return pl.pallas_call(
        kern,
        grid=(nrows // tq,),
        in_specs=[
            pl.BlockSpec((tq, X_WIDTH), lambda i: (rb + i, Z_QX // X_WIDTH)),
            pl.BlockSpec((tq, X_WIDTH), lambda i: (rb + i, Z_GX // X_WIDTH)),
            pl.BlockSpec((nk, X_WIDTH), kmap(kv_col_blk[0])),
            pl.BlockSpec((nk, X_WIDTH), kmap(kv_col_blk[1])),
        ],
        out_specs=pl.BlockSpec((tq, X_WIDTH), lambda i: (i, 0)),
        out_shape=jax.ShapeDtypeStruct((nrows, X_WIDTH), BF16),
        compiler_params=_cparams("parallel"),
        name=name,
    )(z, z, mk, mv)


def _glu_kernel(x_ref, wa_ref, wb_ref, g_ref, o_ref):
    x = x_ref[...]
    a = jnp.dot(x, wa_ref[...], preferred_element_type=F32)
    b = jnp.dot(x, wb_ref[...], preferred_element_type=F32)
    g = g_ref[...].astype(F32)
    o_ref[...] = (a * jax.nn.sigmoid(b) * (g * jax.nn.sigmoid(g))).astype(o_ref.dtype)


def _glu(gl, w_glu, z, row_off, *, bm, bn):
    m, k = gl.shape
    rb = row_off // bm
    return pl.pallas_call(
        _glu_kernel,
        grid=(S5_WIDTH // bn, m // bm),
        in_specs=[
            pl.BlockSpec((bm, k), lambda j, i: (i, 0)),
            pl.BlockSpec((k, bn), lambda j, i: (0, j)),
            pl.BlockSpec((k, bn), lambda j, i: (0, j + S5_WIDTH // bn)),
            pl.BlockSpec((bm, bn), lambda j, i: (rb + i, Z_GS5 // bn + j)),
        ],
        out_specs=pl.BlockSpec((bm, bn), lambda j, i: (i, j)),
        out_shape=jax.ShapeDtypeStruct((m, S5_WIDTH), BF16),
        compiler_params=_cparams("parallel", "parallel"),
        name="glu",
    )(gl, w_glu, w_glu, z)


def _merge_kernel(oa_ref, ob_ref, oc_ref, wa_ref, wb_ref, wc_ref, ma_ref, mb_ref, mc_ref, o_ref):
    def branch(o_r, w_r, m_r):
        return jax.nn.sigmoid(m_r[...].astype(F32)) * jnp.dot(o_r[...], w_r[...], preferred_element_type=F32)

    o_ref[...] = (branch(oa_ref, wa_ref, ma_ref) + branch(ob_ref, wb_ref, mb_ref)
                  + branch(oc_ref, wc_ref, mc_ref)).astype(o_ref.dtype)


def _merge(o_a, o_b, o_c, w_a, w_b, w_c, z, row_off, *, bm, bn):
    m = o_a.shape[0]
    rb = row_off // bm
    lhs = lambda t: pl.BlockSpec((bm, t.shape[1]), lambda j, i: (i, 0))
    rhs = lambda t: pl.BlockSpec((t.shape[0], bn), lambda j, i: (0, j))
    gate = lambda off: pl.BlockSpec((bm, bn), lambda j, i: (rb + i, off // bn + j))
    return pl.pallas_call(
        _merge_kernel,
        grid=(D_MODEL // bn, m // bm),
        in_specs=[lhs(o_a), lhs(o_b), lhs(o_c), rhs(w_a), rhs(w_b), rhs(w_c), gate(Z_MA), gate(Z_MB), gate(Z_MC)],
        out_specs=pl.BlockSpec((bm, bn), lambda j, i: (i, j)),
        out_shape=jax.ShapeDtypeStruct((m, D_MODEL), BF16),
        compiler_params=_cparams("parallel", "parallel"),
        name="merge",
    )(o_a, o_b, o_c, w_a, w_b, w_c, z, z, z)


def _out_ln_kernel(m_ref, w_ref, x_ref, g_ref, b_ref, o_ref, pre_scr, *, nj, bn):
    j = pl.program_id(1)
    pre_scr[j] = DN_ALPHA * x_ref[...] + jnp.dot(m_ref[...], w_ref[...], preferred_element_type=F32)

    @pl.when(j == nj - 1)
    def _():
        width = nj * bn
        tot = pre_scr[0].sum(axis=-1, keepdims=True)
        for t in range(1, nj):
            tot = tot + pre_scr[t].sum(axis=-1, keepdims=True)
        mu = tot / width
        sq = None
        for t in range(nj):
            d = pre_scr[t] - mu
            part = (d * d).sum(axis=-1, keepdims=True)
            sq = part if sq is None else sq + part
        rstd = lax.rsqrt(sq / width + LN_EPS)
        for t in range(nj):
            cols = slice(t * bn, (t + 1) * bn)
            o_ref[:, cols] = (pre_scr[t] - mu) * rstd * g_ref[:, cols] + b_ref[:, cols]


def _out_ln(merged, w_out, x, ln_g, ln_b, *, bm, bn):
    m, k = merged.shape
    nj = D_MODEL // bn
    kern = functools.partial(_out_ln_kernel, nj=nj, bn=bn)
    return pl.pallas_call(
        kern,
        grid=(m // bm, nj),
        in_specs=[
            pl.BlockSpec((bm, k), lambda i, j: (i, 0)),
            pl.BlockSpec((k, bn), lambda i, j: (0, j)),
            pl.BlockSpec((bm, bn), lambda i, j: (i, j)),
            pl.BlockSpec((1, D_MODEL), lambda i, j: (0, 0)),
            pl.BlockSpec((1, D_MODEL), lambda i, j: (0, 0)),
        ],
        out_specs=pl.BlockSpec((bm, D_MODEL), lambda i, j: (i, 0)),
        out_shape=jax.ShapeDtypeStruct((m, D_MODEL), F32),
        scratch_shapes=[pltpu.VMEM((nj, bm, bn), F32)],
        compiler_params=_cparams("parallel", "arbitrary"),
        name="out_ln",
    )(merged, w_out, x, ln_g.reshape(1, D_MODEL).astype(F32), ln_b.reshape(1, D_MODEL).astype(F32))


def _group_tail(z, row_off, o_ret, gl, o_x, x2d, w):
    o_s5 = _glu(gl, w["glu"], z, row_off, bm=512, bn=512)
    merged = _merge(o_ret, o_s5, o_x, w["proj_a"], w["proj_b"], w["proj_c"], z, row_off, bm=512, bn=512)
    return _out_ln(merged, w["out"], x2d, w["ln_g"], w["ln_b"], bm=512, bn=512)


def kernel(x_prompt, x_sample, mem_prompt, state_ret, state_s5_re, state_s5_im, cache_mem_k, cache_mem_v, w_in, w_mem_kv, s5_a_re, s5_a_im, s5_log_step, s5_b_re, s5_b_im, s5_c_re, s5_c_im, s5_d, w_glu, w_proj_a, w_proj_b, w_proj_c, w_out, ln_g, ln_b):
    depth = w_in.shape[0]
    assert depth == 1
    l = 0
    n_p = BATCH * SEQ
    n_s = DEC_BATCH * DEC_SEQ
    xp2 = x_prompt.reshape(n_p, D_MODEL)
    xs2 = x_sample.reshape(n_s, D_MODEL)

    xb = jnp.concatenate([xp2, xs2], axis=0).astype(BF16)
    w = dict(glu=w_glu[l].astype(BF16), proj_a=w_proj_a[l].astype(BF16), proj_b=w_proj_b[l].astype(BF16),
             proj_c=w_proj_c[l].astype(BF16), out=w_out[l].astype(BF16), ln_g=ln_g[l], ln_b=ln_b[l])

    z = _matmul(xb, w_in[l], bm=1024, bn=512, out_dtype=BF16, n_out=IN_WIDTH, name="in_proj")

    memb = mem_prompt.reshape(BATCH * MEM_LEN, D_MODEL).astype(BF16)
    mk = _matmul(memb, w_mem_kv[l], bm=1024, bn=512, out_dtype=F32, n_out=X_WIDTH, col_blk_off=0, name="mem_k")
    mv = _matmul(memb, w_mem_kv[l], bm=1024, bn=512, out_dtype=F32, n_out=X_WIDTH, col_blk_off=X_WIDTH // 512,
                 name="mem_v")

    s5_win, s5_wout, s5_tabs = _s5_prepare(s5_a_re[l], s5_a_im[l], s5_log_step[l], s5_b_re[l], s5_b_im[l],
                                           s5_c_re[l], s5_c_im[l])
    d_skip = s5_d[l].reshape(1, S5_WIDTH).astype(F32)

    o_ret_p, ret_p = _retention_prompt(z, BATCH, SEQ)
    gl_p, hre_p, him_p = _s5_prompt(z, s5_win, s5_wout, d_skip, s5_tabs, BATCH, SEQ)
    o_x_p = _xattn(z, mk, mv, row_off=0, nrows=n_p, tq=512, q_per_seq=SEQ, seqs_per_step=1,
                   kv_col_blk=(0, 0), name="xattn_prompt")
    y_p = _group_tail(z, 0, o_ret_p, gl_p, o_x_p, xp2, w)

    o_ret_s, ret_s = _retention_sample(z, state_ret[l], n_p, DEC_BATCH, DEC_SEQ, PAST_LEN)
    gl_s, hre_s, him_s = _s5_sample(z, s5_win, s5_wout, d_skip, s5_tabs, state_s5_re[l], state_s5_im[l],
                                    n_p, DEC_BATCH, DEC_SEQ)
    ck = cache_mem_k[l].reshape(DEC_BATCH * MEM_LEN, X_WIDTH)
    cv = cache_mem_v[l].reshape(DEC_BATCH * MEM_LEN, X_WIDTH)
    o_x_s = _xattn(z, ck, cv, row_off=n_p, nrows=n_s, tq=2 * DEC_SEQ, q_per_seq=DEC_SEQ, seqs_per_step=2,
                   kv_col_blk=(0, 0), name="xattn_sample")
    y_s = _group_tail(z, n_p, o_ret_s, gl_s, o_x_s, xs2, w)

    return (y_p.reshape(BATCH, SEQ, D_MODEL), y_s.reshape(DEC_BATCH, DEC_SEQ, D_MODEL),
            ret_p[None], hre_p[None], him_p[None],
            mk.reshape(1, BATCH, MEM_LEN, X_HEADS, X_HD), mv.reshape(1, BATCH, MEM_LEN, X_HEADS, X_HD),
            ret_s[None], hre_s[None], him_s[None])
```

```python
import functools
import math

import jax
import jax.numpy as jnp
import numpy as np
from jax import lax
from jax.experimental import pallas as pl
from jax.experimental.pallas import tpu as pltpu

F32 = jnp.float32
BF16 = jnp.bfloat16

D_MODEL = 4096
BATCH = 4
SEQ = 2048
DEC_BATCH = 128
DEC_SEQ = 8
PAST_LEN = 16384

RET_HEADS = 16
RET_DK = 128
RET_DV = 256
RET_QK = RET_HEADS * RET_DK
RET_V = RET_HEADS * RET_DV
RET_CHUNK = 128
ROPE_BASE = 10000.0

S5_WIDTH = D_MODEL // 2
S5_GROUP = 16
S5_GROUPS = S5_WIDTH // S5_GROUP
S5_STATE = 64

X_HEADS = 4
X_WIDTH = D_MODEL // 2
X_HD = X_WIDTH // X_HEADS
MEM_LEN = 256

DN_ALPHA = 2.0 ** 0.25
LN_EPS = 1e-5
GN_EPS = 1e-5

IN_WIDTH = 2 * RET_QK + 2 * RET_V + 2 * S5_WIDTH + 2 * X_WIDTH + 3 * D_MODEL

Z_Q = 0
Z_K = Z_Q + RET_QK
Z_V = Z_K + RET_QK
Z_GRET = Z_V + RET_V
Z_U = Z_GRET + RET_V
Z_GS5 = Z_U + S5_WIDTH
Z_QX = Z_GS5 + S5_WIDTH
Z_GX = Z_QX + X_WIDTH
Z_MA = Z_GX + X_WIDTH
Z_MB = Z_MA + D_MODEL
Z_MC = Z_MB + D_MODEL

SUBLANES = 8
LANES = 128
VMEM_PHYSICAL_BYTES = 64 * 1024 * 1024
VMEM_LIMIT_BYTES = 56 * 1024 * 1024

S5_BLK_GROUPS = LANES // S5_GROUP
S5_BLK_STATE = S5_BLK_GROUPS * S5_STATE
S5_NBLK = S5_GROUPS // S5_BLK_GROUPS
SCAN_LEVELS = (1, 2, 4)
T_A = 0
T_POW = 2
T_LVL = T_POW + 2 * SUBLANES
T_CARRY = T_LVL + 2 * len(SCAN_LEVELS)
N_TABS = T_CARRY + 2


def _cparams(*sem, vmem_limit_bytes=VMEM_LIMIT_BYTES):
    return pltpu.CompilerParams(dimension_semantics=sem, vmem_limit_bytes=vmem_limit_bytes)


def _concat_cast_kernel(a_ref, b_ref, o_ref, *, na):
    i = pl.program_id(0)

    @pl.when(i < na)
    def _():
        o_ref[...] = a_ref[...].astype(o_ref.dtype)

    @pl.when(i >= na)
    def _():
        o_ref[...] = b_ref[...].astype(o_ref.dtype)


def _concat_cast(a, b, *, bm, out_dtype):
    k = a.shape[1]
    na, nb = a.shape[0] // bm, b.shape[0] // bm
    return pl.pallas_call(
        functools.partial(_concat_cast_kernel, na=na),
        grid=(na + nb,),
        in_specs=[
            pl.BlockSpec((bm, k), lambda i: (jnp.minimum(i, na - 1), 0)),
            pl.BlockSpec((bm, k), lambda i: (jnp.maximum(i - na, 0), 0)),
        ],
        out_specs=pl.BlockSpec((bm, k), lambda i: (i, 0)),
        out_shape=jax.ShapeDtypeStruct((a.shape[0] + b.shape[0], k), out_dtype),
        compiler_params=_cparams("parallel"),
        name="concat_cast",
    )(a, b)


def _mm_kernel(x_ref, w_ref, o_ref, wb_scr):
    @pl.when(pl.program_id(1) == 0)
    def _():
        wb_scr[...] = w_ref[...].astype(BF16)

    o_ref[...] = jnp.dot(x_ref[...], wb_scr[...], preferred_element_type=F32).astype(o_ref.dtype)


def _matmul(x, w, *, bm, bn, out_dtype, n_out, col_blk_off=0, name):
    m, k = x.shape
    need = (2 * k * bn * 4 + k * bn * 2 + 2 * bm * k * x.dtype.itemsize
            + 2 * bm * bn * jnp.dtype(out_dtype).itemsize + bm * bn * 4)
    limit = max(VMEM_LIMIT_BYTES, need + (2 << 20))
    assert limit <= VMEM_PHYSICAL_BYTES - (2 << 20), (need, limit)
    return pl.pallas_call(
        _mm_kernel,
        grid=(n_out // bn, m // bm),
        in_specs=[
            pl.BlockSpec((bm, k), lambda j, i: (i, 0)),
            pl.BlockSpec((k, bn), lambda j, i: (0, j + col_blk_off)),
        ],
        out_specs=pl.BlockSpec((bm, bn), lambda j, i: (i, j)),
        out_shape=jax.ShapeDtypeStruct((m, n_out), out_dtype),
        scratch_shapes=[pltpu.VMEM((k, bn), BF16)],
        compiler_params=_cparams("parallel", "arbitrary", vmem_limit_bytes=limit),
        name=name,
    )(x, w)


def _rotate(x, cos, sin_next, sin_prev):
    return x * cos + pltpu.roll(x, LANES - 1, 1) * sin_next + pltpu.roll(x, 1, 1) * sin_prev


def _ret_block(q, k, v, g, cos, sin_next, sin_prev, mask, xi, zeta, gc, states, slen):
    nseq = len(states)
    rows = q.shape[0]
    qr = _rotate(q, cos, sin_next, sin_prev)
    kr = _rotate(k, cos, sin_next, sin_prev) * (RET_DK ** -0.5)
    qb = qr.astype(BF16)
    kb = kr.astype(BF16)
    sc = lax.dot_general(qb, kb, (((1,), (1,)), ((), ())), preferred_element_type=F32) * mask
    o = jnp.dot(sc.astype(BF16), v, preferred_element_type=F32)
    qx = qr * xi
    kzt = (kr * zeta).T.astype(BF16)
    new_states = []
    if nseq == 1:
        s = states[0]
        o = o + jnp.dot(qx.astype(BF16), s.astype(BF16), preferred_element_type=F32)
        new_states.append(gc * s + jnp.dot(kzt, v, preferred_element_type=F32))
    else:
        pair = 2 * slen
        assert pair == 2 * SUBLANES and nseq % 2 == 0
        row_in_pair = lax.broadcasted_iota(jnp.int32, (pair, RET_DV), 0)
        row_seq = lax.broadcasted_iota(jnp.int32, (rows, RET_DV), 0) // slen
        parts = []
        for m in range(nseq // 2):
            qpair = qx[m * pair:(m + 1) * pair].astype(BF16)
            o0 = jnp.dot(qpair, states[2 * m].astype(BF16), preferred_element_type=F32)
            o1 = jnp.dot(qpair, states[2 * m + 1].astype(BF16), preferred_element_type=F32)
            parts.append(jnp.where(row_in_pair < slen, o0, o1))
        o = o + jnp.concatenate(parts, axis=0)
        vf = v.astype(F32)
        for n in range(nseq):
            vn = jnp.where(row_seq == n, vf, 0.0).astype(BF16)
            new_states.append(gc * states[n] + jnp.dot(kzt, vn, preferred_element_type=F32))
    mu = jnp.mean(o, axis=-1, keepdims=True)
    d = o - mu
    var = jnp.mean(d * d, axis=-1, keepdims=True)
    on = d * lax.rsqrt(var + GN_EPS)
    out = (on * (g * jax.nn.sigmoid(g))).astype(BF16)
    return out, new_states


def _ret_prompt_kernel(gc_ref, q_ref, k_ref, v_ref, g_ref, cos_ref, sn_ref, sp_ref, mask_ref, xi_ref, zeta_ref,
                       o_ref, sfin_ref, s_scr, *, chunk, nchunks, hb):
    head0 = pl.program_id(1) * hb
    s_scr[...] = jnp.zeros_like(s_scr)

    def body(c, carry):
        rows = pl.ds(pl.multiple_of(c * chunk, chunk), chunk)
        cos, sn, sp = cos_ref[rows, :], sn_ref[rows, :], sp_ref[rows, :]
        for hh in range(hb):
            qc = slice(hh * RET_DK, (hh + 1) * RET_DK)
            vc = slice(hh * RET_DV, (hh + 1) * RET_DV)
            out, (s_new,) = _ret_block(
                q_ref[rows, qc].astype(F32), k_ref[rows, qc].astype(F32), v_ref[rows, vc],
                g_ref[rows, vc].astype(F32), cos, sn, sp, mask_ref[hh], xi_ref[hh], zeta_ref[hh],
                gc_ref[head0 + hh], [s_scr[hh]], chunk)
            o_ref[rows, vc] = out
            s_scr[hh] = s_new
        return carry

    lax.fori_loop(0, nchunks, body, 0)
    sfin_ref[...] = s_scr[...]


def _ret_sample_kernel(gc_ref, q_ref, k_ref, v_ref, g_ref, cos_ref, sn_ref, sp_ref, mask_ref, xi_ref, zeta_ref,
                       s0_ref, o_ref, sfin_ref, *, slen, nseq):
    gc = gc_ref[pl.program_id(1)]
    states = [s0_ref[n] for n in range(nseq)]
    out, new_states = _ret_block(
        q_ref[...].astype(F32), k_ref[...].astype(F32), v_ref[...], g_ref[...].astype(F32),
        cos_ref[...], sn_ref[...], sp_ref[...], mask_ref[...], xi_ref[...], zeta_ref[...], gc, states, slen)
    o_ref[...] = out
    for n in range(nseq):
        sfin_ref[n] = new_states[n]


def _rope_tables(pos):
    half = RET_DK // 2
    inv = 1.0 / (ROPE_BASE ** (np.arange(half, dtype=np.float64) / half))
    ang = np.asarray(pos, np.float64)[:, None] * inv[None, :]
    cos = np.repeat(np.cos(ang), 2, axis=1)
    sin = np.repeat(np.sin(ang), 2, axis=1)
    even = (np.arange(RET_DK) % 2) == 0
    return (np.asarray(cos, np.float32), np.asarray(np.where(even, -sin, 0.0), np.float32),
            np.asarray(np.where(even, 0.0, sin), np.float32))


def _decay_tables(slen, nseq):
    lg = np.log1p(-np.exp2(-5.0 - np.arange(RET_HEADS, dtype=np.float64)))
    idx = np.arange(slen, dtype=np.float64)
    rel = idx[:, None] - idx[None, :]
    inner = np.where(rel[None] >= 0, np.exp(lg[:, None, None] * np.maximum(rel, 0.0)[None]), 0.0)
    xi = np.exp(lg[:, None] * (idx + 1.0))
    zeta = np.exp(lg[:, None] * (slen - 1.0 - idx))
    gc = np.exp(lg * slen)
    mask = np.einsum("nm,hij->hnimj", np.eye(nseq), inner).reshape(RET_HEADS, nseq * slen, nseq * slen)
    rows = nseq * slen
    xi_t = np.broadcast_to(np.tile(xi, (1, nseq))[:, :, None], (RET_HEADS, rows, RET_DK))
    zeta_t = np.broadcast_to(np.tile(zeta, (1, nseq))[:, :, None], (RET_HEADS, rows, RET_DK))
    f32 = lambda t: np.ascontiguousarray(t, dtype=np.float32)
    return f32(mask), f32(xi_t), f32(zeta_t), f32(gc)


def _retention_prompt(z, nbatch, seqlen):
    chunk = RET_CHUNK
    hb = 4
    cos, sn, sp = _rope_tables(np.arange(seqlen))
    mask, xi, zeta, gc = _decay_tables(chunk, 1)
    tab = lambda: pl.BlockSpec((seqlen, RET_DK), lambda b, h: (0, 0))
    head_tab = lambda w: pl.BlockSpec((hb, chunk, w), lambda b, h: (h, 0, 0))
    kern = functools.partial(_ret_prompt_kernel, chunk=chunk, nchunks=seqlen // chunk, hb=hb)
    qk_w, v_w = hb * RET_DK, hb * RET_DV
    return pl.pallas_call(
        kern,
        grid=(nbatch, RET_HEADS // hb),
        in_specs=[
            pl.BlockSpec(memory_space=pltpu.SMEM),
            pl.BlockSpec((seqlen, qk_w), lambda b, h: (b, Z_Q // qk_w + h)),
            pl.BlockSpec((seqlen, qk_w), lambda b, h: (b, Z_K // qk_w + h)),
            pl.BlockSpec((seqlen, v_w), lambda b, h: (b, Z_V // v_w + h)),
            pl.BlockSpec((seqlen, v_w), lambda b, h: (b, Z_GRET // v_w + h)),
            tab(), tab(), tab(),
            head_tab(chunk), head_tab(RET_DK), head_tab(RET_DK),
        ],
        out_specs=[
            pl.BlockSpec((seqlen, v_w), lambda b, h: (b, h)),
            pl.BlockSpec((None, hb, RET_DK, RET_DV), lambda b, h: (b, h, 0, 0)),
        ],
        out_shape=[
            jax.ShapeDtypeStruct((nbatch * seqlen, RET_V), BF16),
            jax.ShapeDtypeStruct((nbatch, RET_HEADS, RET_DK, RET_DV), F32),
        ],
        scratch_shapes=[pltpu.VMEM((hb, RET_DK, RET_DV), F32)],
        compiler_params=_cparams("parallel", "parallel"),
        name="retention_prompt",
    )(gc, z, z, z, z, cos, sn, sp, mask, xi, zeta)


def _retention_sample(z, s0, row_off, nbatch, slen, pos0):
    nseq = 32
    rows = nseq * slen
    cos, sn, sp = (np.tile(t, (nseq, 1)) for t in _rope_tables(pos0 + np.arange(slen)))
    mask, xi, zeta, gc = _decay_tables(slen, nseq)
    rb = row_off // rows
    tab = lambda: pl.BlockSpec((rows, RET_DK), lambda i, h: (0, 0))
    head_tab = lambda w: pl.BlockSpec((None, rows, w), lambda i, h: (h, 0, 0))
    kern = functools.partial(_ret_sample_kernel, slen=slen, nseq=nseq)
    return pl.pallas_call(
        kern,
        grid=(nbatch // nseq, RET_HEADS),
        in_specs=[
            pl.BlockSpec(memory_space=pltpu.SMEM),
            pl.BlockSpec((rows, RET_DK), lambda i, h: (rb + i, Z_Q // RET_DK + h)),
            pl.BlockSpec((rows, RET_DK), lambda i, h: (rb + i, Z_K // RET_DK + h)),
            pl.BlockSpec((rows, RET_DV), lambda i, h: (rb + i, Z_V // RET_DV + h)),
            pl.BlockSpec((rows, RET_DV), lambda i, h: (rb + i, Z_GRET // RET_DV + h)),
            tab(), tab(), tab(),
            head_tab(rows), head_tab(RET_DK), head_tab(RET_DK),
            pl.BlockSpec((nseq, None, RET_DK, RET_DV), lambda i, h: (i, h, 0, 0)),
        ],
        out_specs=[
            pl.BlockSpec((rows, RET_DV), lambda i, h: (i, h)),
            pl.BlockSpec((nseq, None, RET_DK, RET_DV), lambda i, h: (i, h, 0, 0)),
        ],
        out_shape=[
            jax.ShapeDtypeStruct((nbatch * slen, RET_V), BF16),
            jax.ShapeDtypeStruct((nbatch, RET_HEADS, RET_DK, RET_DV), F32),
        ],
        compiler_params=_cparams("parallel", "parallel"),
        name="retention_sample",
    )(gc, z, z, z, z, cos, sn, sp, mask, xi, zeta, s0)


def _s5_prep_kernel(are_ref, aim_ref, dt_ref, are_w_ref, aim_w_ref, dt_w_ref, bre_ref, bim_ref, cim_ref,
                    tabs_ref, bbre_ref, bbim_ref, ncim_ref):
    def abar(ar, ai, dt):
        mag = jnp.exp(dt * ar)
        return mag * jnp.cos(dt * ai), mag * jnp.sin(dt * ai)

    def powers(ar, ai):
        out = [(ar, ai)]
        for _ in range(1, SUBLANES):
            pr, pi = out[-1]
            out.append((pr * ar - pi * ai, pr * ai + pi * ar))
        return out

    pw = powers(*abar(are_ref[...], aim_ref[...], dt_ref[...]))
    qw = powers(*pw[-1])
    k_idx = lax.broadcasted_iota(jnp.int32, (SUBLANES, S5_BLK_STATE), 0)
    for j in range(S5_NBLK):
        every_row = lambda v: jnp.broadcast_to(v[j:j + 1, :], (SUBLANES, S5_BLK_STATE))
        for c in range(2):
            tabs_ref[j, T_A + c] = every_row(pw[0][c])
            for i in range(SUBLANES):
                tabs_ref[j, T_POW + 2 * i + c] = every_row(pw[i][c])
            for l, lvl in enumerate(SCAN_LEVELS):
                tabs_ref[j, T_LVL + 2 * l + c] = jnp.where(k_idx >= lvl, every_row(qw[lvl - 1][c]), 0.0)
            carry = every_row(qw[0][c])
            for k in range(1, SUBLANES):
                carry = jnp.where(k_idx == k, every_row(qw[k][c]), carry)
            tabs_ref[j, T_CARRY + c] = carry

    a_r, a_i = are_w_ref[...], aim_w_ref[...]
    w_r, w_i = abar(a_r, a_i, dt_w_ref[...])
    den = a_r * a_r + a_i * a_i
    x_re = w_r - 1.0
    f_re = (x_re * a_r + w_i * a_i) / den
    f_im = (w_i * a_r - x_re * a_i) / den
    br, bi = bre_ref[...], bim_ref[...]
    bbre_ref[...] = f_re * br - f_im * bi
    bbim_ref[...] = f_re * bi + f_im * br
    ncim_ref[...] = -cim_ref[...]


def _s5_prepare(a_re, a_im, log_step, b_re, b_im, c_re, c_im):
    g, n, p = S5_GROUPS, S5_STATE, S5_GROUP
    dt = jnp.broadcast_to(jnp.exp(log_step.astype(F32))[:, None], (g, n))
    wide = lambda t: jnp.repeat(t, p, axis=1)
    vm = lambda: pl.BlockSpec(memory_space=pltpu.VMEM)
    nb, bg = S5_NBLK, S5_BLK_GROUPS
    per_blk = lambda t: t.reshape(nb, S5_BLK_STATE)
    tabs, bbre, bbim, ncim = pl.pallas_call(
        _s5_prep_kernel,
        in_specs=[vm() for _ in range(9)],
        out_specs=[vm() for _ in range(4)],
        out_shape=[
            jax.ShapeDtypeStruct((nb, N_TABS, SUBLANES, S5_BLK_STATE), F32),
            jax.ShapeDtypeStruct((g, n * p), F32),
            jax.ShapeDtypeStruct((g, n * p), F32),
            jax.ShapeDtypeStruct((g, p * n), F32),
        ],
        name="s5_discretize",
    )(per_blk(a_re.astype(F32)), per_blk(a_im.astype(F32)), per_blk(dt),
      wide(a_re.astype(F32)), wide(a_im.astype(F32)), wide(dt),
      b_re.astype(F32).reshape(g, n * p), b_im.astype(F32).reshape(g, n * p), c_im.astype(F32).reshape(g, p * n))

    eye = jnp.eye(bg, dtype=bool)

    def in_blockdiag(t):
        t = t.reshape(nb, bg, n, p).transpose(0, 1, 3, 2)
        return jnp.where(eye[None, :, None, :, None], t[:, :, :, None, :], 0.0).reshape(nb, bg * p, bg * n)

    def out_blockdiag(t):
        t = t.reshape(nb, bg, p, n).transpose(0, 1, 3, 2)
        return jnp.where(eye[None, :, None, :, None], t[:, :, :, None, :], 0.0).reshape(nb, bg * n, bg * p)

    w_in = jnp.concatenate([in_blockdiag(bbre), in_blockdiag(bbim)], axis=-1).astype(BF16)
    w_out = jnp.concatenate([out_blockdiag(c_re.astype(F32).reshape(g, p * n)), out_blockdiag(ncim)],
                            axis=1).astype(BF16)
    return w_in, w_out, tabs


S5_SUB = SUBLANES * SUBLANES


def _segment_scan(er, ei, tabs_ref, cr, ci):
    for l, lvl in enumerate(SCAN_LEVELS):
        pr = tabs_ref[T_LVL + 2 * l]
        pi = tabs_ref[T_LVL + 2 * l + 1]
        sr = pltpu.roll(er, lvl, 0)
        si = pltpu.roll(ei, lvl, 0)
        er, ei = er + pr * sr - pi * si, ei + pr * si + pi * sr
    rr = tabs_ref[T_CARRY]
    ri = tabs_ref[T_CARRY + 1]
    crb = jnp.broadcast_to(cr, er.shape)
    cib = jnp.broadcast_to(ci, ei.shape)
    return er + rr * crb - ri * cib, ei + rr * cib + ri * crb


def _transpose_tiles(ref, nsub):
    return jnp.concatenate([ref[pl.ds(S5_SUB * m + i, SUBLANES, stride=SUBLANES), :]
                            for m in range(nsub) for i in range(SUBLANES)], axis=0)


def _s5_stream(u_ref, win_ref, wout_ref, d_ref, tabs_ref, uf_scr, bu_scr, h_scr, y_scr, init, long_seq):
    tc = u_ref.shape[0]
    nsub = tc // S5_SUB
    ns = S5_BLK_STATE
    uf = u_ref[...].astype(F32)
    uf_scr[...] = uf
    up = _transpose_tiles(uf_scr, nsub).astype(BF16)
    bu_scr[...] = jnp.dot(up, win_ref[...], preferred_element_type=F32)
    ar = tabs_ref[T_A]
    ai = tabs_ref[T_A + 1]
    finals = []
    for m in range(nsub):
        tile = lambda i: slice(S5_SUB * m + SUBLANES * i, S5_SUB * m + SUBLANES * (i + 1))
        if long_seq:
            hr = hi = None
        else:
            hr = init[0][SUBLANES * m:SUBLANES * (m + 1), :]
            hi = init[1][SUBLANES * m:SUBLANES * (m + 1), :]
        for i in range(SUBLANES):
            xr = bu_scr[tile(i), :ns]
            xi = bu_scr[tile(i), ns:]
            if hr is None:
                hr, hi = xr, xi
            else:
                hr, hi = ar * hr - ai * hi + xr, ar * hi + ai * hr + xi
            h_scr[tile(i), :ns] = hr
            h_scr[tile(i), ns:] = hi
        if long_seq:
            cr, ci = init
            fr, fi = _segment_scan(hr, hi, tabs_ref, cr, ci)
            first = lax.broadcasted_iota(jnp.int32, fr.shape, 0) == 0
            sr = jnp.where(first, jnp.broadcast_to(cr, fr.shape), pltpu.roll(fr, 1, 0))
            si = jnp.where(first, jnp.broadcast_to(ci, fi.shape), pltpu.roll(fi, 1, 0))
            init = (fr[SUBLANES - 1:, :], fi[SUBLANES - 1:, :])
            for i in range(SUBLANES):
                pr = tabs_ref[T_POW + 2 * i]
                pi = tabs_ref[T_POW + 2 * i + 1]
                h_scr[tile(i), :ns] = h_scr[tile(i), :ns] + pr * sr - pi * si
                h_scr[tile(i), ns:] = h_scr[tile(i), ns:] + pr * si + pi * sr
        else:
            finals.append((hr, hi))
    y_scr[...] = jnp.dot(h_scr[...].astype(BF16), wout_ref[...], preferred_element_type=F32)
    y = _transpose_tiles(y_scr, nsub) + d_ref[...] * uf
    return jax.nn.gelu(y).astype(BF16), (init if long_seq else finals)


def _s5_prompt_kernel(*refs, nb):
    u_refs = refs[:nb]
    win_ref, wout_ref, d_ref, tabs_ref, gl_ref, hre_ref, him_ref, uf_scr, bu_scr, h_scr, y_scr, carry_scr = refs[nb:]
    ns = S5_BLK_STATE

    @pl.when(pl.program_id(2) == 0)
    def _():
        carry_scr[...] = jnp.zeros_like(carry_scr)

    for s in range(nb):
        init = (carry_scr[s, 0:1, :ns], carry_scr[s, 0:1, ns:])
        out, (cr, ci) = _s5_stream(u_refs[s], win_ref, wout_ref, d_ref, tabs_ref, uf_scr.at[s], bu_scr.at[s],
                                   h_scr.at[s], y_scr.at[s], init, True)
        gl_ref[s] = out
        carry_scr[s, 0:1, :ns] = cr
        carry_scr[s, 0:1, ns:] = ci
        hre_ref[s] = cr
        him_ref[s] = ci


def _s5_sample_kernel(u_ref, win_ref, wout_ref, d_ref, tabs_ref, h0re_ref, h0im_ref, gl_ref, hre_ref, him_ref,
                      uf_scr, bu_scr, h_scr, y_scr):
    out, finals = _s5_stream(u_ref, win_ref, wout_ref, d_ref, tabs_ref, uf_scr, bu_scr, h_scr, y_scr,
                             (h0re_ref, h0im_ref), False)
    gl_ref[...] = out
    for m, (hr, hi) in enumerate(finals):
        hre_ref[SUBLANES * m:SUBLANES * (m + 1), :] = hr
        him_ref[SUBLANES * m:SUBLANES * (m + 1), :] = hi


def _s5_weight_specs():
    nst = S5_BLK_STATE
    return [
        pl.BlockSpec((None, LANES, 2 * nst), lambda j, *_: (j, 0, 0)),
        pl.BlockSpec((None, 2 * nst, LANES), lambda j, *_: (j, 0, 0)),
        pl.BlockSpec((1, LANES), lambda j, *_: (0, j)),
        pl.BlockSpec((None, N_TABS, SUBLANES, nst), lambda j, *_: (j, 0, 0, 0)),
    ]


def _s5_scratch(lead, tc):
    nst = S5_BLK_STATE
    return [pltpu.VMEM(lead + (tc, LANES), F32), pltpu.VMEM(lead + (tc, 2 * nst), F32),
            pltpu.VMEM(lead + (tc, 2 * nst), F32), pltpu.VMEM(lead + (tc, LANES), F32)]


def _s5_prompt(z, w_in, w_out, d_skip, tabs, nbatch, seqlen):
    tc = 256
    nb = 4
    nt = seqlen // tc
    nst = S5_BLK_STATE
    kern = functools.partial(_s5_prompt_kernel, nb=nb)
    u_spec = lambda s: pl.BlockSpec((tc, LANES), lambda j, bb, t: ((bb * nb + s) * nt + t, Z_U // LANES + j))
    state_spec = lambda: pl.BlockSpec((nb, 1, nst), lambda j, bb, t: (bb, 0, j))
    gl, hre, him = pl.pallas_call(
        kern,
        grid=(S5_NBLK, nbatch // nb, nt),
        in_specs=[u_spec(s) for s in range(nb)] + _s5_weight_specs(),
        out_specs=[pl.BlockSpec((nb, tc, LANES), lambda j, bb, t: (bb, t, j)), state_spec(), state_spec()],
        out_shape=[
            jax.ShapeDtypeStruct((nbatch, seqlen, S5_WIDTH), BF16),
            jax.ShapeDtypeStruct((nbatch, 1, S5_GROUPS * S5_STATE), F32),
            jax.ShapeDtypeStruct((nbatch, 1, S5_GROUPS * S5_STATE), F32),
        ],
        scratch_shapes=_s5_scratch((nb,), tc) + [pltpu.VMEM((nb, SUBLANES, 2 * nst), F32)],
        compiler_params=_cparams("parallel", "parallel", "arbitrary"),
        name="s5_prompt",
    )(*([z] * nb), w_in, w_out, d_skip, tabs)
    return (gl.reshape(nbatch * seqlen, S5_WIDTH), hre.reshape(nbatch, S5_GROUPS, S5_STATE),
            him.reshape(nbatch, S5_GROUPS, S5_STATE))


def _s5_sample(z, w_in, w_out, d_skip, tabs, h0_re, h0_im, row_off, nbatch, slen):
    assert slen == SUBLANES
    tc = 256
    nseq = tc // slen
    rb = row_off // tc
    nst = S5_BLK_STATE
    state_spec = lambda: pl.BlockSpec((nseq, nst), lambda j, i: (i, j))
    gl, hre, him = pl.pallas_call(
        _s5_sample_kernel,
        grid=(S5_NBLK, nbatch // nseq),
        in_specs=([pl.BlockSpec((tc, LANES), lambda j, i: (rb + i, Z_U // LANES + j))] + _s5_weight_specs()
                  + [state_spec(), state_spec()]),
        out_specs=[pl.BlockSpec((tc, LANES), lambda j, i: (i, j)), state_spec(), state_spec()],
        out_shape=[
            jax.ShapeDtypeStruct((nbatch * slen, S5_WIDTH), BF16),
            jax.ShapeDtypeStruct((nbatch, S5_GROUPS * S5_STATE), F32),
            jax.ShapeDtypeStruct((nbatch, S5_GROUPS * S5_STATE), F32),
        ],
        scratch_shapes=_s5_scratch((), tc),
        compiler_params=_cparams("parallel", "parallel"),
        name="s5_sample",
    )(z, w_in, w_out, d_skip, tabs, h0_re.reshape(nbatch, -1).astype(F32), h0_im.reshape(nbatch, -1).astype(F32))
    return gl, hre.reshape(nbatch, S5_GROUPS, S5_STATE), him.reshape(nbatch, S5_GROUPS, S5_STATE)


def _xattn_kernel(q_ref, g_ref, k_ref, v_ref, o_ref, *, q_per_seq, k_per_seq):
    tq = q_ref.shape[0]
    nk = k_ref.shape[0]
    kb = k_ref[...].astype(BF16)
    vb = v_ref[...].astype(BF16)
    if tq // q_per_seq > 1:
        same = (lax.broadcasted_iota(jnp.int32, (tq, nk), 0) // q_per_seq
                == lax.broadcasted_iota(jnp.int32, (tq, nk), 1) // k_per_seq)
    else:
        same = None
    for h in range(X_HEADS):
        cols = slice(h * X_HD, (h + 1) * X_HD)
        s = lax.dot_general(q_ref[:, cols], kb[:, cols], (((1,), (1,)), ((), ())),
                            preferred_element_type=F32) * (X_HD ** -0.5)
        if same is not None:
            s = jnp.where(same, s, -jnp.inf)
        e = jnp.exp(s - jnp.max(s, axis=-1, keepdims=True))
        oh = jnp.dot(e.astype(BF16), vb[:, cols], preferred_element_type=F32) / jnp.sum(e, axis=-1, keepdims=True)
        g = g_ref[:, cols].astype(F32)
        o_ref[:, cols] = (oh * (g * jax.nn.sigmoid(g))).astype(BF16)


def _xattn(z, mk, mv, *, row_off, nrows, tq, q_per_seq, seqs_per_step, kv_col_blk, name):
    nk = seqs_per_step * MEM_LEN
    steps_per_kv = (seqs_per_step * q_per_seq) // tq if tq < seqs_per_step * q_per_seq else 1
    rb = row_off // tq
    kern = functools.partial(_xattn_kernel, q_per_seq=q_per_seq, k_per_seq=MEM_LEN)
    kmap = lambda cb: (lambda i: (i // steps_per_kv, cb))
    return pl.pallas_call(
        kern,
        grid=(nrows // tq,),
        in_specs=[
            pl.BlockSpec((tq, X_WIDTH), lambda i: (rb + i, Z_QX // X_WIDTH)),
            pl.BlockSpec((tq, X_WIDTH), lambda i: (rb + i, Z_GX // X_WIDTH)),
            pl.BlockSpec((nk, X_WIDTH), kmap(kv_col_blk[0])),
            pl.BlockSpec((nk, X_WIDTH), kmap(kv_col_blk[1])),
        ],
        out_specs=pl.BlockSpec((tq, X_WIDTH), lambda i: (i, 0)),
        out_shape=jax.ShapeDtypeStruct((nrows, X_WIDTH), BF16),
        compiler_params=_cparams("parallel"),
        name=name,
    )(z, z, mk, mv)


def _glu_kernel(x_ref, wa_ref, wb_ref, g_ref, o_ref):
    x = x_ref[...]
    a = jnp.dot(x, wa_ref[...], preferred_element_type=F32)
    b = jnp.dot(x, wb_ref[...], preferred_element_type=F32)
    g = g_ref[...].astype(F32)
    o_ref[...] = (a * jax.nn.sigmoid(b) * (g * jax.nn.sigmoid(g))).astype(o_ref.dtype)


def _glu(gl, w_glu, z, row_off, *, bm, bn):
    m, k = gl.shape
    rb = row_off // bm
    return pl.pallas_call(
        _glu_kernel,
        grid=(S5_WIDTH // bn, m // bm),
        in_specs=[
            pl.BlockSpec((bm, k), lambda j, i: (i, 0)),
            pl.BlockSpec((k, bn), lambda j, i: (0, j)),
            pl.BlockSpec((k, bn), lambda j, i: (0, j + S5_WIDTH // bn)),
            pl.BlockSpec((bm, bn), lambda j, i: (rb + i, Z_GS5 // bn + j)),
        ],
        out_specs=pl.BlockSpec((bm, bn), lambda j, i: (i, j)),
        out_shape=jax.ShapeDtypeStruct((m, S5_WIDTH), BF16),
        compiler_params=_cparams("parallel", "parallel"),
        name="glu",
    )(gl, w_glu, w_glu, z)


def _merge_kernel(oa_ref, ob_ref, oc_ref, wa_ref, wb_ref, wc_ref, ma_ref, mb_ref, mc_ref, o_ref):
    def branch(o_r, w_r, m_r):
        return jax.nn.sigmoid(m_r[...].astype(F32)) * jnp.dot(o_r[...], w_r[...], preferred_element_type=F32)

    o_ref[...] = (branch(oa_ref, wa_ref, ma_ref) + branch(ob_ref, wb_ref, mb_ref)
                  + branch(oc_ref, wc_ref, mc_ref)).astype(o_ref.dtype)


def _merge(o_a, o_b, o_c, w_a, w_b, w_c, z, row_off, *, bm, bn):
    m = o_a.shape[0]
    rb = row_off // bm
    lhs = lambda t: pl.BlockSpec((bm, t.shape[1]), lambda j, i: (i, 0))
    rhs = lambda t: pl.BlockSpec((t.shape[0], bn), lambda j, i: (0, j))
    gate = lambda off: pl.BlockSpec((bm, bn), lambda j, i: (rb + i, off // bn + j))
    return pl.pallas_call(
        _merge_kernel,
        grid=(D_MODEL // bn, m // bm),
        in_specs=[lhs(o_a), lhs(o_b), lhs(o_c), rhs(w_a), rhs(w_b), rhs(w_c), gate(Z_MA), gate(Z_MB), gate(Z_MC)],
        out_specs=pl.BlockSpec((bm, bn), lambda j, i: (i, j)),
        out_shape=jax.ShapeDtypeStruct((m, D_MODEL), BF16),
        compiler_params=_cparams("parallel", "parallel"),
        name="merge",
    )(o_a, o_b, o_c, w_a, w_b, w_c, z, z, z)


def _out_ln_kernel(m_ref, w_ref, x_ref, g_ref, b_ref, o_ref, pre_scr, *, nj, bn):
    j = pl.program_id(1)
    pre_scr[j] = DN_ALPHA * x_ref[...] + jnp.dot(m_ref[...], w_ref[...], preferred_element_type=F32)

    @pl.when(j == nj - 1)
    def _():
        width = nj * bn
        tot = pre_scr[0].sum(axis=-1, keepdims=True)
        for t in range(1, nj):
            tot = tot + pre_scr[t].sum(axis=-1, keepdims=True)
        mu = tot / width
        sq = None
        for t in range(nj):
            d = pre_scr[t] - mu
            part = (d * d).sum(axis=-1, keepdims=True)
            sq = part if sq is None else sq + part
        rstd = lax.rsqrt(sq / width + LN_EPS)
        for t in range(nj):
            cols = slice(t * bn, (t + 1) * bn)
            o_ref[:, cols] = (pre_scr[t] - mu) * rstd * g_ref[:, cols] + b_ref[:, cols]


def _out_ln(merged, w_out, x, ln_g, ln_b, *, bm, bn):
    m, k = merged.shape
    nj = D_MODEL // bn
    kern = functools.partial(_out_ln_kernel, nj=nj, bn=bn)
    return pl.pallas_call(
        kern,
        grid=(m // bm, nj),
        in_specs=[
            pl.BlockSpec((bm, k), lambda i, j: (i, 0)),
            pl.BlockSpec((k, bn), lambda i, j: (0, j)),
            pl.BlockSpec((bm, bn), lambda i, j: (i, j)),
            pl.BlockSpec((1, D_MODEL), lambda i, j: (0, 0)),
            pl.BlockSpec((1, D_MODEL), lambda i, j: (0, 0)),
        ],
        out_specs=pl.BlockSpec((bm, D_MODEL), lambda i, j: (i, 0)),
        out_shape=jax.ShapeDtypeStruct((m, D_MODEL), F32),
        scratch_shapes=[pltpu.VMEM((nj, bm, bn), F32)],
        compiler_params=_cparams("parallel", "arbitrary"),
        name="out_ln",
    )(merged, w_out, x, ln_g.reshape(1, D_MODEL).astype(F32), ln_b.reshape(1, D_MODEL).astype(F32))


def _group_tail(z, row_off, o_ret, gl, o_x, x2d, w):
    o_s5 = _glu(gl, w["glu"], z, row_off, bm=512, bn=512)
    merged = _merge(o_ret, o_s5, o_x, w["proj_a"], w["proj_b"], w["proj_c"], z, row_off, bm=512, bn=512)
    return _out_ln(merged, w["out"], x2d, w["ln_g"], w["ln_b"], bm=512, bn=512)


def kernel(x_prompt, x_sample, mem_prompt, state_ret, state_s5_re, state_s5_im, cache_mem_k, cache_mem_v, w_in, w_mem_kv, s5_a_re, s5_a_im, s5_log_step, s5_b_re, s5_b_im, s5_c_re, s5_c_im, s5_d, w_glu, w_proj_a, w_proj_b, w_proj_c, w_out, ln_g, ln_b):
    depth = w_in.shape[0]
    assert depth == 1
    l = 0
    n_p = BATCH * SEQ
    n_s = DEC_BATCH * DEC_SEQ
    xp2 = x_prompt.reshape(n_p, D_MODEL)
    xs2 = x_sample.reshape(n_s, D_MODEL)

    xb = _concat_cast(xp2, xs2, bm=256, out_dtype=BF16)
    w = dict(glu=w_glu[l].astype(BF16), proj_a=w_proj_a[l].astype(BF16), proj_b=w_proj_b[l].astype(BF16),
             proj_c=w_proj_c[l].astype(BF16), out=w_out[l].astype(BF16), ln_g=ln_g[l], ln_b=ln_b[l])

    z = _matmul(xb, w_in[l], bm=512, bn=1024, out_dtype=BF16, n_out=IN_WIDTH, name="in_proj")

    memb = mem_prompt.reshape(BATCH * MEM_LEN, D_MODEL).astype(BF16)
    mk = _matmul(memb, w_mem_kv[l], bm=1024, bn=512, out_dtype=F32, n_out=X_WIDTH, col_blk_off=0, name="mem_k")
    mv = _matmul(memb, w_mem_kv[l], bm=1024, bn=512, out_dtype=F32, n_out=X_WIDTH, col_blk_off=X_WIDTH // 512,
                 name="mem_v")

    s5_win, s5_wout, s5_tabs = _s5_prepare(s5_a_re[l], s5_a_im[l], s5_log_step[l], s5_b_re[l], s5_b_im[l],
                                           s5_c_re[l], s5_c_im[l])
    d_skip = s5_d[l].reshape(1, S5_WIDTH).astype(F32)

    o_ret_p, ret_p = _retention_prompt(z, BATCH, SEQ)
    gl_p, hre_p, him_p = _s5_prompt(z, s5_win, s5_wout, d_skip, s5_tabs, BATCH, SEQ)
    o_x_p = _xattn(z, mk, mv, row_off=0, nrows=n_p, tq=512, q_per_seq=SEQ, seqs_per_step=1,
                   kv_col_blk=(0, 0), name="xattn_prompt")
    y_p = _group_tail(z, 0, o_ret_p, gl_p, o_x_p, xp2, w)

    o_ret_s, ret_s = _retention_sample(z, state_ret[l], n_p, DEC_BATCH, DEC_SEQ, PAST_LEN)
    gl_s, hre_s, him_s = _s5_sample(z, s5_win, s5_wout, d_skip, s5_tabs, state_s5_re[l], state_s5_im[l],
                                    n_p, DEC_BATCH, DEC_SEQ)
    ck = cache_mem_k[l].reshape(DEC_BATCH * MEM_LEN, X_WIDTH)
    cv = cache_mem_v[l].reshape(DEC_BATCH * MEM_LEN, X_WIDTH)
    o_x_s = _xattn(z, ck, cv, row_off=n_p, nrows=n_s, tq=2 * DEC_SEQ, q_per_seq=DEC_SEQ, seqs_per_step=2,
                   kv_col_blk=(0, 0), name="xattn_sample")
    y_s = _group_tail(z, n_p, o_ret_s, gl_s, o_x_s, xs2, w)

    return (y_p.reshape(BATCH, SEQ, D_MODEL), y_s.reshape(DEC_BATCH, DEC_SEQ, D_MODEL),
            ret_p[None], hre_p[None], him_p[None],
            mk.reshape(1, BATCH, MEM_LEN, X_HEADS, X_HD), mv.reshape(1, BATCH, MEM_LEN, X_HEADS, X_HD),
            ret_s[None], hre_s[None], him_s[None])
```

```python
import functools
import math

import jax
import jax.numpy as jnp
import numpy as np
from jax import lax
from jax.experimental import pallas as pl
from jax.experimental.pallas import tpu as pltpu

F32 = jnp.float32
BF16 = jnp.bfloat16

D_MODEL = 4096
BATCH = 4
SEQ = 2048
DEC_BATCH = 128
DEC_SEQ = 8
PAST_LEN = 16384

RET_HEADS = 16
RET_DK = 128
RET_DV = 256
RET_QK = RET_HEADS * RET_DK
RET_V = RET_HEADS * RET_DV
RET_CHUNK = 128
ROPE_BASE = 10000.0

S5_WIDTH = D_MODEL // 2
S5_GROUP = 16
S5_GROUPS = S5_WIDTH // S5_GROUP
S5_STATE = 64

X_HEADS = 4
X_WIDTH = D_MODEL // 2
X_HD = X_WIDTH // X_HEADS
MEM_LEN = 256

DN_ALPHA = 2.0 ** 0.25
LN_EPS = 1e-5
GN_EPS = 1e-5

IN_WIDTH = 2 * RET_QK + 2 * RET_V + 2 * S5_WIDTH + 2 * X_WIDTH + 3 * D_MODEL

Z_Q = 0
Z_K = Z_Q + RET_QK
Z_V = Z_K + RET_QK
Z_GRET = Z_V + RET_V
Z_U = Z_GRET + RET_V
Z_GS5 = Z_U + S5_WIDTH
Z_QX = Z_GS5 + S5_WIDTH
Z_GX = Z_QX + X_WIDTH
Z_MA = Z_GX + X_WIDTH
Z_MB = Z_MA + D_MODEL
Z_MC = Z_MB + D_MODEL

SUBLANES = 8
LANES = 128
VMEM_PHYSICAL_BYTES = 64 * 1024 * 1024
VMEM_LIMIT_BYTES = 56 * 1024 * 1024
VMEM_TEMP_BYTES = 12 * 1024 * 1024

S5_BLK_GROUPS = LANES // S5_GROUP
S5_BLK_STATE = S5_BLK_GROUPS * S5_STATE
S5_NBLK = S5_GROUPS // S5_BLK_GROUPS
SCAN_LEVELS = (1, 2, 4)
T_A = 0
T_POW = 2
T_LVL = T_POW + 2 * SUBLANES
T_CARRY = T_LVL + 2 * len(SCAN_LEVELS)
N_TABS = T_CARRY + 2


def _cparams(*sem, vmem_limit_bytes=VMEM_LIMIT_BYTES):
    return pltpu.CompilerParams(dimension_semantics=sem, vmem_limit_bytes=vmem_limit_bytes)


def _front_kernel(xp_ref, xs_ref, mem_ref, w_ref, xb_ref, mk_ref, mv_ref, *, nxp, nkv):
    i = pl.program_id(0)

    @pl.when(i < nxp)
    def _():
        xb_ref[...] = xp_ref[...].astype(BF16)

    @pl.when(i >= nxp)
    def _():
        xb_ref[...] = xs_ref[...].astype(BF16)

    @pl.when(i < 2 * nkv)
    def _():
        res = jnp.dot(mem_ref[...], w_ref[...].astype(BF16), preferred_element_type=F32)

        @pl.when(i < nkv)
        def _():
            mk_ref[...] = res

        @pl.when(i >= nkv)
        def _():
            mv_ref[...] = res


def _front(xp, xs, memb, w_kv, *, bm, bn):
    k = xp.shape[1]
    nxp, nxs = xp.shape[0] // bm, xs.shape[0] // bm
    mrows = memb.shape[0]
    nkv = X_WIDTH // bn
    assert 2 * nkv <= nxp + nxs
    kern = functools.partial(_front_kernel, nxp=nxp, nkv=nkv)
    return pl.pallas_call(
        kern,
        grid=(nxp + nxs,),
        in_specs=[
            pl.BlockSpec((bm, k), lambda i: (jnp.minimum(i, nxp - 1), 0)),
            pl.BlockSpec((bm, k), lambda i: (jnp.maximum(i - nxp, 0), 0), pipeline_mode=pl.Buffered(1)),
            pl.BlockSpec((mrows, k), lambda i: (0, 0), pipeline_mode=pl.Buffered(1)),
            pl.BlockSpec((k, bn), lambda i: (0, jnp.minimum(i, 2 * nkv - 1))),
        ],
        out_specs=[
            pl.BlockSpec((bm, k), lambda i: (i, 0)),
            pl.BlockSpec((mrows, bn), lambda i: (0, jnp.minimum(i, nkv - 1))),
            pl.BlockSpec((mrows, bn), lambda i: (0, jnp.clip(i - nkv, 0, nkv - 1))),
        ],
        out_shape=[
            jax.ShapeDtypeStruct((xp.shape[0] + xs.shape[0], k), BF16),
            jax.ShapeDtypeStruct((mrows, X_WIDTH), F32),
            jax.ShapeDtypeStruct((mrows, X_WIDTH), F32),
        ],
        compiler_params=_cparams("arbitrary"),
        name="front",
    )(xp, xs, memb, w_kv)


def _colmm_kernel(*refs, n_lhs, terms, n_extra, epilogue, ncols, bn):
    lhs = refs[:n_lhs]
    w_hbm = refs[n_lhs:n_lhs + len(terms)]
    extras = refs[n_lhs + len(terms):n_lhs + len(terms) + n_extra]
    o_ref, stage, wb_scr, sem = refs[n_lhs + len(terms) + n_extra:]
    j = pl.program_id(0)
    i = pl.program_id(1)

    def tile_copies(col):
        return [pltpu.make_async_copy(
            w_hbm[t].at[:, pl.ds(pl.multiple_of(off + col * bn, LANES), bn)],
            stage.at[pl.ds(row0, kt), :], sem.at[t]) for t, (_, row0, kt, off) in enumerate(terms)]

    @pl.when(i == 0)
    def _():
        @pl.when(j == 0)
        def _():
            for c in tile_copies(0):
                c.start()

        for c in tile_copies(j):
            c.wait()
        wb_scr[...] = stage[...].astype(BF16)

        @pl.when(j + 1 < ncols)
        def _():
            for c in tile_copies(j + 1):
                c.start()

    prods = [jnp.dot(lhs[li][...], wb_scr[row0:row0 + kt, :], preferred_element_type=F32)
             for li, row0, kt, _ in terms]
    o_ref[...] = epilogue(prods, [e[...] for e in extras]).astype(o_ref.dtype)


def _colmm(lhs, weights, extras, epilogue, *, n_out, bm, bn, out_dtype, name):
    m = lhs[0].shape[0]
    terms, row0 = [], 0
    for li, w, off in weights:
        kt = w.shape[0]
        assert lhs[li].shape == (m, kt) and off % LANES == 0
        terms.append((li, row0, kt, off))
        row0 += kt
    ktot = row0
    lhs_bytes = sum(2 * bm * a.shape[1] * a.dtype.itemsize for a in lhs)
    extra_bytes = sum(2 * bm * bn * a.dtype.itemsize for a, _, _ in extras)
    need = (ktot * bn * 6 + lhs_bytes + extra_bytes + 2 * bm * bn * jnp.dtype(out_dtype).itemsize
            + len(terms) * bm * bn * 4)
    limit = min(max(VMEM_LIMIT_BYTES, need + VMEM_TEMP_BYTES), VMEM_PHYSICAL_BYTES - (2 << 20))
    assert need + (4 << 20) <= limit, (name, need, limit)
    kern = functools.partial(_colmm_kernel, n_lhs=len(lhs), terms=tuple(terms), n_extra=len(extras),
                             epilogue=epilogue, ncols=n_out // bn, bn=bn)
    extra_spec = lambda rb, cb: pl.BlockSpec((bm, bn), lambda j, i: (rb + i, cb + j))
    return pl.pallas_call(
        kern,
        grid=(n_out // bn, m // bm),
        in_specs=([pl.BlockSpec((bm, a.shape[1]), lambda j, i: (i, 0)) for a in lhs]
                  + [pl.BlockSpec(memory_space=pl.ANY) for _ in terms]
                  + [extra_spec(rb, cb) for _, rb, cb in extras]),
        out_specs=pl.BlockSpec((bm, bn), lambda j, i: (i, j)),
        out_shape=jax.ShapeDtypeStruct((m, n_out), out_dtype),
        scratch_shapes=[pltpu.VMEM((ktot, bn), F32), pltpu.VMEM((ktot, bn), BF16),
                        pltpu.SemaphoreType.DMA((len(terms),))],
        compiler_params=_cparams("arbitrary", "arbitrary", vmem_limit_bytes=limit),
        name=name,
    )(*lhs, *[w for _, w, _ in weights], *[a for a, _, _ in extras])


def _rotate(x, cos, sin_next, sin_prev):
    return x * cos + pltpu.roll(x, LANES - 1, 1) * sin_next + pltpu.roll(x, 1, 1) * sin_prev


def _ret_block(q, k, v, g, cos, sin_next, sin_prev, mask, xi, zeta, gc, states, slen):
    nseq = len(states)
    rows = q.shape[0]
    qr = _rotate(q, cos, sin_next, sin_prev)
    kr = _rotate(k, cos, sin_next, sin_prev) * (RET_DK ** -0.5)
    qb = qr.astype(BF16)
    kb = kr.astype(BF16)
    sc = lax.dot_general(qb, kb, (((1,), (1,)), ((), ())), preferred_element_type=F32) * mask
    o = jnp.dot(sc.astype(BF16), v, preferred_element_type=F32)
    qx = qr * xi
    kzt = (kr * zeta).T.astype(BF16)
    new_states = []
    if nseq == 1:
        s = states[0]
        o = o + jnp.dot(qx.astype(BF16), s.astype(BF16), preferred_element_type=F32)
        new_states.append(gc * s + jnp.dot(kzt, v, preferred_element_type=F32))
    else:
        pair = 2 * slen
        assert pair == 2 * SUBLANES and nseq % 2 == 0
        row_in_pair = lax.broadcasted_iota(jnp.int32, (pair, RET_DV), 0)
        row_seq = lax.broadcasted_iota(jnp.int32, (rows, RET_DV), 0) // slen
        parts = []
        for m in range(nseq // 2):
            qpair = qx[m * pair:(m + 1) * pair].astype(BF16)
            o0 = jnp.dot(qpair, states[2 * m].astype(BF16), preferred_element_type=F32)
            o1 = jnp.dot(qpair, states[2 * m + 1].astype(BF16), preferred_element_type=F32)
            parts.append(jnp.where(row_in_pair < slen, o0, o1))
        o = o + jnp.concatenate(parts, axis=0)
        vf = v.astype(F32)
        for n in range(nseq):
            vn = jnp.where(row_seq == n, vf, 0.0).astype(BF16)
            new_states.append(gc * states[n] + jnp.dot(kzt, vn, preferred_element_type=F32))
    mu = jnp.mean(o, axis=-1, keepdims=True)
    d = o - mu
    var = jnp.mean(d * d, axis=-1, keepdims=True)
    on = d * lax.rsqrt(var + GN_EPS)
    out = (on * (g * jax.nn.sigmoid(g))).astype(BF16)
    return out, new_states


def _ret_prompt_kernel(gc_ref, q_ref, k_ref, v_ref, g_ref, cos_ref, sn_ref, sp_ref, mask_ref, xi_ref, zeta_ref,
                       o_ref, sfin_ref, s_scr, *, chunk, nchunks, hb):
    head0 = pl.program_id(1) * hb
    s_scr[...] = jnp.zeros_like(s_scr)

    def body(c, carry):
        rows = pl.ds(pl.multiple_of(c * chunk, chunk), chunk)
        cos, sn, sp = cos_ref[rows, :], sn_ref[rows, :], sp_ref[rows, :]
        for hh in range(hb):
            qc = slice(hh * RET_DK, (hh + 1) * RET_DK)
            vc = slice(hh * RET_DV, (hh + 1) * RET_DV)
            out, (s_new,) = _ret_block(
                q_ref[rows, qc].astype(F32), k_ref[rows, qc].astype(F32), v_ref[rows, vc],
                g_ref[rows, vc].astype(F32), cos, sn, sp, mask_ref[hh], xi_ref[hh], zeta_ref[hh],
                gc_ref[head0 + hh], [s_scr[hh]], chunk)
            o_ref[rows, vc] = out
            s_scr[hh] = s_new
        return carry

    lax.fori_loop(0, nchunks, body, 0)
    sfin_ref[...] = s_scr[...]


def _ret_sample_kernel(gc_ref, q_ref, k_ref, v_ref, g_ref, cos_ref, sn_ref, sp_ref, mask_ref, xi_ref, zeta_ref,
                       s0_ref, o_ref, sfin_ref, *, slen, nseq):
    gc = gc_ref[pl.program_id(1)]
    states = [s0_ref[n] for n in range(nseq)]
    out, new_states = _ret_block(
        q_ref[...].astype(F32), k_ref[...].astype(F32), v_ref[...], g_ref[...].astype(F32),
        cos_ref[...], sn_ref[...], sp_ref[...], mask_ref[...], xi_ref[...], zeta_ref[...], gc, states, slen)
    o_ref[...] = out
    for n in range(nseq):
        sfin_ref[n] = new_states[n]


def _rope_tables(pos):
    half = RET_DK // 2
    inv = 1.0 / (ROPE_BASE ** (np.arange(half, dtype=np.float64) / half))
    ang = np.asarray(pos, np.float64)[:, None] * inv[None, :]
    cos = np.repeat(np.cos(ang), 2, axis=1)
    sin = np.repeat(np.sin(ang), 2, axis=1)
    even = (np.arange(RET_DK) % 2) == 0
    return (np.asarray(cos, np.float32), np.asarray(np.where(even, -sin, 0.0), np.float32),
            np.asarray(np.where(even, 0.0, sin), np.float32))


def _decay_tables(slen, nseq):
    lg = np.log1p(-np.exp2(-5.0 - np.arange(RET_HEADS, dtype=np.float64)))
    idx = np.arange(slen, dtype=np.float64)
    rel = idx[:, None] - idx[None, :]
    inner = np.where(rel[None] >= 0, np.exp(lg[:, None, None] * np.maximum(rel, 0.0)[None]), 0.0)
    xi = np.exp(lg[:, None] * (idx + 1.0))
    zeta = np.exp(lg[:, None] * (slen - 1.0 - idx))
    gc = np.exp(lg * slen)
    mask = np.einsum("nm,hij->hnimj", np.eye(nseq), inner).reshape(RET_HEADS, nseq * slen, nseq * slen)
    rows = nseq * slen
    xi_t = np.broadcast_to(np.tile(xi, (1, nseq))[:, :, None], (RET_HEADS, rows, RET_DK))
    zeta_t = np.broadcast_to(np.tile(zeta, (1, nseq))[:, :, None], (RET_HEADS, rows, RET_DK))
    f32 = lambda t: np.ascontiguousarray(t, dtype=np.float32)
    return f32(mask), f32(xi_t), f32(zeta_t), f32(gc)


def _retention_prompt(z, nbatch, seqlen):
    chunk = RET_CHUNK
    hb = 4
    cos, sn, sp = _rope_tables(np.arange(seqlen))
    mask, xi, zeta, gc = _decay_tables(chunk, 1)
    tab = lambda: pl.BlockSpec((seqlen, RET_DK), lambda b, h: (0, 0))
    head_tab = lambda w: pl.BlockSpec((hb, chunk, w), lambda b, h: (h, 0, 0))
    kern = functools.partial(_ret_prompt_kernel, chunk=chunk, nchunks=seqlen // chunk, hb=hb)
    qk_w, v_w = hb * RET_DK, hb * RET_DV
    return pl.pallas_call(
        kern,
        grid=(nbatch, RET_HEADS // hb),
        in_specs=[
            pl.BlockSpec(memory_space=pltpu.SMEM),
            pl.BlockSpec((seqlen, qk_w), lambda b, h: (b, Z_Q // qk_w + h)),
            pl.BlockSpec((seqlen, qk_w), lambda b, h: (b, Z_K // qk_w + h)),
            pl.BlockSpec((seqlen, v_w), lambda b, h: (b, Z_V // v_w + h)),
            pl.BlockSpec((seqlen, v_w), lambda b, h: (b, Z_GRET // v_w + h)),
            tab(), tab(), tab(),
            head_tab(chunk), head_tab(RET_DK), head_tab(RET_DK),
        ],
        out_specs=[
            pl.BlockSpec((seqlen, v_w), lambda b, h: (b, h)),
            pl.BlockSpec((None, hb, RET_DK, RET_DV), lambda b, h: (b, h, 0, 0)),
        ],
        out_shape=[
            jax.ShapeDtypeStruct((nbatch * seqlen, RET_V), BF16),
            jax.ShapeDtypeStruct((nbatch, RET_HEADS, RET_DK, RET_DV), F32),
        ],
        scratch_shapes=[pltpu.VMEM((hb, RET_DK, RET_DV), F32)],
        compiler_params=_cparams("parallel", "parallel"),
        name="retention_prompt",
    )(gc, z, z, z, z, cos, sn, sp, mask, xi, zeta)


def _retention_sample(z, s0, row_off, nbatch, slen, pos0):
    nseq = 32
    rows = nseq * slen
    cos, sn, sp = (np.tile(t, (nseq, 1)) for t in _rope_tables(pos0 + np.arange(slen)))
    mask, xi, zeta, gc = _decay_tables(slen, nseq)
    rb = row_off // rows
    tab = lambda: pl.BlockSpec((rows, RET_DK), lambda i, h: (0, 0))
    head_tab = lambda w: pl.BlockSpec((None, rows, w), lambda i, h: (h, 0, 0))
    kern = functools.partial(_ret_sample_kernel, slen=slen, nseq=nseq)
    return pl.pallas_call(
        kern,
        grid=(nbatch // nseq, RET_HEADS),
        in_specs=[
            pl.BlockSpec(memory_space=pltpu.SMEM),
            pl.BlockSpec((rows, RET_DK), lambda i, h: (rb + i, Z_Q // RET_DK + h)),
            pl.BlockSpec((rows, RET_DK), lambda i, h: (rb + i, Z_K // RET_DK + h)),
            pl.BlockSpec((rows, RET_DV), lambda i, h: (rb + i, Z_V // RET_DV + h)),
            pl.BlockSpec((rows, RET_DV), lambda i, h: (rb + i, Z_GRET // RET_DV + h)),
            tab(), tab(), tab(),
            head_tab(rows), head_tab(RET_DK), head_tab(RET_DK),
            pl.BlockSpec((nseq, None, RET_DK, RET_DV), lambda i, h: (i, h, 0, 0)),
        ],
        out_specs=[
            pl.BlockSpec((rows, RET_DV), lambda i, h: (i, h)),
            pl.BlockSpec((nseq, None, RET_DK, RET_DV), lambda i, h: (i, h, 0, 0)),
        ],
        out_shape=[
            jax.ShapeDtypeStruct((nbatch * slen, RET_V), BF16),
            jax.ShapeDtypeStruct((nbatch, RET_HEADS, RET_DK, RET_DV), F32),
        ],
        compiler_params=_cparams("parallel", "parallel"),
        name="retention_sample",
    )(gc, z, z, z, z, cos, sn, sp, mask, xi, zeta, s0)


def _s5_prep_kernel(are_ref, aim_ref, dt_ref, are_w_ref, aim_w_ref, dt_w_ref, bre_ref, bim_ref, cim_ref,
                    tabs_ref, bbre_ref, bbim_ref, ncim_ref):
    def abar(ar, ai, dt):
        mag = jnp.exp(dt * ar)
        return mag * jnp.cos(dt * ai), mag * jnp.sin(dt * ai)

    def powers(ar, ai):
        out = [(ar, ai)]
        for _ in range(1, SUBLANES):
            pr, pi = out[-1]
            out.append((pr * ar - pi * ai, pr * ai + pi * ar))
        return out

    pw = powers(*abar(are_ref[...], aim_ref[...], dt_ref[...]))
    qw = powers(*pw[-1])
    k_idx = lax.broadcasted_iota(jnp.int32, (SUBLANES, S5_BLK_STATE), 0)
    for j in range(S5_NBLK):
        every_row = lambda v: jnp.broadcast_to(v[j:j + 1, :], (SUBLANES, S5_BLK_STATE))
        for c in range(2):
            tabs_ref[j, T_A + c] = every_row(pw[0][c])
            for i in range(SUBLANES):
                tabs_ref[j, T_POW + 2 * i + c] = every_row(pw[i][c])
            for l, lvl in enumerate(SCAN_LEVELS):
                tabs_ref[j, T_LVL + 2 * l + c] = jnp.where(k_idx >= lvl, every_row(qw[lvl - 1][c]), 0.0)
            carry = every_row(qw[0][c])
            for k in range(1, SUBLANES):
                carry = jnp.where(k_idx == k, every_row(qw[k][c]), carry)
            tabs_ref[j, T_CARRY + c] = carry

    a_r, a_i = are_w_ref[...], aim_w_ref[...]
    w_r, w_i = abar(a_r, a_i, dt_w_ref[...])
    den = a_r * a_r + a_i * a_i
    x_re = w_r - 1.0
    f_re = (x_re * a_r + w_i * a_i) / den
    f_im = (w_i * a_r - x_re * a_i) / den
    br, bi = bre_ref[...], bim_ref[...]
    bbre_ref[...] = f_re * br - f_im * bi
    bbim_ref[...] = f_re * bi + f_im * br
    ncim_ref[...] = -cim_ref[...]


def _s5_prepare(a_re, a_im, log_step, b_re, b_im, c_re, c_im):
    g, n, p = S5_GROUPS, S5_STATE, S5_GROUP
    dt = jnp.broadcast_to(jnp.exp(log_step.astype(F32))[:, None], (g, n))
    wide = lambda t: jnp.repeat(t, p, axis=1)
    vm = lambda: pl.BlockSpec(memory_space=pltpu.VMEM)
    nb, bg = S5_NBLK, S5_BLK_GROUPS
    per_blk = lambda t: t.reshape(nb, S5_BLK_STATE)
    tabs, bbre, bbim, ncim = pl.pallas_call(
        _s5_prep_kernel,
        in_specs=[vm() for _ in range(9)],
        out_specs=[vm() for _ in range(4)],
        out_shape=[
            jax.ShapeDtypeStruct((nb, N_TABS, SUBLANES, S5_BLK_STATE), F32),
            jax.ShapeDtypeStruct((g, n * p), F32),
            jax.ShapeDtypeStruct((g, n * p), F32),
            jax.ShapeDtypeStruct((g, p * n), F32),
        ],
        name="s5_discretize",
    )(per_blk(a_re.astype(F32)), per_blk(a_im.astype(F32)), per_blk(dt),
      wide(a_re.astype(F32)), wide(a_im.astype(F32)), wide(dt),
      b_re.astype(F32).reshape(g, n * p), b_im.astype(F32).reshape(g, n * p), c_im.astype(F32).reshape(g, p * n))

    eye = jnp.eye(bg, dtype=bool)

    def in_blockdiag(t):
        t = t.reshape(nb, bg, n, p).transpose(0, 1, 3, 2)
        return jnp.where(eye[None, :, None, :, None], t[:, :, :, None, :], 0.0).reshape(nb, bg * p, bg * n)

    def out_blockdiag(t):
        t = t.reshape(nb, bg, p, n).transpose(0, 1, 3, 2)
        return jnp.where(eye[None, :, None, :, None], t[:, :, :, None, :], 0.0).reshape(nb, bg * n, bg * p)

    w_in = jnp.concatenate([in_blockdiag(bbre), in_blockdiag(bbim)], axis=-1).astype(BF16)
    w_out = jnp.concatenate([out_blockdiag(c_re.astype(F32).reshape(g, p * n)), out_blockdiag(ncim)],
                            axis=1).astype(BF16)
    return w_in, w_out, tabs


S5_SUB = SUBLANES * SUBLANES


def _segment_scan(er, ei, tabs_ref, cr, ci):
    for l, lvl in enumerate(SCAN_LEVELS):
        pr = tabs_ref[T_LVL + 2 * l]
        pi = tabs_ref[T_LVL + 2 * l + 1]
        sr = pltpu.roll(er, lvl, 0)
        si = pltpu.roll(ei, lvl, 0)
        er, ei = er + pr * sr - pi * si, ei + pr * si + pi * sr
    rr = tabs_ref[T_CARRY]
    ri = tabs_ref[T_CARRY + 1]
    crb = jnp.broadcast_to(cr, er.shape)
    cib = jnp.broadcast_to(ci, ei.shape)
    return er + rr * crb - ri * cib, ei + rr * cib + ri * crb


def _transpose_tiles(ref, nsub):
    return jnp.concatenate([ref[pl.ds(S5_SUB * m + i, SUBLANES, stride=SUBLANES), :]
                            for m in range(nsub) for i in range(SUBLANES)], axis=0)


def _s5_stream(u_ref, win_ref, wout_ref, d_ref, tabs_ref, uf_scr, bu_scr, h_scr, y_scr, init, long_seq):
    tc = u_ref.shape[0]
    nsub = tc // S5_SUB
    ns = S5_BLK_STATE
    uf = u_ref[...].astype(F32)
    uf_scr[...] = uf
    up = _transpose_tiles(uf_scr, nsub).astype(BF16)
    bu_scr[...] = jnp.dot(up, win_ref[...], preferred_element_type=F32)
    ar = tabs_ref[T_A]
    ai = tabs_ref[T_A + 1]
    finals = []
    for m in range(nsub):
        tile = lambda i: slice(S5_SUB * m + SUBLANES * i, S5_SUB * m + SUBLANES * (i + 1))
        if long_seq:
            hr = hi = None
        else:
            hr = init[0][SUBLANES * m:SUBLANES * (m + 1), :]
            hi = init[1][SUBLANES * m:SUBLANES * (m + 1), :]
        for i in range(SUBLANES):
            xr = bu_scr[tile(i), :ns]
            xi = bu_scr[tile(i), ns:]
            if hr is None:
                hr, hi = xr, xi
            else:
                hr, hi = ar * hr - ai * hi + xr, ar * hi + ai * hr + xi
            h_scr[tile(i), :ns] = hr
            h_scr[tile(i), ns:] = hi
        if long_seq:
            cr, ci = init
            fr, fi = _segment_scan(hr, hi, tabs_ref, cr, ci)
            first = lax.broadcasted_iota(jnp.int32, fr.shape, 0) == 0
            sr = jnp.where(first, jnp.broadcast_to(cr, fr.shape), pltpu.roll(fr, 1, 0))
            si = jnp.where(first, jnp.broadcast_to(ci, fi.shape), pltpu.roll(fi, 1, 0))
            init = (fr[SUBLANES - 1:, :], fi[SUBLANES - 1:, :])
            for i in range(SUBLANES):
                pr = tabs_ref[T_POW + 2 * i]
                pi = tabs_ref[T_POW + 2 * i + 1]
                h_scr[tile(i), :ns] = h_scr[tile(i), :ns] + pr * sr - pi * si
                h_scr[tile(i), ns:] = h_scr[tile(i), ns:] + pr * si + pi * sr
        else:
            finals.append((hr, hi))
    y_scr[...] = jnp.dot(h_scr[...].astype(BF16), wout_ref[...], preferred_element_type=F32)
    y = _transpose_tiles(y_scr, nsub) + d_ref[...] * uf
    return jax.nn.gelu(y).astype(BF16), (init if long_seq else finals)


def _s5_prompt_kernel(*refs, nb):
    u_refs = refs[:nb]
    win_ref, wout_ref, d_ref, tabs_ref, gl_ref, hre_ref, him_ref, uf_scr, bu_scr, h_scr, y_scr, carry_scr = refs[nb:]
    ns = S5_BLK_STATE

    @pl.when(pl.program_id(2) == 0)
    def _():
        carry_scr[...] = jnp.zeros_like(carry_scr)

    for s in range(nb):
        init = (carry_scr[s, 0:1, :ns], carry_scr[s, 0:1, ns:])
        out, (cr, ci) = _s5_stream(u_refs[s], win_ref, wout_ref, d_ref, tabs_ref, uf_scr.at[s], bu_scr.at[s],
                                   h_scr.at[s], y_scr.at[s], init, True)
        gl_ref[s] = out
        carry_scr[s, 0:1, :ns] = cr
        carry_scr[s, 0:1, ns:] = ci
        hre_ref[s] = cr
        him_ref[s] = ci


def _s5_sample_kernel(u_ref, win_ref, wout_ref, d_ref, tabs_ref, h0re_ref, h0im_ref, gl_ref, hre_ref, him_ref,
                      uf_scr, bu_scr, h_scr, y_scr):
    out, finals = _s5_stream(u_ref, win_ref, wout_ref, d_ref, tabs_ref, uf_scr, bu_scr, h_scr, y_scr,
                             (h0re_ref, h0im_ref), False)
    gl_ref[...] = out
    for m, (hr, hi) in enumerate(finals):
        hre_ref[SUBLANES * m:SUBLANES * (m + 1), :] = hr
        him_ref[SUBLANES * m:SUBLANES * (m + 1), :] = hi


def _s5_weight_specs():
    nst = S5_BLK_STATE
    return [
        pl.BlockSpec((None, LANES, 2 * nst), lambda j, *_: (j, 0, 0)),
        pl.BlockSpec((None, 2 * nst, LANES), lambda j, *_: (j, 0, 0)),
        pl.BlockSpec((1, LANES), lambda j, *_: (0, j)),
        pl.BlockSpec((None, N_TABS, SUBLANES, nst), lambda j, *_: (j, 0, 0, 0)),
    ]


def _s5_scratch(lead, tc):
    nst = S5_BLK_STATE
    return [pltpu.VMEM(lead + (tc, LANES), F32), pltpu.VMEM(lead + (tc, 2 * nst), F32),
            pltpu.VMEM(lead + (tc, 2 * nst), F32), pltpu.VMEM(lead + (tc, LANES), F32)]


def _s5_prompt(z, w_in, w_out, d_skip, tabs, nbatch, seqlen):
    tc = 256
    nb = 4
    nt = seqlen // tc
    nst = S5_BLK_STATE
    kern = functools.partial(_s5_prompt_kernel, nb=nb)
    u_spec = lambda s: pl.BlockSpec((tc, LANES), lambda j, bb, t: ((bb * nb + s) * nt + t, Z_U // LANES + j))
    state_spec = lambda: pl.BlockSpec((nb, 1, nst), lambda j, bb, t: (bb, 0, j))
    gl, hre, him = pl.pallas_call(
        kern,
        grid=(S5_NBLK, nbatch // nb, nt),
        in_specs=[u_spec(s) for s in range(nb)] + _s5_weight_specs(),
        out_specs=[pl.BlockSpec((nb, tc, LANES), lambda j, bb, t: (bb, t, j)), state_spec(), state_spec()],
        out_shape=[
            jax.ShapeDtypeStruct((nbatch, seqlen, S5_WIDTH), BF16),
            jax.ShapeDtypeStruct((nbatch, 1, S5_GROUPS * S5_STATE), F32),
            jax.ShapeDtypeStruct((nbatch, 1, S5_GROUPS * S5_STATE), F32),
        ],
        scratch_shapes=_s5_scratch((nb,), tc) + [pltpu.VMEM((nb, SUBLANES, 2 * nst), F32)],
        compiler_params=_cparams("parallel", "parallel", "arbitrary"),
        name="s5_prompt",
    )(*([z] * nb), w_in, w_out, d_skip, tabs)
    return (gl.reshape(nbatch * seqlen, S5_WIDTH), hre.reshape(nbatch, S5_GROUPS, S5_STATE),
            him.reshape(nbatch, S5_GROUPS, S5_STATE))


def _s5_sample(z, w_in, w_out, d_skip, tabs, h0_re, h0_im, row_off, nbatch, slen):
    assert slen == SUBLANES
    tc = 256
    nseq = tc // slen
    rb = row_off // tc
    nst = S5_BLK_STATE
    state_spec = lambda: pl.BlockSpec((nseq, nst), lambda j, i: (i, j))
    gl, hre, him = pl.pallas_call(
        _s5_sample_kernel,
        grid=(S5_NBLK, nbatch // nseq),
        in_specs=([pl.BlockSpec((tc, LANES), lambda j, i: (rb + i, Z_U // LANES + j))] + _s5_weight_specs()
                  + [state_spec(), state_spec()]),
        out_specs=[pl.BlockSpec((tc, LANES), lambda j, i: (i, j)), state_spec(), state_spec()],
        out_shape=[
            jax.ShapeDtypeStruct((nbatch * slen, S5_WIDTH), BF16),
            jax.ShapeDtypeStruct((nbatch, S5_GROUPS * S5_STATE), F32),
            jax.ShapeDtypeStruct((nbatch, S5_GROUPS * S5_STATE), F32),
        ],
        scratch_shapes=_s5_scratch((), tc),
        compiler_params=_cparams("parallel", "parallel"),
        name="s5_sample",
    )(z, w_in, w_out, d_skip, tabs, h0_re.reshape(nbatch, -1).astype(F32), h0_im.reshape(nbatch, -1).astype(F32))
    return gl, hre.reshape(nbatch, S5_GROUPS, S5_STATE), him.reshape(nbatch, S5_GROUPS, S5_STATE)


def _xattn_kernel(q_ref, g_ref, k_ref, v_ref, o_ref, *, q_per_seq, k_per_seq):
    tq = q_ref.shape[0]
    nk = k_ref.shape[0]
    kb = k_ref[...].astype(BF16)
    vb = v_ref[...].astype(BF16)
    if tq // q_per_seq > 1:
        same = (lax.broadcasted_iota(jnp.int32, (tq, nk), 0) // q_per_seq
                == lax.broadcasted_iota(jnp.int32, (tq, nk), 1) // k_per_seq)
    else:
        same = None
    for h in range(X_HEADS):
        cols = slice(h * X_HD, (h + 1) * X_HD)
        s = lax.dot_general(q_ref[:, cols], kb[:, cols], (((1,), (1,)), ((), ())),
                            preferred_element_type=F32) * (X_HD ** -0.5)
        if same is not None:
            s = jnp.where(same, s, -jnp.inf)
        e = jnp.exp(s - jnp.max(s, axis=-1, keepdims=True))
        oh = jnp.dot(e.astype(BF16), vb[:, cols], preferred_element_type=F32) / jnp.sum(e, axis=-1, keepdims=True)
        g = g_ref[:, cols].astype(F32)
        o_ref[:, cols] = (oh * (g * jax.nn.sigmoid(g))).astype(BF16)


def _xattn(z, mk, mv, *, row_off, nrows, tq, q_per_seq, seqs_per_step, kv_col_blk, name):
    nk = seqs_per_step * MEM_LEN
    steps_per_kv = (seqs_per_step * q_per_seq) // tq if tq < seqs_per_step * q_per_seq else 1
    rb = row_off // tq
    kern = functools.partial(_xattn_kernel, q_per_seq=q_per_seq, k_per_seq=MEM_LEN)
    kmap = lambda cb: (lambda i: (i // steps_per_kv, cb))
    return pl.pallas_call(
        kern,
        grid=(nrows // tq,),
        in_specs=[
            pl.BlockSpec((tq, X_WIDTH), lambda i: (rb + i, Z_QX // X_WIDTH)),
            pl.BlockSpec((tq, X_WIDTH), lambda i: (rb + i, Z_GX // X_WIDTH)),
            pl.BlockSpec((nk, X_WIDTH), kmap(kv_col_blk[0])),
            pl.BlockSpec((nk, X_WIDTH), kmap(kv_col_blk[1])),
        ],
        out_specs=pl.BlockSpec((tq, X_WIDTH), lambda i: (i, 0)),
        out_shape=jax.ShapeDtypeStruct((nrows, X_WIDTH), BF16),
        compiler_params=_cparams("parallel"),
        name=name,
    )(z, z, mk, mv)


def _cast_epilogue(prods, extras):
    return prods[0]


def _glu_epilogue(prods, extras):
    a, b = prods
    g = extras[0].astype(F32)
    return a * jax.nn.sigmoid(b) * (g * jax.nn.sigmoid(g))


def _glu(gl, w_glu, z, row_off, *, bm, bn):
    return _colmm([gl], [(0, w_glu, 0), (0, w_glu, S5_WIDTH)], [(z, row_off // bm, Z_GS5 // bn)], _glu_epilogue,
                  n_out=S5_WIDTH, bm=bm, bn=bn, out_dtype=BF16, name="glu")


def _merge_epilogue(prods, extras):
    out = jax.nn.sigmoid(extras[0].astype(F32)) * prods[0]
    for p, m in zip(prods[1:], extras[1:]):
        out = out + jax.nn.sigmoid(m.astype(F32)) * p
    return out


def _merge(o_a, o_b, o_c, w_a, w_b, w_c, z, row_off, *, bm, bn):
    rb = row_off // bm
    return _colmm([o_a, o_b, o_c], [(0, w_a, 0), (1, w_b, 0), (2, w_c, 0)],
                  [(z, rb, Z_MA // bn), (z, rb, Z_MB // bn), (z, rb, Z_MC // bn)], _merge_epilogue,
                  n_out=D_MODEL, bm=bm, bn=bn, out_dtype=BF16, name="merge")


def _out_ln_kernel(m_ref, w_ref, x_ref, g_ref, b_ref, o_ref, pre_scr, *, nj, bn):
    j = pl.program_id(1)
    pre_scr[j] = DN_ALPHA * x_ref[...] + jnp.dot(m_ref[...], w_ref[...], preferred_element_type=F32)

    @pl.when(j == nj - 1)
    def _():
        width = nj * bn
        tot = pre_scr[0].sum(axis=-1, keepdims=True)
        for t in range(1, nj):
            tot = tot + pre_scr[t].sum(axis=-1, keepdims=True)
        mu = tot / width
        sq = None
        for t in range(nj):
            d = pre_scr[t] - mu
            part = (d * d).sum(axis=-1, keepdims=True)
            sq = part if sq is None else sq + part
        rstd = lax.rsqrt(sq / width + LN_EPS)
        for t in range(nj):
            cols = slice(t * bn, (t + 1) * bn)
            o_ref[:, cols] = (pre_scr[t] - mu) * rstd * g_ref[:, cols] + b_ref[:, cols]


def _out_ln(merged, w_out, x, ln_g, ln_b, *, bm, bn):
    m, k = merged.shape
    nj = D_MODEL // bn
    kern = functools.partial(_out_ln_kernel, nj=nj, bn=bn)
    return pl.pallas_call(
        kern,
        grid=(m // bm, nj),
        in_specs=[
            pl.BlockSpec((bm, k), lambda i, j: (i, 0)),
            pl.BlockSpec((k, bn), lambda i, j: (0, j)),
            pl.BlockSpec((bm, bn), lambda i, j: (i, j)),
            pl.BlockSpec((1, D_MODEL), lambda i, j: (0, 0)),
            pl.BlockSpec((1, D_MODEL), lambda i, j: (0, 0)),
        ],
        out_specs=pl.BlockSpec((bm, D_MODEL), lambda i, j: (i, 0)),
        out_shape=jax.ShapeDtypeStruct((m, D_MODEL), F32),
        scratch_shapes=[pltpu.VMEM((nj, bm, bn), F32)],
        compiler_params=_cparams("parallel", "arbitrary"),
        name="out_ln",
    )(merged, w_out, x, ln_g.reshape(1, D_MODEL).astype(F32), ln_b.reshape(1, D_MODEL).astype(F32))


def _group_tail(z, row_off, o_ret, gl, o_x, x2d, w):
    o_s5 = _glu(gl, w["glu"], z, row_off, bm=1024, bn=512)
    merged = _merge(o_ret, o_s5, o_x, w["proj_a"], w["proj_b"], w["proj_c"], z, row_off, bm=512, bn=512)
    return _out_ln(merged, w["out"], x2d, w["ln_g"], w["ln_b"], bm=512, bn=512)


def kernel(x_prompt, x_sample, mem_prompt, state_ret, state_s5_re, state_s5_im, cache_mem_k, cache_mem_v, w_in, w_mem_kv, s5_a_re, s5_a_im, s5_log_step, s5_b_re, s5_b_im, s5_c_re, s5_c_im, s5_d, w_glu, w_proj_a, w_proj_b, w_proj_c, w_out, ln_g, ln_b):
    depth = w_in.shape[0]
    assert depth == 1
    l = 0
    n_p = BATCH * SEQ
    n_s = DEC_BATCH * DEC_SEQ
    xp2 = x_prompt.reshape(n_p, D_MODEL)
    xs2 = x_sample.reshape(n_s, D_MODEL)

    w = dict(glu=w_glu[l], proj_a=w_proj_a[l], proj_b=w_proj_b[l], proj_c=w_proj_c[l],
             out=w_out[l].astype(BF16), ln_g=ln_g[l], ln_b=ln_b[l])

    memb = mem_prompt.reshape(BATCH * MEM_LEN, D_MODEL).astype(BF16)
    xb, mk, mv = _front(xp2, xs2, memb, w_mem_kv[l], bm=512, bn=256)
    z = _colmm([xb], [(0, w_in[l], 0)], [], _cast_epilogue, n_out=IN_WIDTH, bm=1024, bn=1024, out_dtype=BF16,
               name="in_proj")

    s5_win, s5_wout, s5_tabs = _s5_prepare(s5_a_re[l], s5_a_im[l], s5_log_step[l], s5_b_re[l], s5_b_im[l],
                                           s5_c_re[l], s5_c_im[l])
    d_skip = s5_d[l].reshape(1, S5_WIDTH).astype(F32)

    o_ret_p, ret_p = _retention_prompt(z, BATCH, SEQ)
    gl_p, hre_p, him_p = _s5_prompt(z, s5_win, s5_wout, d_skip, s5_tabs, BATCH, SEQ)
    o_x_p = _xattn(z, mk, mv, row_off=0, nrows=n_p, tq=512, q_per_seq=SEQ, seqs_per_step=1,
                   kv_col_blk=(0, 0), name="xattn_prompt")
    y_p = _group_tail(z, 0, o_ret_p, gl_p, o_x_p, xp2, w)

    o_ret_s, ret_s = _retention_sample(z, state_ret[l], n_p, DEC_BATCH, DEC_SEQ, PAST_LEN)
    gl_s, hre_s, him_s = _s5_sample(z, s5_win, s5_wout, d_skip, s5_tabs, state_s5_re[l], state_s5_im[l],
                                    n_p, DEC_BATCH, DEC_SEQ)
    ck = cache_mem_k[l].reshape(DEC_BATCH * MEM_LEN, X_WIDTH)
    cv = cache_mem_v[l].reshape(DEC_BATCH * MEM_LEN, X_WIDTH)
    o_x_s = _xattn(z, ck, cv, row_off=n_p, nrows=n_s, tq=2 * DEC_SEQ, q_per_seq=DEC_SEQ, seqs_per_step=2,
                   kv_col_blk=(0, 0), name="xattn_sample")
    y_s = _group_tail(z, n_p, o_ret_s, gl_s, o_x_s, xs2, w)

    return (y_p.reshape(BATCH, SEQ, D_MODEL), y_s.reshape(DEC_BATCH, DEC_SEQ, D_MODEL),
            ret_p[None], hre_p[None], him_p[None],
            mk.reshape(1, BATCH, MEM_LEN, X_HEADS, X_HD), mv.reshape(1, BATCH, MEM_LEN, X_HEADS, X_HD),
            ret_s[None], hre_s[None], him_s[None])
```

```python
import functools
import math

import jax
import jax.numpy as jnp
import numpy as np
from jax import lax
from jax.experimental import pallas as pl
from jax.experimental.pallas import tpu as pltpu

F32 = jnp.float32
BF16 = jnp.bfloat16

D_MODEL = 4096
BATCH = 4
SEQ = 2048
DEC_BATCH = 128
DEC_SEQ = 8
PAST_LEN = 16384

RET_HEADS = 16
RET_DK = 128
RET_DV = 256
RET_QK = RET_HEADS * RET_DK
RET_V = RET_HEADS * RET_DV
RET_CHUNK = 128
ROPE_BASE = 10000.0

S5_WIDTH = D_MODEL // 2
S5_GROUP = 16
S5_GROUPS = S5_WIDTH // S5_GROUP
S5_STATE = 64

X_HEADS = 4
X_WIDTH = D_MODEL // 2
X_HD = X_WIDTH // X_HEADS
MEM_LEN = 256

DN_ALPHA = 2.0 ** 0.25
LN_EPS = 1e-5
GN_EPS = 1e-5

IN_WIDTH = 2 * RET_QK + 2 * RET_V + 2 * S5_WIDTH + 2 * X_WIDTH + 3 * D_MODEL

Z_Q = 0
Z_K = Z_Q + RET_QK
Z_V = Z_K + RET_QK
Z_GRET = Z_V + RET_V
Z_U = Z_GRET + RET_V
Z_GS5 = Z_U + S5_WIDTH
Z_QX = Z_GS5 + S5_WIDTH
Z_GX = Z_QX + X_WIDTH
Z_MA = Z_GX + X_WIDTH
Z_MB = Z_MA + D_MODEL
Z_MC = Z_MB + D_MODEL

SUBLANES = 8
LANES = 128
VMEM_PHYSICAL_BYTES = 64 * 1024 * 1024
VMEM_LIMIT_BYTES = 56 * 1024 * 1024
VMEM_TEMP_BYTES = 12 * 1024 * 1024

S5_BLK_GROUPS = LANES // S5_GROUP
S5_BLK_STATE = S5_BLK_GROUPS * S5_STATE
S5_NBLK = S5_GROUPS // S5_BLK_GROUPS
SCAN_LEVELS = (1, 2, 4)
T_A = 0
T_POW = 2
T_LVL = T_POW + 2 * SUBLANES
T_CARRY = T_LVL + 2 * len(SCAN_LEVELS)
N_TABS = T_CARRY + 2


def _cparams(*sem, vmem_limit_bytes=VMEM_LIMIT_BYTES):
    return pltpu.CompilerParams(dimension_semantics=sem, vmem_limit_bytes=vmem_limit_bytes)


def _concat_cast_kernel(a_ref, b_ref, o_ref, *, na):
    i = pl.program_id(0)

    @pl.when(i < na)
    def _():
        o_ref[...] = a_ref[...].astype(o_ref.dtype)

    @pl.when(i >= na)
    def _():
        o_ref[...] = b_ref[...].astype(o_ref.dtype)


def _concat_cast(a, b, *, bm, out_dtype):
    k = a.shape[1]
    na, nb = a.shape[0] // bm, b.shape[0] // bm
    return pl.pallas_call(
        functools.partial(_concat_cast_kernel, na=na),
        grid=(na + nb,),
        in_specs=[
            pl.BlockSpec((bm, k), lambda i: (jnp.minimum(i, na - 1), 0)),
            pl.BlockSpec((bm, k), lambda i: (jnp.maximum(i - na, 0), 0), pipeline_mode=pl.Buffered(1)),
        ],
        out_specs=pl.BlockSpec((bm, k), lambda i: (i, 0)),
        out_shape=jax.ShapeDtypeStruct((a.shape[0] + b.shape[0], k), out_dtype),
        compiler_params=_cparams("arbitrary"),
        name="concat_cast",
    )(a, b)


def _colmm_kernel(*refs, n_lhs, terms, n_extra, epilogue, ncols, bn):
    lhs = refs[:n_lhs]
    w_hbm = refs[n_lhs:n_lhs + len(terms)]
    extras = refs[n_lhs + len(terms):n_lhs + len(terms) + n_extra]
    o_ref, stage, wb_scr, sem = refs[n_lhs + len(terms) + n_extra:]
    j = pl.program_id(0)
    i = pl.program_id(1)

    def tile_copies(col):
        return [pltpu.make_async_copy(
            w_hbm[t].at[:, pl.ds(pl.multiple_of(off + col * bn, LANES), bn)],
            stage.at[pl.ds(row0, kt), :], sem.at[t]) for t, (_, row0, kt, off) in enumerate(terms)]

    @pl.when(i == 0)
    def _():
        @pl.when(j == 0)
        def _():
            for c in tile_copies(0):
                c.start()

        for c in tile_copies(j):
            c.wait()
        wb_scr[...] = stage[...].astype(BF16)

        @pl.when(j + 1 < ncols)
        def _():
            for c in tile_copies(j + 1):
                c.start()

    prods = [jnp.dot(lhs[li][...], wb_scr[row0:row0 + kt, :], preferred_element_type=F32)
             for li, row0, kt, _ in terms]
    o_ref[...] = epilogue(prods, [e[...] for e in extras]).astype(o_ref.dtype)


def _colmm(lhs, weights, extras, epilogue, *, n_out, bm, bn, out_dtype, name):
    m = lhs[0].shape[0]
    terms, row0 = [], 0
    for li, w, off in weights:
        kt = w.shape[0]
        assert lhs[li].shape == (m, kt) and off % LANES == 0
        terms.append((li, row0, kt, off))
        row0 += kt
    ktot = row0
    lhs_bytes = sum(2 * bm * a.shape[1] * a.dtype.itemsize for a in lhs)
    extra_bytes = sum(2 * bm * bn * a.dtype.itemsize for a, _, _ in extras)
    need = (ktot * bn * 6 + lhs_bytes + extra_bytes + 2 * bm * bn * jnp.dtype(out_dtype).itemsize
            + len(terms) * bm * bn * 4)
    limit = min(max(VMEM_LIMIT_BYTES, need + VMEM_TEMP_BYTES), VMEM_PHYSICAL_BYTES - (2 << 20))
    assert need + (4 << 20) <= limit, (name, need, limit)
    kern = functools.partial(_colmm_kernel, n_lhs=len(lhs), terms=tuple(terms), n_extra=len(extras),
                             epilogue=epilogue, ncols=n_out // bn, bn=bn)
    extra_spec = lambda rb, cb: pl.BlockSpec((bm, bn), lambda j, i: (rb + i, cb + j))
    return pl.pallas_call(
        kern,
        grid=(n_out // bn, m // bm),
        in_specs=([pl.BlockSpec((bm, a.shape[1]), lambda j, i: (i, 0)) for a in lhs]
                  + [pl.BlockSpec(memory_space=pl.ANY) for _ in terms]
                  + [extra_spec(rb, cb) for _, rb, cb in extras]),
        out_specs=pl.BlockSpec((bm, bn), lambda j, i: (i, j)),
        out_shape=jax.ShapeDtypeStruct((m, n_out), out_dtype),
        scratch_shapes=[pltpu.VMEM((ktot, bn), F32), pltpu.VMEM((ktot, bn), BF16),
                        pltpu.SemaphoreType.DMA((len(terms),))],
        compiler_params=_cparams("arbitrary", "arbitrary", vmem_limit_bytes=limit),
        name=name,
    )(*lhs, *[w for _, w, _ in weights], *[a for a, _, _ in extras])


def _rotate(x, cos, sin_next, sin_prev):
    return x * cos + pltpu.roll(x, LANES - 1, 1) * sin_next + pltpu.roll(x, 1, 1) * sin_prev


def _ret_block(q, k, v, g, cos, sin_next, sin_prev, mask, xi, zeta, gc, states, slen):
    nseq = len(states)
    rows = q.shape[0]
    qr = _rotate(q, cos, sin_next, sin_prev)
    kr = _rotate(k, cos, sin_next, sin_prev) * (RET_DK ** -0.5)
    qb = qr.astype(BF16)
    kb = kr.astype(BF16)
    sc = lax.dot_general(qb, kb, (((1,), (1,)), ((), ())), preferred_element_type=F32) * mask
    o = jnp.dot(sc.astype(BF16), v, preferred_element_type=F32)
    qx = qr * xi
    kzt = (kr * zeta).T.astype(BF16)
    new_states = []
    if nseq == 1:
        s = states[0]
        o = o + jnp.dot(qx.astype(BF16), s.astype(BF16), preferred_element_type=F32)
        new_states.append(gc * s + jnp.dot(kzt, v, preferred_element_type=F32))
    else:
        pair = 2 * slen
        assert pair == 2 * SUBLANES and nseq % 2 == 0
        row_in_pair = lax.broadcasted_iota(jnp.int32, (pair, RET_DV), 0)
        row_seq = lax.broadcasted_iota(jnp.int32, (rows, RET_DV), 0) // slen
        parts = []
        for m in range(nseq // 2):
            qpair = qx[m * pair:(m + 1) * pair].astype(BF16)
            o0 = jnp.dot(qpair, states[2 * m].astype(BF16), preferred_element_type=F32)
            o1 = jnp.dot(qpair, states[2 * m + 1].astype(BF16), preferred_element_type=F32)
            parts.append(jnp.where(row_in_pair < slen, o0, o1))
        o = o + jnp.concatenate(parts, axis=0)
        vf = v.astype(F32)
        for n in range(nseq):
            vn = jnp.where(row_seq == n, vf, 0.0).astype(BF16)
            new_states.append(gc * states[n] + jnp.dot(kzt, vn, preferred_element_type=F32))
    mu = jnp.mean(o, axis=-1, keepdims=True)
    d = o - mu
    var = jnp.mean(d * d, axis=-1, keepdims=True)
    on = d * lax.rsqrt(var + GN_EPS)
    out = (on * (g * jax.nn.sigmoid(g))).astype(BF16)
    return out, new_states


def _ret_prompt_kernel(gc_ref, q_ref, k_ref, v_ref, g_ref, cos_ref, sn_ref, sp_ref, mask_ref, xi_ref, zeta_ref,
                       o_ref, sfin_ref, s_scr, *, chunk, nchunks, hb):
    head0 = pl.program_id(1) * hb
    s_scr[...] = jnp.zeros_like(s_scr)

    def body(c, carry):
        rows = pl.ds(pl.multiple_of(c * chunk, chunk), chunk)
        cos, sn, sp = cos_ref[rows, :], sn_ref[rows, :], sp_ref[rows, :]
        for hh in range(hb):
            qc = slice(hh * RET_DK, (hh + 1) * RET_DK)
            vc = slice(hh * RET_DV, (hh + 1) * RET_DV)
            out, (s_new,) = _ret_block(
                q_ref[rows, qc].astype(F32), k_ref[rows, qc].astype(F32), v_ref[rows, vc],
                g_ref[rows, vc].astype(F32), cos, sn, sp, mask_ref[hh], xi_ref[hh], zeta_ref[hh],
                gc_ref[head0 + hh], [s_scr[hh]], chunk)
            o_ref[rows, vc] = out
            s_scr[hh] = s_new
        return carry

    lax.fori_loop(0, nchunks, body, 0)
    sfin_ref[...] = s_scr[...]


def _ret_sample_kernel(gc_ref, q_ref, k_ref, v_ref, g_ref, cos_ref, sn_ref, sp_ref, mask_ref, xi_ref, zeta_ref,
                       s0_ref, o_ref, sfin_ref, *, slen, nseq):
    gc = gc_ref[pl.program_id(1)]
    states = [s0_ref[n] for n in range(nseq)]
    out, new_states = _ret_block(
        q_ref[...].astype(F32), k_ref[...].astype(F32), v_ref[...], g_ref[...].astype(F32),
        cos_ref[...], sn_ref[...], sp_ref[...], mask_ref[...], xi_ref[...], zeta_ref[...], gc, states, slen)
    o_ref[...] = out
    for n in range(nseq):
        sfin_ref[n] = new_states[n]


def _rope_tables(pos):
    half = RET_DK // 2
    inv = 1.0 / (ROPE_BASE ** (np.arange(half, dtype=np.float64) / half))
    ang = np.asarray(pos, np.float64)[:, None] * inv[None, :]
    cos = np.repeat(np.cos(ang), 2, axis=1)
    sin = np.repeat(np.sin(ang), 2, axis=1)
    even = (np.arange(RET_DK) % 2) == 0
    return (np.asarray(cos, np.float32), np.asarray(np.where(even, -sin, 0.0), np.float32),
            np.asarray(np.where(even, 0.0, sin), np.float32))


def _decay_tables(slen, nseq):
    lg = np.log1p(-np.exp2(-5.0 - np.arange(RET_HEADS, dtype=np.float64)))
    idx = np.arange(slen, dtype=np.float64)
    rel = idx[:, None] - idx[None, :]
    inner = np.where(rel[None] >= 0, np.exp(lg[:, None, None] * np.maximum(rel, 0.0)[None]), 0.0)
    xi = np.exp(lg[:, None] * (idx + 1.0))
    zeta = np.exp(lg[:, None] * (slen - 1.0 - idx))
    gc = np.exp(lg * slen)
    mask = np.einsum("nm,hij->hnimj", np.eye(nseq), inner).reshape(RET_HEADS, nseq * slen, nseq * slen)
    rows = nseq * slen
    xi_t = np.broadcast_to(np.tile(xi, (1, nseq))[:, :, None], (RET_HEADS, rows, RET_DK))
    zeta_t = np.broadcast_to(np.tile(zeta, (1, nseq))[:, :, None], (RET_HEADS, rows, RET_DK))
    f32 = lambda t: np.ascontiguousarray(t, dtype=np.float32)
    return f32(mask), f32(xi_t), f32(zeta_t), f32(gc)


def _retention_prompt(z, nbatch, seqlen):
    chunk = RET_CHUNK
    hb = 4
    cos, sn, sp = _rope_tables(np.arange(seqlen))
    mask, xi, zeta, gc = _decay_tables(chunk, 1)
    tab = lambda: pl.BlockSpec((seqlen, RET_DK), lambda b, h: (0, 0))
    head_tab = lambda w: pl.BlockSpec((hb, chunk, w), lambda b, h: (h, 0, 0))
    kern = functools.partial(_ret_prompt_kernel, chunk=chunk, nchunks=seqlen // chunk, hb=hb)
    qk_w, v_w = hb * RET_DK, hb * RET_DV
    return pl.pallas_call(
        kern,
        grid=(nbatch, RET_HEADS // hb),
        in_specs=[
            pl.BlockSpec(memory_space=pltpu.SMEM),
            pl.BlockSpec((seqlen, qk_w), lambda b, h: (b, Z_Q // qk_w + h)),
            pl.BlockSpec((seqlen, qk_w), lambda b, h: (b, Z_K // qk_w + h)),
            pl.BlockSpec((seqlen, v_w), lambda b, h: (b, Z_V // v_w + h)),
            pl.BlockSpec((seqlen, v_w), lambda b, h: (b, Z_GRET // v_w + h)),
            tab(), tab(), tab(),
            head_tab(chunk), head_tab(RET_DK), head_tab(RET_DK),
        ],
        out_specs=[
            pl.BlockSpec((seqlen, v_w), lambda b, h: (b, h)),
            pl.BlockSpec((None, hb, RET_DK, RET_DV), lambda b, h: (b, h, 0, 0)),
        ],
        out_shape=[
            jax.ShapeDtypeStruct((nbatch * seqlen, RET_V), BF16),
            jax.ShapeDtypeStruct((nbatch, RET_HEADS, RET_DK, RET_DV), F32),
        ],
        scratch_shapes=[pltpu.VMEM((hb, RET_DK, RET_DV), F32)],
        compiler_params=_cparams("parallel", "parallel"),
        name="retention_prompt",
    )(gc, z, z, z, z, cos, sn, sp, mask, xi, zeta)


def _retention_sample(z, s0, row_off, nbatch, slen, pos0):
    nseq = 32
    rows = nseq * slen
    cos, sn, sp = (np.tile(t, (nseq, 1)) for t in _rope_tables(pos0 + np.arange(slen)))
    mask, xi, zeta, gc = _decay_tables(slen, nseq)
    rb = row_off // rows
    tab = lambda: pl.BlockSpec((rows, RET_DK), lambda i, h: (0, 0))
    head_tab = lambda w: pl.BlockSpec((None, rows, w), lambda i, h: (h, 0, 0))
    kern = functools.partial(_ret_sample_kernel, slen=slen, nseq=nseq)
    return pl.pallas_call(
        kern,
        grid=(nbatch // nseq, RET_HEADS),
        in_specs=[
            pl.BlockSpec(memory_space=pltpu.SMEM),
            pl.BlockSpec((rows, RET_DK), lambda i, h: (rb + i, Z_Q // RET_DK + h)),
            pl.BlockSpec((rows, RET_DK), lambda i, h: (rb + i, Z_K // RET_DK + h)),
            pl.BlockSpec((rows, RET_DV), lambda i, h: (rb + i, Z_V // RET_DV + h)),
            pl.BlockSpec((rows, RET_DV), lambda i, h: (rb + i, Z_GRET // RET_DV + h)),
            tab(), tab(), tab(),
            head_tab(rows), head_tab(RET_DK), head_tab(RET_DK),
            pl.BlockSpec((nseq, None, RET_DK, RET_DV), lambda i, h: (i, h, 0, 0)),
        ],
        out_specs=[
            pl.BlockSpec((rows, RET_DV), lambda i, h: (i, h)),
            pl.BlockSpec((nseq, None, RET_DK, RET_DV), lambda i, h: (i, h, 0, 0)),
        ],
        out_shape=[
            jax.ShapeDtypeStruct((nbatch * slen, RET_V), BF16),
            jax.ShapeDtypeStruct((nbatch, RET_HEADS, RET_DK, RET_DV), F32),
        ],
        compiler_params=_cparams("parallel", "parallel"),
        name="retention_sample",
    )(gc, z, z, z, z, cos, sn, sp, mask, xi, zeta, s0)


def _s5_prep_kernel(are_ref, aim_ref, dt_ref, are_w_ref, aim_w_ref, dt_w_ref, bre_ref, bim_ref, cim_ref,
                    tabs_ref, bbre_ref, bbim_ref, ncim_ref):
    def abar(ar, ai, dt):
        mag = jnp.exp(dt * ar)
        return mag * jnp.cos(dt * ai), mag * jnp.sin(dt * ai)

    def powers(ar, ai):
        out = [(ar, ai)]
        for _ in range(1, SUBLANES):
            pr, pi = out[-1]
            out.append((pr * ar - pi * ai, pr * ai + pi * ar))
        return out

    pw = powers(*abar(are_ref[...], aim_ref[...], dt_ref[...]))
    qw = powers(*pw[-1])
    k_idx = lax.broadcasted_iota(jnp.int32, (SUBLANES, S5_BLK_STATE), 0)
    for j in range(S5_NBLK):
        every_row = lambda v: jnp.broadcast_to(v[j:j + 1, :], (SUBLANES, S5_BLK_STATE))
        for c in range(2):
            tabs_ref[j, T_A + c] = every_row(pw[0][c])
            for i in range(SUBLANES):
                tabs_ref[j, T_POW + 2 * i + c] = every_row(pw[i][c])
            for l, lvl in enumerate(SCAN_LEVELS):
                tabs_ref[j, T_LVL + 2 * l + c] = jnp.where(k_idx >= lvl, every_row(qw[lvl - 1][c]), 0.0)
            carry = every_row(qw[0][c])
            for k in range(1, SUBLANES):
                carry = jnp.where(k_idx == k, every_row(qw[k][c]), carry)
            tabs_ref[j, T_CARRY + c] = carry

    a_r, a_i = are_w_ref[...], aim_w_ref[...]
    w_r, w_i = abar(a_r, a_i, dt_w_ref[...])
    den = a_r * a_r + a_i * a_i
    x_re = w_r - 1.0
    f_re = (x_re * a_r + w_i * a_i) / den
    f_im = (w_i * a_r - x_re * a_i) / den
    br, bi = bre_ref[...], bim_ref[...]
    bbre_ref[...] = f_re * br - f_im * bi
    bbim_ref[...] = f_re * bi + f_im * br
    ncim_ref[...] = -cim_ref[...]


def _s5_prepare(a_re, a_im, log_step, b_re, b_im, c_re, c_im):
    g, n, p = S5_GROUPS, S5_STATE, S5_GROUP
    dt = jnp.broadcast_to(jnp.exp(log_step.astype(F32))[:, None], (g, n))
    wide = lambda t: jnp.repeat(t, p, axis=1)
    vm = lambda: pl.BlockSpec(memory_space=pltpu.VMEM)
    nb, bg = S5_NBLK, S5_BLK_GROUPS
    per_blk = lambda t: t.reshape(nb, S5_BLK_STATE)
    tabs, bbre, bbim, ncim = pl.pallas_call(
        _s5_prep_kernel,
        in_specs=[vm() for _ in range(9)],
        out_specs=[vm() for _ in range(4)],
        out_shape=[
            jax.ShapeDtypeStruct((nb, N_TABS, SUBLANES, S5_BLK_STATE), F32),
            jax.ShapeDtypeStruct((g, n * p), F32),
            jax.ShapeDtypeStruct((g, n * p), F32),
            jax.ShapeDtypeStruct((g, p * n), F32),
        ],
        name="s5_discretize",
    )(per_blk(a_re.astype(F32)), per_blk(a_im.astype(F32)), per_blk(dt),
      wide(a_re.astype(F32)), wide(a_im.astype(F32)), wide(dt),
      b_re.astype(F32).reshape(g, n * p), b_im.astype(F32).reshape(g, n * p), c_im.astype(F32).reshape(g, p * n))

    eye = jnp.eye(bg, dtype=bool)

    def in_blockdiag(t):
        t = t.reshape(nb, bg, n, p).transpose(0, 1, 3, 2)
        return jnp.where(eye[None, :, None, :, None], t[:, :, :, None, :], 0.0).reshape(nb, bg * p, bg * n)

    def out_blockdiag(t):
        t = t.reshape(nb, bg, p, n).transpose(0, 1, 3, 2)
        return jnp.where(eye[None, :, None, :, None], t[:, :, :, None, :], 0.0).reshape(nb, bg * n, bg * p)

    w_in = jnp.concatenate([in_blockdiag(bbre), in_blockdiag(bbim)], axis=-1).astype(BF16)
    w_out = jnp.concatenate([out_blockdiag(c_re.astype(F32).reshape(g, p * n)), out_blockdiag(ncim)],
                            axis=1).astype(BF16)
    return w_in, w_out, tabs


S5_SUB = SUBLANES * SUBLANES


def _segment_scan(er, ei, tabs_ref, cr, ci):
    for l, lvl in enumerate(SCAN_LEVELS):
        pr = tabs_ref[T_LVL + 2 * l]
        pi = tabs_ref[T_LVL + 2 * l + 1]
        sr = pltpu.roll(er, lvl, 0)
        si = pltpu.roll(ei, lvl, 0)
        er, ei = er + pr * sr - pi * si, ei + pr * si + pi * sr
    rr = tabs_ref[T_CARRY]
    ri = tabs_ref[T_CARRY + 1]
    crb = jnp.broadcast_to(cr, er.shape)
    cib = jnp.broadcast_to(ci, ei.shape)
    return er + rr * crb - ri * cib, ei + rr * cib + ri * crb


def _transpose_tiles(ref, nsub):
    return jnp.concatenate([ref[pl.ds(S5_SUB * m + i, SUBLANES, stride=SUBLANES), :]
                            for m in range(nsub) for i in range(SUBLANES)], axis=0)


def _s5_stream(u_ref, win_ref, wout_ref, d_ref, tabs_ref, uf_scr, bu_scr, h_scr, y_scr, init, long_seq):
    tc = u_ref.shape[0]
    nsub = tc // S5_SUB
    ns = S5_BLK_STATE
    uf = u_ref[...].astype(F32)
    uf_scr[...] = uf
    up = _transpose_tiles(uf_scr, nsub).astype(BF16)
    bu_scr[...] = jnp.dot(up, win_ref[...], preferred_element_type=F32)
    ar = tabs_ref[T_A]
    ai = tabs_ref[T_A + 1]
    finals = []
    for m in range(nsub):
        tile = lambda i: slice(S5_SUB * m + SUBLANES * i, S5_SUB * m + SUBLANES * (i + 1))
        if long_seq:
            hr = hi = None
        else:
            hr = init[0][SUBLANES * m:SUBLANES * (m + 1), :]
            hi = init[1][SUBLANES * m:SUBLANES * (m + 1), :]
        for i in range(SUBLANES):
            xr = bu_scr[tile(i), :ns]
            xi = bu_scr[tile(i), ns:]
            if hr is None:
                hr, hi = xr, xi
            else:
                hr, hi = ar * hr - ai * hi + xr, ar * hi + ai * hr + xi
            h_scr[tile(i), :ns] = hr
            h_scr[tile(i), ns:] = hi
        if long_seq:
            cr, ci = init
            fr, fi = _segment_scan(hr, hi, tabs_ref, cr, ci)
            first = lax.broadcasted_iota(jnp.int32, fr.shape, 0) == 0
            sr = jnp.where(first, jnp.broadcast_to(cr, fr.shape), pltpu.roll(fr, 1, 0))
            si = jnp.where(first, jnp.broadcast_to(ci, fi.shape), pltpu.roll(fi, 1, 0))
            init = (fr[SUBLANES - 1:, :], fi[SUBLANES - 1:, :])
            for i in range(SUBLANES):
                pr = tabs_ref[T_POW + 2 * i]
                pi = tabs_ref[T_POW + 2 * i + 1]
                h_scr[tile(i), :ns] = h_scr[tile(i), :ns] + pr * sr - pi * si
                h_scr[tile(i), ns:] = h_scr[tile(i), ns:] + pr * si + pi * sr
        else:
            finals.append((hr, hi))
    y_scr[...] = jnp.dot(h_scr[...].astype(BF16), wout_ref[...], preferred_element_type=F32)
    y = _transpose_tiles(y_scr, nsub) + d_ref[...] * uf
    return jax.nn.gelu(y).astype(BF16), (init if long_seq else finals)


def _s5_prompt_kernel(*refs, nb):
    u_refs = refs[:nb]
    win_ref, wout_ref, d_ref, tabs_ref, gl_ref, hre_ref, him_ref, uf_scr, bu_scr, h_scr, y_scr, carry_scr = refs[nb:]
    ns = S5_BLK_STATE

    @pl.when(pl.program_id(2) == 0)
    def _():
        carry_scr[...] = jnp.zeros_like(carry_scr)

    for s in range(nb):
        init = (carry_scr[s, 0:1, :ns], carry_scr[s, 0:1, ns:])
        out, (cr, ci) = _s5_stream(u_refs[s], win_ref, wout_ref, d_ref, tabs_ref, uf_scr.at[s], bu_scr.at[s],
                                   h_scr.at[s], y_scr.at[s], init, True)
        gl_ref[s] = out
        carry_scr[s, 0:1, :ns] = cr
        carry_scr[s, 0:1, ns:] = ci
        hre_ref[s] = cr
        him_ref[s] = ci


def _s5_sample_kernel(u_ref, win_ref, wout_ref, d_ref, tabs_ref, h0re_ref, h0im_ref, gl_ref, hre_ref, him_ref,
                      uf_scr, bu_scr, h_scr, y_scr):
    out, finals = _s5_stream(u_ref, win_ref, wout_ref, d_ref, tabs_ref, uf_scr, bu_scr, h_scr, y_scr,
                             (h0re_ref, h0im_ref), False)
    gl_ref[...] = out
    for m, (hr, hi) in enumerate(finals):
        hre_ref[SUBLANES * m:SUBLANES * (m + 1), :] = hr
        him_ref[SUBLANES * m:SUBLANES * (m + 1), :] = hi


def _s5_weight_specs():
    nst = S5_BLK_STATE
    return [
        pl.BlockSpec((None, LANES, 2 * nst), lambda j, *_: (j, 0, 0)),
        pl.BlockSpec((None, 2 * nst, LANES), lambda j, *_: (j, 0, 0)),
        pl.BlockSpec((1, LANES), lambda j, *_: (0, j)),
        pl.BlockSpec((None, N_TABS, SUBLANES, nst), lambda j, *_: (j, 0, 0, 0)),
    ]


def _s5_scratch(lead, tc):
    nst = S5_BLK_STATE
    return [pltpu.VMEM(lead + (tc, LANES), F32), pltpu.VMEM(lead + (tc, 2 * nst), F32),
            pltpu.VMEM(lead + (tc, 2 * nst), F32), pltpu.VMEM(lead + (tc, LANES), F32)]


def _s5_prompt(z, w_in, w_out, d_skip, tabs, nbatch, seqlen):
    tc = 256
    nb = 4
    nt = seqlen // tc
    nst = S5_BLK_STATE
    kern = functools.partial(_s5_prompt_kernel, nb=nb)
    u_spec = lambda s: pl.BlockSpec((tc, LANES), lambda j, bb, t: ((bb * nb + s) * nt + t, Z_U // LANES + j))
    state_spec = lambda: pl.BlockSpec((nb, 1, nst), lambda j, bb, t: (bb, 0, j))
    gl, hre, him = pl.pallas_call(
        kern,
        grid=(S5_NBLK, nbatch // nb, nt),
        in_specs=[u_spec(s) for s in range(nb)] + _s5_weight_specs(),
        out_specs=[pl.BlockSpec((nb, tc, LANES), lambda j, bb, t: (bb, t, j)), state_spec(), state_spec()],
        out_shape=[
            jax.ShapeDtypeStruct((nbatch, seqlen, S5_WIDTH), BF16),
            jax.ShapeDtypeStruct((nbatch, 1, S5_GROUPS * S5_STATE), F32),
            jax.ShapeDtypeStruct((nbatch, 1, S5_GROUPS * S5_STATE), F32),
        ],
        scratch_shapes=_s5_scratch((nb,), tc) + [pltpu.VMEM((nb, SUBLANES, 2 * nst), F32)],
        compiler_params=_cparams("parallel", "parallel", "arbitrary"),
        name="s5_prompt",
    )(*([z] * nb), w_in, w_out, d_skip, tabs)
    return (gl.reshape(nbatch * seqlen, S5_WIDTH), hre.reshape(nbatch, S5_GROUPS, S5_STATE),
            him.reshape(nbatch, S5_GROUPS, S5_STATE))


def _s5_sample(z, w_in, w_out, d_skip, tabs, h0_re, h0_im, row_off, nbatch, slen):
    assert slen == SUBLANES
    tc = 256
    nseq = tc // slen
    rb = row_off // tc
    nst = S5_BLK_STATE
    state_spec = lambda: pl.BlockSpec((nseq, nst), lambda j, i: (i, j))
    gl, hre, him = pl.pallas_call(
        _s5_sample_kernel,
        grid=(S5_NBLK, nbatch // nseq),
        in_specs=([pl.BlockSpec((tc, LANES), lambda j, i: (rb + i, Z_U // LANES + j))] + _s5_weight_specs()
                  + [state_spec(), state_spec()]),
        out_specs=[pl.BlockSpec((tc, LANES), lambda j, i: (i, j)), state_spec(), state_spec()],
        out_shape=[
            jax.ShapeDtypeStruct((nbatch * slen, S5_WIDTH), BF16),
            jax.ShapeDtypeStruct((nbatch, S5_GROUPS * S5_STATE), F32),
            jax.ShapeDtypeStruct((nbatch, S5_GROUPS * S5_STATE), F32),
        ],
        scratch_shapes=_s5_scratch((), tc),
        compiler_params=_cparams("parallel", "parallel"),
        name="s5_sample",
    )(z, w_in, w_out, d_skip, tabs, h0_re.reshape(nbatch, -1).astype(F32), h0_im.reshape(nbatch, -1).astype(F32))
    return gl, hre.reshape(nbatch, S5_GROUPS, S5_STATE), him.reshape(nbatch, S5_GROUPS, S5_STATE)


def _xattn_kernel(q_ref, g_ref, k_ref, v_ref, o_ref, *, q_per_seq, k_per_seq):
    tq = q_ref.shape[0]
    nk = k_ref.shape[0]
    kb = k_ref[...].astype(BF16)
    vb = v_ref[...].astype(BF16)
    if tq // q_per_seq > 1:
        same = (lax.broadcasted_iota(jnp.int32, (tq, nk), 0) // q_per_seq
                == lax.broadcasted_iota(jnp.int32, (tq, nk), 1) // k_per_seq)
    else:
        same = None
    for h in range(X_HEADS):
        cols = slice(h * X_HD, (h + 1) * X_HD)
        s = lax.dot_general(q_ref[:, cols], kb[:, cols], (((1,), (1,)), ((), ())),
                            preferred_element_type=F32) * (X_HD ** -0.5)
        if same is not None:
            s = jnp.where(same, s, -jnp.inf)
        e = jnp.exp(s - jnp.max(s, axis=-1, keepdims=True))
        oh = jnp.dot(e.astype(BF16), vb[:, cols], preferred_element_type=F32) / jnp.sum(e, axis=-1, keepdims=True)
        g = g_ref[:, cols].astype(F32)
        o_ref[:, cols] = (oh * (g * jax.nn.sigmoid(g))).astype(BF16)


def _xattn(z, mk, mv, *, row_off, nrows, tq, q_per_seq, seqs_per_step, kv_col_blk, name):
    nk = seqs_per_step * MEM_LEN
    steps_per_kv = (seqs_per_step * q_per_seq) // tq if tq < seqs_per_step * q_per_seq else 1
    rb = row_off // tq
    kern = functools.partial(_xattn_kernel, q_per_seq=q_per_seq, k_per_seq=MEM_LEN)
    kmap = lambda cb: (lambda i: (i // steps_per_kv, cb))
    return pl.pallas_call(
        kern,
        grid=(nrows // tq,),
        in_specs=[
            pl.BlockSpec((tq, X_WIDTH), lambda i: (rb + i, Z_QX // X_WIDTH)),
            pl.BlockSpec((tq, X_WIDTH), lambda i: (rb + i, Z_GX // X_WIDTH)),
            pl.BlockSpec((nk, X_WIDTH), kmap(kv_col_blk[0])),
            pl.BlockSpec((nk, X_WIDTH), kmap(kv_col_blk[1])),
        ],
        out_specs=pl.BlockSpec((tq, X_WIDTH), lambda i: (i, 0)),
        out_shape=jax.ShapeDtypeStruct((nrows, X_WIDTH), BF16),
        compiler_params=_cparams("parallel"),
        name=name,
    )(z, z, mk, mv)


def _cast_epilogue(prods, extras):
    return prods[0]


def _glu_epilogue(prods, extras):
    a, b = prods
    g = extras[0].astype(F32)
    return a * jax.nn.sigmoid(b) * (g * jax.nn.sigmoid(g))


def _glu(gl, w_glu, z, row_off, *, bm, bn):
    return _colmm([gl], [(0, w_glu, 0), (0, w_glu, S5_WIDTH)], [(z, row_off // bm, Z_GS5 // bn)], _glu_epilogue,
                  n_out=S5_WIDTH, bm=bm, bn=bn, out_dtype=BF16, name="glu")


def _merge_epilogue(prods, extras):
    out = jax.nn.sigmoid(extras[0].astype(F32)) * prods[0]
    for p, m in zip(prods[1:], extras[1:]):
        out = out + jax.nn.sigmoid(m.astype(F32)) * p
    return out


def _merge(o_a, o_b, o_c, w_a, w_b, w_c, z, row_off, *, bm, bn):
    rb = row_off // bm
    return _colmm([o_a, o_b, o_c], [(0, w_a, 0), (1, w_b, 0), (2, w_c, 0)],
                  [(z, rb, Z_MA // bn), (z, rb, Z_MB // bn), (z, rb, Z_MC // bn)], _merge_epilogue,
                  n_out=D_MODEL, bm=bm, bn=bn, out_dtype=BF16, name="merge")


def _out_ln_kernel(m_ref, w_ref, x_ref, g_ref, b_ref, o_ref, pre_scr, *, nj, bn):
    j = pl.program_id(1)
    pre_scr[j] = DN_ALPHA * x_ref[...] + jnp.dot(m_ref[...], w_ref[...], preferred_element_type=F32)

    @pl.when(j == nj - 1)
    def _():
        width = nj * bn
        tot = pre_scr[0].sum(axis=-1, keepdims=True)
        for t in range(1, nj):
            tot = tot + pre_scr[t].sum(axis=-1, keepdims=True)
        mu = tot / width
        sq = None
        for t in range(nj):
            d = pre_scr[t] - mu
            part = (d * d).sum(axis=-1, keepdims=True)
            sq = part if sq is None else sq + part
        rstd = lax.rsqrt(sq / width + LN_EPS)
        for t in range(nj):
            cols = slice(t * bn, (t + 1) * bn)
            o_ref[:, cols] = (pre_scr[t] - mu) * rstd * g_ref[:, cols] + b_ref[:, cols]


def _out_ln(merged, w_out, x, ln_g, ln_b, *, bm, bn):
    m, k = merged.shape
    nj = D_MODEL // bn
    kern = functools.partial(_out_ln_kernel, nj=nj, bn=bn)
    return pl.pallas_call(
        kern,
        grid=(m // bm, nj),
        in_specs=[
            pl.BlockSpec((bm, k), lambda i, j: (i, 0)),
            pl.BlockSpec((k, bn), lambda i, j: (0, j)),
            pl.BlockSpec((bm, bn), lambda i, j: (i, j)),
            pl.BlockSpec((1, D_MODEL), lambda i, j: (0, 0)),
            pl.BlockSpec((1, D_MODEL), lambda i, j: (0, 0)),
        ],
        out_specs=pl.BlockSpec((bm, D_MODEL), lambda i, j: (i, 0)),
        out_shape=jax.ShapeDtypeStruct((m, D_MODEL), F32),
        scratch_shapes=[pltpu.VMEM((nj, bm, bn), F32)],
        compiler_params=_cparams("parallel", "arbitrary"),
        name="out_ln",
    )(merged, w_out, x, ln_g.reshape(1, D_MODEL).astype(F32), ln_b.reshape(1, D_MODEL).astype(F32))


def _group_tail(z, row_off, o_ret, gl, o_x, x2d, w):
    o_s5 = _glu(gl, w["glu"], z, row_off, bm=1024, bn=512)
    merged = _merge(o_ret, o_s5, o_x, w["proj_a"], w["proj_b"], w["proj_c"], z, row_off, bm=512, bn=512)
    return _out_ln(merged, w["out"], x2d, w["ln_g"], w["ln_b"], bm=512, bn=512)


def kernel(x_prompt, x_sample, mem_prompt, state_ret, state_s5_re, state_s5_im, cache_mem_k, cache_mem_v, w_in, w_mem_kv, s5_a_re, s5_a_im, s5_log_step, s5_b_re, s5_b_im, s5_c_re, s5_c_im, s5_d, w_glu, w_proj_a, w_proj_b, w_proj_c, w_out, ln_g, ln_b):
    depth = w_in.shape[0]
    assert depth == 1
    l = 0
    n_p = BATCH * SEQ
    n_s = DEC_BATCH * DEC_SEQ
    xp2 = x_prompt.reshape(n_p, D_MODEL)
    xs2 = x_sample.reshape(n_s, D_MODEL)

    w = dict(glu=w_glu[l], proj_a=w_proj_a[l], proj_b=w_proj_b[l], proj_c=w_proj_c[l],
             out=w_out[l].astype(BF16), ln_g=ln_g[l], ln_b=ln_b[l])

    xb = _concat_cast(xp2, xs2, bm=512, out_dtype=BF16)
    z = _colmm([xb], [(0, w_in[l], 0)], [], _cast_epilogue, n_out=IN_WIDTH, bm=1024, bn=1024, out_dtype=BF16,
               name="in_proj")

    memb = mem_prompt.reshape(BATCH * MEM_LEN, D_MODEL).astype(BF16)
    mem_kv = lambda off, name: _colmm([memb], [(0, w_mem_kv[l], off)], [(z, 0, 0)], _cast_epilogue, n_out=X_WIDTH,
                                      bm=BATCH * MEM_LEN, bn=512, out_dtype=F32, name=name)
    mk, mv = mem_kv(0, "mem_k"), mem_kv(X_WIDTH, "mem_v")

    s5_win, s5_wout, s5_tabs = _s5_prepare(s5_a_re[l], s5_a_im[l], s5_log_step[l], s5_b_re[l], s5_b_im[l],
                                           s5_c_re[l], s5_c_im[l])
    d_skip = s5_d[l].reshape(1, S5_WIDTH).astype(F32)

    o_ret_p, ret_p = _retention_prompt(z, BATCH, SEQ)
    gl_p, hre_p, him_p = _s5_prompt(z, s5_win, s5_wout, d_skip, s5_tabs, BATCH, SEQ)
    o_x_p = _xattn(z, mk, mv, row_off=0, nrows=n_p, tq=512, q_per_seq=SEQ, seqs_per_step=1,
                   kv_col_blk=(0, 0), name="xattn_prompt")
    y_p = _group_tail(z, 0, o_ret_p, gl_p, o_x_p, xp2, w)

    o_ret_s, ret_s = _retention_sample(z, state_ret[l], n_p, DEC_BATCH, DEC_SEQ, PAST_LEN)
    gl_s, hre_s, him_s = _s5_sample(z, s5_win, s5_wout, d_skip, s5_tabs, state_s5_re[l], state_s5_im[l],
                                    n_p, DEC_BATCH, DEC_SEQ)
    ck = cache_mem_k[l].reshape(DEC_BATCH * MEM_LEN, X_WIDTH)
    cv = cache_mem_v[l].reshape(DEC_BATCH * MEM_LEN, X_WIDTH)
    o_x_s = _xattn(z, ck, cv, row_off=n_p, nrows=n_s, tq=2 * DEC_SEQ, q_per_seq=DEC_SEQ, seqs_per_step=2,
                   kv_col_blk=(0, 0), name="xattn_sample")
    y_s = _group_tail(z, n_p, o_ret_s, gl_s, o_x_s, xs2, w)

    return (y_p.reshape(BATCH, SEQ, D_MODEL), y_s.reshape(DEC_BATCH, DEC_SEQ, D_MODEL),
            ret_p[None], hre_p[None], him_p[None],
            mk.reshape(1, BATCH, MEM_LEN, X_HEADS, X_HD), mv.reshape(1, BATCH, MEM_LEN, X_HEADS, X_HD),
            ret_s[None], hre_s[None], him_s[None])
```

```python
import functools
import math

import jax
import jax.numpy as jnp
import numpy as np
from jax import lax
from jax.experimental import pallas as pl
from jax.experimental.pallas import tpu as pltpu

F32 = jnp.float32
BF16 = jnp.bfloat16

D_MODEL = 4096
BATCH = 4
SEQ = 2048
DEC_BATCH = 128
DEC_SEQ = 8
PAST_LEN = 16384

RET_HEADS = 16
RET_DK = 128
RET_DV = 256
RET_QK = RET_HEADS * RET_DK
RET_V = RET_HEADS * RET_DV
RET_CHUNK = 128
ROPE_BASE = 10000.0

S5_WIDTH = D_MODEL // 2
S5_GROUP = 16
S5_GROUPS = S5_WIDTH // S5_GROUP
S5_STATE = 64

X_HEADS = 4
X_WIDTH = D_MODEL // 2
X_HD = X_WIDTH // X_HEADS
MEM_LEN = 256

DN_ALPHA = 2.0 ** 0.25
LN_EPS = 1e-5
GN_EPS = 1e-5

IN_WIDTH = 2 * RET_QK + 2 * RET_V + 2 * S5_WIDTH + 2 * X_WIDTH + 3 * D_MODEL

Z_Q = 0
Z_K = Z_Q + RET_QK
Z_V = Z_K + RET_QK
Z_GRET = Z_V + RET_V
Z_U = Z_GRET + RET_V
Z_GS5 = Z_U + S5_WIDTH
Z_QX = Z_GS5 + S5_WIDTH
Z_GX = Z_QX + X_WIDTH
Z_MA = Z_GX + X_WIDTH
Z_MB = Z_MA + D_MODEL
Z_MC = Z_MB + D_MODEL

SUBLANES = 8
LANES = 128
VMEM_PHYSICAL_BYTES = 64 * 1024 * 1024
VMEM_LIMIT_BYTES = 56 * 1024 * 1024
VMEM_TEMP_BYTES = 12 * 1024 * 1024

S5_BLK_GROUPS = LANES // S5_GROUP
S5_BLK_STATE = S5_BLK_GROUPS * S5_STATE
S5_NBLK = S5_GROUPS // S5_BLK_GROUPS
SCAN_LEVELS = (1, 2, 4)
T_A = 0
T_POW = 2
T_LVL = T_POW + 2 * SUBLANES
T_CARRY = T_LVL + 2 * len(SCAN_LEVELS)
N_TABS = T_CARRY + 2


def _cparams(*sem, vmem_limit_bytes=VMEM_LIMIT_BYTES):
    return pltpu.CompilerParams(dimension_semantics=sem, vmem_limit_bytes=vmem_limit_bytes)


def _concat_cast_kernel(a_ref, b_ref, o_ref, *, na):
    i = pl.program_id(0)

    @pl.when(i < na)
    def _():
        o_ref[...] = a_ref[...].astype(o_ref.dtype)

    @pl.when(i >= na)
    def _():
        o_ref[...] = b_ref[...].astype(o_ref.dtype)


def _concat_cast(a, b, *, bm, out_dtype):
    k = a.shape[1]
    na, nb = a.shape[0] // bm, b.shape[0] // bm
    return pl.pallas_call(
        functools.partial(_concat_cast_kernel, na=na),
        grid=(na + nb,),
        in_specs=[
            pl.BlockSpec((bm, k), lambda i: (jnp.minimum(i, na - 1), 0)),
            pl.BlockSpec((bm, k), lambda i: (jnp.maximum(i - na, 0), 0), pipeline_mode=pl.Buffered(1)),
        ],
        out_specs=pl.BlockSpec((bm, k), lambda i: (i, 0)),
        out_shape=jax.ShapeDtypeStruct((a.shape[0] + b.shape[0], k), out_dtype),
        compiler_params=_cparams("arbitrary"),
        name="concat_cast",
    )(a, b)


def _colmm_kernel(*refs, n_lhs, terms, n_extra, epilogue, ncols, bn, guest):
    lhs = refs[:n_lhs]
    w_hbm = refs[n_lhs:n_lhs + len(terms)]
    extras = refs[n_lhs + len(terms):n_lhs + len(terms) + n_extra]
    rest = refs[n_lhs + len(terms) + n_extra:]
    n_gin, n_gout, guest_body = guest
    guest_in, o_ref, guest_out = rest[:n_gin], rest[n_gin], rest[n_gin + 1:n_gin + 1 + n_gout]
    stage, wb_scr, sem = rest[n_gin + 1 + n_gout:]
    j = pl.program_id(0)
    i = pl.program_id(1)
    if guest_body is not None:
        guest_body(*guest_in, *guest_out)

    def tile_copies(col):
        return [pltpu.make_async_copy(
            w_hbm[t].at[:, pl.ds(pl.multiple_of(off + col * bn, LANES), bn)],
            stage.at[pl.ds(row0, kt), :], sem.at[t]) for t, (_, row0, kt, off) in enumerate(terms)]

    @pl.when(i == 0)
    def _():
        @pl.when(j == 0)
        def _():
            for c in tile_copies(0):
                c.start()

        for c in tile_copies(j):
            c.wait()
        wb_scr[...] = stage[...].astype(BF16)

        @pl.when(j + 1 < ncols)
        def _():
            for c in tile_copies(j + 1):
                c.start()

    prods = [jnp.dot(lhs[li][...], wb_scr[row0:row0 + kt, :], preferred_element_type=F32)
             for li, row0, kt, _ in terms]
    o_ref[...] = epilogue(prods, [e[...] for e in extras]).astype(o_ref.dtype)


def _colmm(lhs, weights, extras, epilogue, *, n_out, bm, bn, out_dtype, name, guest=None, guest_bytes=0):
    m = lhs[0].shape[0]
    g_args, g_in_specs, g_out_specs, g_out_shape, g_body = guest or ((), [], [], [], None)
    terms, row0 = [], 0
    for li, w, off in weights:
        kt = w.shape[0]
        assert lhs[li].shape == (m, kt) and off % LANES == 0
        terms.append((li, row0, kt, off))
        row0 += kt
    ktot = row0
    lhs_bytes = sum(2 * bm * a.shape[1] * a.dtype.itemsize for a in lhs)
    extra_bytes = sum(2 * bm * bn * a.dtype.itemsize for a, _, _ in extras)
    need = (ktot * bn * 6 + lhs_bytes + extra_bytes + 2 * bm * bn * jnp.dtype(out_dtype).itemsize
            + len(terms) * bm * bn * 4 + guest_bytes)
    limit = min(max(VMEM_LIMIT_BYTES, need + VMEM_TEMP_BYTES), VMEM_PHYSICAL_BYTES - (2 << 20))
    assert need + (4 << 20) <= limit, (name, need, limit)
    kern = functools.partial(_colmm_kernel, n_lhs=len(lhs), terms=tuple(terms), n_extra=len(extras),
                             epilogue=epilogue, ncols=n_out // bn, bn=bn,
                             guest=(len(g_args), len(g_out_specs), g_body))
    extra_spec = lambda rb, cb: pl.BlockSpec((bm, bn), lambda j, i: (rb + i, cb + j))
    outs = pl.pallas_call(
        kern,
        grid=(n_out // bn, m // bm),
        in_specs=([pl.BlockSpec((bm, a.shape[1]), lambda j, i: (i, 0)) for a in lhs]
                  + [pl.BlockSpec(memory_space=pl.ANY) for _ in terms]
                  + [extra_spec(rb, cb) for _, rb, cb in extras] + list(g_in_specs)),
        out_specs=[pl.BlockSpec((bm, bn), lambda j, i: (i, j))] + list(g_out_specs),
        out_shape=[jax.ShapeDtypeStruct((m, n_out), out_dtype)] + list(g_out_shape),
        scratch_shapes=[pltpu.VMEM((ktot, bn), F32), pltpu.VMEM((ktot, bn), BF16),
                        pltpu.SemaphoreType.DMA((len(terms),))],
        compiler_params=_cparams("arbitrary", "arbitrary", vmem_limit_bytes=limit),
        name=name,
    )(*lhs, *[w for _, w, _ in weights], *[a for a, _, _ in extras], *g_args)
    return outs if guest else outs[0]


def _rotate(x, cos, sin_next, sin_prev):
    return x * cos + pltpu.roll(x, LANES - 1, 1) * sin_next + pltpu.roll(x, 1, 1) * sin_prev


def _ret_block(q, k, v, g, cos, sin_next, sin_prev, mask, xi, zeta, gc, states, slen):
    nseq = len(states)
    rows = q.shape[0]
    qr = _rotate(q, cos, sin_next, sin_prev)
    kr = _rotate(k, cos, sin_next, sin_prev) * (RET_DK ** -0.5)
    qb = qr.astype(BF16)
    kb = kr.astype(BF16)
    sc = lax.dot_general(qb, kb, (((1,), (1,)), ((), ())), preferred_element_type=F32) * mask
    o = jnp.dot(sc.astype(BF16), v, preferred_element_type=F32)
    qx = qr * xi
    kzt = (kr * zeta).T.astype(BF16)
    new_states = []
    if nseq == 1:
        s = states[0]
        o = o + jnp.dot(qx.astype(BF16), s.astype(BF16), preferred_element_type=F32)
        new_states.append(gc * s + jnp.dot(kzt, v, preferred_element_type=F32))
    else:
        pair = 2 * slen
        assert pair == 2 * SUBLANES and nseq % 2 == 0
        row_in_pair = lax.broadcasted_iota(jnp.int32, (pair, RET_DV), 0)
        row_seq = lax.broadcasted_iota(jnp.int32, (rows, RET_DV), 0) // slen
        parts = []
        for m in range(nseq // 2):
            qpair = qx[m * pair:(m + 1) * pair].astype(BF16)
            o0 = jnp.dot(qpair, states[2 * m].astype(BF16), preferred_element_type=F32)
            o1 = jnp.dot(qpair, states[2 * m + 1].astype(BF16), preferred_element_type=F32)
            parts.append(jnp.where(row_in_pair < slen, o0, o1))
        o = o + jnp.concatenate(parts, axis=0)
        vf = v.astype(F32)
        for n in range(nseq):
            vn = jnp.where(row_seq == n, vf, 0.0).astype(BF16)
            new_states.append(gc * states[n] + jnp.dot(kzt, vn, preferred_element_type=F32))
    mu = jnp.mean(o, axis=-1, keepdims=True)
    d = o - mu
    var = jnp.mean(d * d, axis=-1, keepdims=True)
    on = d * lax.rsqrt(var + GN_EPS)
    out = (on * (g * jax.nn.sigmoid(g))).astype(BF16)
    return out, new_states


def _ret_prompt_kernel(gc_ref, q_ref, k_ref, v_ref, g_ref, cos_ref, sn_ref, sp_ref, mask_ref, xi_ref, zeta_ref,
                       o_ref, sfin_ref, s_scr, *, chunk, nchunks, hb):
    head0 = pl.program_id(1) * hb
    s_scr[...] = jnp.zeros_like(s_scr)

    def body(c, carry):
        rows = pl.ds(pl.multiple_of(c * chunk, chunk), chunk)
        cos, sn, sp = cos_ref[rows, :], sn_ref[rows, :], sp_ref[rows, :]
        for hh in range(hb):
            qc = slice(hh * RET_DK, (hh + 1) * RET_DK)
            vc = slice(hh * RET_DV, (hh + 1) * RET_DV)
            out, (s_new,) = _ret_block(
                q_ref[rows, qc].astype(F32), k_ref[rows, qc].astype(F32), v_ref[rows, vc],
                g_ref[rows, vc].astype(F32), cos, sn, sp, mask_ref[hh], xi_ref[hh], zeta_ref[hh],
                gc_ref[head0 + hh], [s_scr[hh]], chunk)
            o_ref[rows, vc] = out
            s_scr[hh] = s_new
        return carry

    lax.fori_loop(0, nchunks, body, 0)
    sfin_ref[...] = s_scr[...]


def _ret_sample_body(head, gc_ref, q_ref, k_ref, v_ref, g_ref, cos_ref, sn_ref, sp_ref, mask_ref, xi_ref, zeta_ref,
                     s0_ref, o_ref, sfin_ref, *, slen, nseq):
    gc = gc_ref[head]
    states = [s0_ref[n] for n in range(nseq)]
    out, new_states = _ret_block(
        q_ref[...].astype(F32), k_ref[...].astype(F32), v_ref[...], g_ref[...].astype(F32),
        cos_ref[...], sn_ref[...], sp_ref[...], mask_ref[...], xi_ref[...], zeta_ref[...], gc, states, slen)
    o_ref[...] = out
    for n in range(nseq):
        sfin_ref[n] = new_states[n]


def _rope_tables(pos):
    half = RET_DK // 2
    inv = 1.0 / (ROPE_BASE ** (np.arange(half, dtype=np.float64) / half))
    ang = np.asarray(pos, np.float64)[:, None] * inv[None, :]
    cos = np.repeat(np.cos(ang), 2, axis=1)
    sin = np.repeat(np.sin(ang), 2, axis=1)
    even = (np.arange(RET_DK) % 2) == 0
    return (np.asarray(cos, np.float32), np.asarray(np.where(even, -sin, 0.0), np.float32),
            np.asarray(np.where(even, 0.0, sin), np.float32))


def _decay_tables(slen, nseq):
    lg = np.log1p(-np.exp2(-5.0 - np.arange(RET_HEADS, dtype=np.float64)))
    idx = np.arange(slen, dtype=np.float64)
    rel = idx[:, None] - idx[None, :]
    inner = np.where(rel[None] >= 0, np.exp(lg[:, None, None] * np.maximum(rel, 0.0)[None]), 0.0)
    xi = np.exp(lg[:, None] * (idx + 1.0))
    zeta = np.exp(lg[:, None] * (slen - 1.0 - idx))
    gc = np.exp(lg * slen)
    mask = np.einsum("nm,hij->hnimj", np.eye(nseq), inner).reshape(RET_HEADS, nseq * slen, nseq * slen)
    rows = nseq * slen
    xi_t = np.broadcast_to(np.tile(xi, (1, nseq))[:, :, None], (RET_HEADS, rows, RET_DK))
    zeta_t = np.broadcast_to(np.tile(zeta, (1, nseq))[:, :, None], (RET_HEADS, rows, RET_DK))
    f32 = lambda t: np.ascontiguousarray(t, dtype=np.float32)
    return f32(mask), f32(xi_t), f32(zeta_t), f32(gc)


def _retention_prompt(z, nbatch, seqlen):
    chunk = RET_CHUNK
    hb = 4
    cos, sn, sp = _rope_tables(np.arange(seqlen))
    mask, xi, zeta, gc = _decay_tables(chunk, 1)
    tab = lambda: pl.BlockSpec((seqlen, RET_DK), lambda b, h: (0, 0))
    head_tab = lambda w: pl.BlockSpec((hb, chunk, w), lambda b, h: (h, 0, 0))
    kern = functools.partial(_ret_prompt_kernel, chunk=chunk, nchunks=seqlen // chunk, hb=hb)
    qk_w, v_w = hb * RET_DK, hb * RET_DV
    return pl.pallas_call(
        kern,
        grid=(nbatch, RET_HEADS // hb),
        in_specs=[
            pl.BlockSpec(memory_space=pltpu.SMEM),
            pl.BlockSpec((seqlen, qk_w), lambda b, h: (b, Z_Q // qk_w + h)),
            pl.BlockSpec((seqlen, qk_w), lambda b, h: (b, Z_K // qk_w + h)),
            pl.BlockSpec((seqlen, v_w), lambda b, h: (b, Z_V // v_w + h)),
            pl.BlockSpec((seqlen, v_w), lambda b, h: (b, Z_GRET // v_w + h)),
            tab(), tab(), tab(),
            head_tab(chunk), head_tab(RET_DK), head_tab(RET_DK),
        ],
        out_specs=[
            pl.BlockSpec((seqlen, v_w), lambda b, h: (b, h)),
            pl.BlockSpec((None, hb, RET_DK, RET_DV), lambda b, h: (b, h, 0, 0)),
        ],
        out_shape=[
            jax.ShapeDtypeStruct((nbatch * seqlen, RET_V), BF16),
            jax.ShapeDtypeStruct((nbatch, RET_HEADS, RET_DK, RET_DV), F32),
        ],
        scratch_shapes=[pltpu.VMEM((hb, RET_DK, RET_DV), F32)],
        compiler_params=_cparams("parallel", "parallel"),
        name="retention_prompt",
    )(gc, z, z, z, z, cos, sn, sp, mask, xi, zeta)


def _retention_sample_operands(z, s0, row_off, nbatch, slen, pos0, nseq, block_of):
    rows = nseq * slen
    cos, sn, sp = (np.tile(t, (nseq, 1)) for t in _rope_tables(pos0 + np.arange(slen)))
    mask, xi, zeta, gc = _decay_tables(slen, nseq)
    rb = row_off // rows

    def at(fn):
        return lambda *ids: fn(*block_of(*ids))

    tab = lambda: pl.BlockSpec((rows, RET_DK), lambda *ids: (0, 0))
    head_tab = lambda w: pl.BlockSpec((None, rows, w), at(lambda i, h: (h, 0, 0)))
    state = lambda: pl.BlockSpec((nseq, None, RET_DK, RET_DV), at(lambda i, h: (i, h, 0, 0)))
    in_specs = [
        pl.BlockSpec(memory_space=pltpu.SMEM),
        pl.BlockSpec((rows, RET_DK), at(lambda i, h: (rb + i, Z_Q // RET_DK + h))),
        pl.BlockSpec((rows, RET_DK), at(lambda i, h: (rb + i, Z_K // RET_DK + h))),
        pl.BlockSpec((rows, RET_DV), at(lambda i, h: (rb + i, Z_V // RET_DV + h))),
        pl.BlockSpec((rows, RET_DV), at(lambda i, h: (rb + i, Z_GRET // RET_DV + h))),
        tab(), tab(), tab(),
        head_tab(rows), head_tab(RET_DK), head_tab(RET_DK),
        state(),
    ]
    out_specs = [pl.BlockSpec((rows, RET_DV), at(lambda i, h: (i, h))), state()]
    out_shape = [jax.ShapeDtypeStruct((nbatch * slen, RET_V), BF16),
                 jax.ShapeDtypeStruct((nbatch, RET_HEADS, RET_DK, RET_DV), F32)]
    return (gc, z, z, z, z, cos, sn, sp, mask, xi, zeta, s0), in_specs, out_specs, out_shape


def _s5_prep_kernel(are_ref, aim_ref, dt_ref, are_w_ref, aim_w_ref, dt_w_ref, bre_ref, bim_ref, cim_ref,
                    tabs_ref, bbre_ref, bbim_ref, ncim_ref):
    def abar(ar, ai, dt):
        mag = jnp.exp(dt * ar)
        return mag * jnp.cos(dt * ai), mag * jnp.sin(dt * ai)

    def powers(ar, ai):
        out = [(ar, ai)]
        for _ in range(1, SUBLANES):
            pr, pi = out[-1]
            out.append((pr * ar - pi * ai, pr * ai + pi * ar))
        return out

    pw = powers(*abar(are_ref[...], aim_ref[...], dt_ref[...]))
    qw = powers(*pw[-1])
    k_idx = lax.broadcasted_iota(jnp.int32, (SUBLANES, S5_BLK_STATE), 0)
    for j in range(S5_NBLK):
        every_row = lambda v: jnp.broadcast_to(v[j:j + 1, :], (SUBLANES, S5_BLK_STATE))
        for c in range(2):
            tabs_ref[j, T_A + c] = every_row(pw[0][c])
            for i in range(SUBLANES):
                tabs_ref[j, T_POW + 2 * i + c] = every_row(pw[i][c])
            for l, lvl in enumerate(SCAN_LEVELS):
                tabs_ref[j, T_LVL + 2 * l + c] = jnp.where(k_idx >= lvl, every_row(qw[lvl - 1][c]), 0.0)
            carry = every_row(qw[0][c])
            for k in range(1, SUBLANES):
                carry = jnp.where(k_idx == k, every_row(qw[k][c]), carry)
            tabs_ref[j, T_CARRY + c] = carry

    a_r, a_i = are_w_ref[...], aim_w_ref[...]
    w_r, w_i = abar(a_r, a_i, dt_w_ref[...])
    den = a_r * a_r + a_i * a_i
    x_re = w_r - 1.0
    f_re = (x_re * a_r + w_i * a_i) / den
    f_im = (w_i * a_r - x_re * a_i) / den
    br, bi = bre_ref[...], bim_ref[...]
    bbre_ref[...] = f_re * br - f_im * bi
    bbim_ref[...] = f_re * bi + f_im * br
    ncim_ref[...] = -cim_ref[...]


def _s5_prepare(a_re, a_im, log_step, b_re, b_im, c_re, c_im):
    g, n, p = S5_GROUPS, S5_STATE, S5_GROUP
    dt = jnp.broadcast_to(jnp.exp(log_step.astype(F32))[:, None], (g, n))
    wide = lambda t: jnp.repeat(t, p, axis=1)
    vm = lambda: pl.BlockSpec(memory_space=pltpu.VMEM)
    nb, bg = S5_NBLK, S5_BLK_GROUPS
    per_blk = lambda t: t.reshape(nb, S5_BLK_STATE)
    tabs, bbre, bbim, ncim = pl.pallas_call(
        _s5_prep_kernel,
        in_specs=[vm() for _ in range(9)],
        out_specs=[vm() for _ in range(4)],
        out_shape=[
            jax.ShapeDtypeStruct((nb, N_TABS, SUBLANES, S5_BLK_STATE), F32),
            jax.ShapeDtypeStruct((g, n * p), F32),
            jax.ShapeDtypeStruct((g, n * p), F32),
            jax.ShapeDtypeStruct((g, p * n), F32),
        ],
        name="s5_discretize",
    )(per_blk(a_re.astype(F32)), per_blk(a_im.astype(F32)), per_blk(dt),
      wide(a_re.astype(F32)), wide(a_im.astype(F32)), wide(dt),
      b_re.astype(F32).reshape(g, n * p), b_im.astype(F32).reshape(g, n * p), c_im.astype(F32).reshape(g, p * n))

    eye = jnp.eye(bg, dtype=bool)

    def in_blockdiag(t):
        t = t.reshape(nb, bg, n, p).transpose(0, 1, 3, 2)
        return jnp.where(eye[None, :, None, :, None], t[:, :, :, None, :], 0.0).reshape(nb, bg * p, bg * n)

    def out_blockdiag(t):
        t = t.reshape(nb, bg, p, n).transpose(0, 1, 3, 2)
        return jnp.where(eye[None, :, None, :, None], t[:, :, :, None, :], 0.0).reshape(nb, bg * n, bg * p)

    w_in = jnp.concatenate([in_blockdiag(bbre), in_blockdiag(bbim)], axis=-1).astype(BF16)
    w_out = jnp.concatenate([out_blockdiag(c_re.astype(F32).reshape(g, p * n)), out_blockdiag(ncim)],
                            axis=1).astype(BF16)
    return w_in, w_out, tabs


S5_SUB = SUBLANES * SUBLANES


def _segment_scan(er, ei, tabs_ref, cr, ci):
    for l, lvl in enumerate(SCAN_LEVELS):
        pr = tabs_ref[T_LVL + 2 * l]
        pi = tabs_ref[T_LVL + 2 * l + 1]
        sr = pltpu.roll(er, lvl, 0)
        si = pltpu.roll(ei, lvl, 0)
        er, ei = er + pr * sr - pi * si, ei + pr * si + pi * sr
    rr = tabs_ref[T_CARRY]
    ri = tabs_ref[T_CARRY + 1]
    crb = jnp.broadcast_to(cr, er.shape)
    cib = jnp.broadcast_to(ci, ei.shape)
    return er + rr * crb - ri * cib, ei + rr * cib + ri * crb


def _transpose_tiles(ref, nsub):
    return jnp.concatenate([ref[pl.ds(S5_SUB * m + i, SUBLANES, stride=SUBLANES), :]
                            for m in range(nsub) for i in range(SUBLANES)], axis=0)


def _s5_stream(u_ref, win_ref, wout_ref, d_ref, tabs_ref, uf_scr, bu_scr, h_scr, y_scr, init, long_seq):
    tc = u_ref.shape[0]
    nsub = tc // S5_SUB
    ns = S5_BLK_STATE
    uf = u_ref[...].astype(F32)
    uf_scr[...] = uf
    up = _transpose_tiles(uf_scr, nsub).astype(BF16)
    bu_scr[...] = jnp.dot(up, win_ref[...], preferred_element_type=F32)
    ar = tabs_ref[T_A]
    ai = tabs_ref[T_A + 1]
    finals = []
    for m in range(nsub):
        tile = lambda i: slice(S5_SUB * m + SUBLANES * i, S5_SUB * m + SUBLANES * (i + 1))
        if long_seq:
            hr = hi = None
        else:
            hr = init[0][SUBLANES * m:SUBLANES * (m + 1), :]
            hi = init[1][SUBLANES * m:SUBLANES * (m + 1), :]
        for i in range(SUBLANES):
            xr = bu_scr[tile(i), :ns]
            xi = bu_scr[tile(i), ns:]
            if hr is None:
                hr, hi = xr, xi
            else:
                hr, hi = ar * hr - ai * hi + xr, ar * hi + ai * hr + xi
            h_scr[tile(i), :ns] = hr
            h_scr[tile(i), ns:] = hi
        if long_seq:
            cr, ci = init
            fr, fi = _segment_scan(hr, hi, tabs_ref, cr, ci)
            first = lax.broadcasted_iota(jnp.int32, fr.shape, 0) == 0
            sr = jnp.where(first, jnp.broadcast_to(cr, fr.shape), pltpu.roll(fr, 1, 0))
            si = jnp.where(first, jnp.broadcast_to(ci, fi.shape), pltpu.roll(fi, 1, 0))
            init = (fr[SUBLANES - 1:, :], fi[SUBLANES - 1:, :])
            for i in range(SUBLANES):
                pr = tabs_ref[T_POW + 2 * i]
                pi = tabs_ref[T_POW + 2 * i + 1]
                h_scr[tile(i), :ns] = h_scr[tile(i), :ns] + pr * sr - pi * si
                h_scr[tile(i), ns:] = h_scr[tile(i), ns:] + pr * si + pi * sr
        else:
            finals.append((hr, hi))
    y_scr[...] = jnp.dot(h_scr[...].astype(BF16), wout_ref[...], preferred_element_type=F32)
    y = _transpose_tiles(y_scr, nsub) + d_ref[...] * uf
    return jax.nn.gelu(y).astype(BF16), (init if long_seq else finals)


N_RET_SAMPLE_IN = 12


def _s5_prompt_kernel(*refs, nb, nt, slen, nseq):
    u_refs = refs[:nb]
    win_ref, wout_ref, d_ref, tabs_ref = refs[nb:nb + 4]
    ret_in = refs[nb + 4:nb + 4 + N_RET_SAMPLE_IN]
    gl_ref, hre_ref, him_ref, o_ret_ref, sfin_ref, uf_scr, bu_scr, h_scr, y_scr, carry_scr = refs[nb + 4 + N_RET_SAMPLE_IN:]
    ns = S5_BLK_STATE

    head = (pl.program_id(0) * nt + pl.program_id(2)) % RET_HEADS
    _ret_sample_body(head, *ret_in, o_ret_ref, sfin_ref, slen=slen, nseq=nseq)

    @pl.when(pl.program_id(2) == 0)
    def _():
        carry_scr[...] = jnp.zeros_like(carry_scr)

    for s in range(nb):
        init = (carry_scr[s, 0:1, :ns], carry_scr[s, 0:1, ns:])
        out, (cr, ci) = _s5_stream(u_refs[s], win_ref, wout_ref, d_ref, tabs_ref, uf_scr.at[s], bu_scr.at[s],
                                   h_scr.at[s], y_scr.at[s], init, True)
        gl_ref[s] = out
        carry_scr[s, 0:1, :ns] = cr
        carry_scr[s, 0:1, ns:] = ci
        hre_ref[s] = cr
        him_ref[s] = ci


def _s5_sample_kernel(u_ref, win_ref, wout_ref, d_ref, tabs_ref, h0re_ref, h0im_ref, gl_ref, hre_ref, him_ref,
                      uf_scr, bu_scr, h_scr, y_scr):
    out, finals = _s5_stream(u_ref, win_ref, wout_ref, d_ref, tabs_ref, uf_scr, bu_scr, h_scr, y_scr,
                             (h0re_ref, h0im_ref), False)
    gl_ref[...] = out
    for m, (hr, hi) in enumerate(finals):
        hre_ref[SUBLANES * m:SUBLANES * (m + 1), :] = hr
        him_ref[SUBLANES * m:SUBLANES * (m + 1), :] = hi


def _s5_weight_specs():
    nst = S5_BLK_STATE
    return [
        pl.BlockSpec((None, LANES, 2 * nst), lambda j, *_: (j, 0, 0)),
        pl.BlockSpec((None, 2 * nst, LANES), lambda j, *_: (j, 0, 0)),
        pl.BlockSpec((1, LANES), lambda j, *_: (0, j)),
        pl.BlockSpec((None, N_TABS, SUBLANES, nst), lambda j, *_: (j, 0, 0, 0)),
    ]


def _s5_scratch(lead, tc):
    nst = S5_BLK_STATE
    return [pltpu.VMEM(lead + (tc, LANES), F32), pltpu.VMEM(lead + (tc, 2 * nst), F32),
            pltpu.VMEM(lead + (tc, 2 * nst), F32), pltpu.VMEM(lead + (tc, LANES), F32)]


def _s5_prompt_ret_sample(z, w_in, w_out, d_skip, tabs, nbatch, seqlen, s0, row_off, nbatch_s, slen, pos0):
    tc = 256
    nb = nbatch
    nt = seqlen // tc
    nst = S5_BLK_STATE
    nseq = 16
    steps = S5_NBLK * nt
    assert steps == (nbatch_s // nseq) * RET_HEADS
    ret_args, ret_in_specs, ret_out_specs, ret_out_shape = _retention_sample_operands(
        z, s0, row_off, nbatch_s, slen, pos0, nseq, lambda j, bb, t: divmod(j * nt + t, RET_HEADS))
    assert len(ret_args) == N_RET_SAMPLE_IN
    kern = functools.partial(_s5_prompt_kernel, nb=nb, nt=nt, slen=slen, nseq=nseq)
    u_spec = lambda s: pl.BlockSpec((tc, LANES), lambda j, bb, t: (s * nt + t, Z_U // LANES + j))
    state_spec = lambda: pl.BlockSpec((nb, 1, nst), lambda j, bb, t: (0, 0, j))
    gl, hre, him, o_ret, sfin = pl.pallas_call(
        kern,
        grid=(S5_NBLK, 1, nt),
        in_specs=[u_spec(s) for s in range(nb)] + _s5_weight_specs() + ret_in_specs,
        out_specs=([pl.BlockSpec((nb, tc, LANES), lambda j, bb, t: (0, t, j)), state_spec(), state_spec()]
                   + ret_out_specs),
        out_shape=[
            jax.ShapeDtypeStruct((nbatch, seqlen, S5_WIDTH), BF16),
            jax.ShapeDtypeStruct((nbatch, 1, S5_GROUPS * S5_STATE), F32),
            jax.ShapeDtypeStruct((nbatch, 1, S5_GROUPS * S5_STATE), F32),
        ] + ret_out_shape,
        scratch_shapes=_s5_scratch((nb,), tc) + [pltpu.VMEM((nb, SUBLANES, 2 * nst), F32)],
        compiler_params=_cparams("arbitrary", "arbitrary", "arbitrary"),
        name="s5_prompt_ret_sample",
    )(*([z] * nb), w_in, w_out, d_skip, tabs, *ret_args)
    return (gl.reshape(nbatch * seqlen, S5_WIDTH), hre.reshape(nbatch, S5_GROUPS, S5_STATE),
            him.reshape(nbatch, S5_GROUPS, S5_STATE), o_ret, sfin)


def _s5_sample(z, w_in, w_out, d_skip, tabs, h0_re, h0_im, row_off, nbatch, slen):
    assert slen == SUBLANES
    tc = 256
    nseq = tc // slen
    rb = row_off // tc
    nst = S5_BLK_STATE
    state_spec = lambda: pl.BlockSpec((nseq, nst), lambda j, i: (i, j))
    gl, hre, him = pl.pallas_call(
        _s5_sample_kernel,
        grid=(S5_NBLK, nbatch // nseq),
        in_specs=([pl.BlockSpec((tc, LANES), lambda j, i: (rb + i, Z_U // LANES + j))] + _s5_weight_specs()
                  + [state_spec(), state_spec()]),
        out_specs=[pl.BlockSpec((tc, LANES), lambda j, i: (i, j)), state_spec(), state_spec()],
        out_shape=[
            jax.ShapeDtypeStruct((nbatch * slen, S5_WIDTH), BF16),
            jax.ShapeDtypeStruct((nbatch, S5_GROUPS * S5_STATE), F32),
            jax.ShapeDtypeStruct((nbatch, S5_GROUPS * S5_STATE), F32),
        ],
        scratch_shapes=_s5_scratch((), tc),
        compiler_params=_cparams("parallel", "parallel"),
        name="s5_sample",
    )(z, w_in, w_out, d_skip, tabs, h0_re.reshape(nbatch, -1).astype(F32), h0_im.reshape(nbatch, -1).astype(F32))
    return gl, hre.reshape(nbatch, S5_GROUPS, S5_STATE), him.reshape(nbatch, S5_GROUPS, S5_STATE)


def _xattn_kernel(q_ref, g_ref, k_ref, v_ref, o_ref, *, q_per_seq, k_per_seq):
    tq = q_ref.shape[0]
    nk = k_ref.shape[0]
    kb = k_ref[...].astype(BF16)
    vb = v_ref[...].astype(BF16)
    if tq // q_per_seq > 1:
        same = (lax.broadcasted_iota(jnp.int32, (tq, nk), 0) // q_per_seq
                == lax.broadcasted_iota(jnp.int32, (tq, nk), 1) // k_per_seq)
    else:
        same = None
    for h in range(X_HEADS):
        cols = slice(h * X_HD, (h + 1) * X_HD)
        s = lax.dot_general(q_ref[:, cols], kb[:, cols], (((1,), (1,)), ((), ())),
                            preferred_element_type=F32) * (X_HD ** -0.5)
        if same is not None:
            s = jnp.where(same, s, -jnp.inf)
        e = jnp.exp(s - jnp.max(s, axis=-1, keepdims=True))
        oh = jnp.dot(e.astype(BF16), vb[:, cols], preferred_element_type=F32) / jnp.sum(e, axis=-1, keepdims=True)
        g = g_ref[:, cols].astype(F32)
        o_ref[:, cols] = (oh * (g * jax.nn.sigmoid(g))).astype(BF16)


def _xattn(z, mk, mv, *, row_off, nrows, tq, q_per_seq, seqs_per_step, kv_col_blk, name):
    nk = seqs_per_step * MEM_LEN
    steps_per_kv = (seqs_per_step * q_per_seq) // tq if tq < seqs_per_step * q_per_seq else 1
    rb = row_off // tq
    kern = functools.partial(_xattn_kernel, q_per_seq=q_per_seq, k_per_seq=MEM_LEN)
    kmap = lambda cb: (lambda i: (i // steps_per_kv, cb))
    return pl.pallas_call(
        kern,
        grid=(nrows // tq,),
        in_specs=[
            pl.BlockSpec((tq, X_WIDTH), lambda i: (rb + i, Z_QX // X_WIDTH)),
            pl.BlockSpec((tq, X_WIDTH), lambda i: (rb + i, Z_GX // X_WIDTH)),
            pl.BlockSpec((nk, X_WIDTH), kmap(kv_col_blk[0])),
            pl.BlockSpec((nk, X_WIDTH), kmap(kv_col_blk[1])),
        ],
        out_specs=pl.BlockSpec((tq, X_WIDTH), lambda i: (i, 0)),
        out_shape=jax.ShapeDtypeStruct((nrows, X_WIDTH), BF16),
        compiler_params=_cparams("parallel"),
        name=name,
    )(z, z, mk, mv)


def _xattn_sample_guest(z, ck, cv, *, row_off, nrows, q_per_seq, ncols_host, nrows_host):
    seqs = 2
    tq = seqs * q_per_seq
    nk = seqs * MEM_LEN
    assert ncols_host * nrows_host == nrows // tq
    rb = row_off // tq
    blk = lambda j, i: j * nrows_host + i
    in_specs = [
        pl.BlockSpec((tq, X_WIDTH), lambda j, i: (rb + blk(j, i), Z_QX // X_WIDTH)),
        pl.BlockSpec((tq, X_WIDTH), lambda j, i: (rb + blk(j, i), Z_GX // X_WIDTH)),
        pl.BlockSpec((nk, X_WIDTH), lambda j, i: (blk(j, i), 0)),
        pl.BlockSpec((nk, X_WIDTH), lambda j, i: (blk(j, i), 0)),
    ]
    out_specs = [pl.BlockSpec((tq, X_WIDTH), lambda j, i: (blk(j, i), 0))]
    out_shape = [jax.ShapeDtypeStruct((nrows, X_WIDTH), BF16)]
    body = functools.partial(_xattn_kernel, q_per_seq=q_per_seq, k_per_seq=MEM_LEN)
    vmem = 2 * 2 * nk * X_WIDTH * 4 + 2 * nk * X_WIDTH * 2
    return ((z, z, ck, cv), in_specs, out_specs, out_shape, body), vmem


def _cast_epilogue(prods, extras):
    return prods[0]


def _glu_epilogue(prods, extras):
    a, b = prods
    g = extras[0].astype(F32)
    return a * jax.nn.sigmoid(b) * (g * jax.nn.sigmoid(g))


def _glu(gl, w_glu, z, row_off, *, bm, bn, guest=None, guest_bytes=0):
    return _colmm([gl], [(0, w_glu, 0), (0, w_glu, S5_WIDTH)], [(z, row_off // bm, Z_GS5 // bn)], _glu_epilogue,
                  n_out=S5_WIDTH, bm=bm, bn=bn, out_dtype=BF16, name="glu", guest=guest, guest_bytes=guest_bytes)


def _merge_epilogue(prods, extras):
    out = jax.nn.sigmoid(extras[0].astype(F32)) * prods[0]
    for p, m in zip(prods[1:], extras[1:]):
        out = out + jax.nn.sigmoid(m.astype(F32)) * p
    return out


def _merge(o_a, o_b, o_c, w_a, w_b, w_c, z, row_off, *, bm, bn):
    rb = row_off // bm
    return _colmm([o_a, o_b, o_c], [(0, w_a, 0), (1, w_b, 0), (2, w_c, 0)],
                  [(z, rb, Z_MA // bn), (z, rb, Z_MB // bn), (z, rb, Z_MC // bn)], _merge_epilogue,
                  n_out=D_MODEL, bm=bm, bn=bn, out_dtype=BF16, name="merge")


def _out_ln_kernel(m_ref, w_ref, x_ref, g_ref, b_ref, o_ref, pre_scr, *, nj, bn):
    j = pl.program_id(1)
    pre_scr[j] = DN_ALPHA * x_ref[...] + jnp.dot(m_ref[...], w_ref[...], preferred_element_type=F32)

    @pl.when(j == nj - 1)
    def _():
        width = nj * bn
        tot = pre_scr[0].sum(axis=-1, keepdims=True)
        for t in range(1, nj):
            tot = tot + pre_scr[t].sum(axis=-1, keepdims=True)
        mu = tot / width
        sq = None
        for t in range(nj):
            d = pre_scr[t] - mu
            part = (d * d).sum(axis=-1, keepdims=True)
            sq = part if sq is None else sq + part
        rstd = lax.rsqrt(sq / width + LN_EPS)
        for t in range(nj):
            cols = slice(t * bn, (t + 1) * bn)
            o_ref[:, cols] = (pre_scr[t] - mu) * rstd * g_ref[:, cols] + b_ref[:, cols]


def _out_ln(merged, w_out, x, ln_g, ln_b, *, bm, bn):
    m, k = merged.shape
    nj = D_MODEL // bn
    kern = functools.partial(_out_ln_kernel, nj=nj, bn=bn)
    return pl.pallas_call(
        kern,
        grid=(m // bm, nj),
        in_specs=[
            pl.BlockSpec((bm, k), lambda i, j: (i, 0)),
            pl.BlockSpec((k, bn), lambda i, j: (0, j)),
            pl.BlockSpec((bm, bn), lambda i, j: (i, j)),
            pl.BlockSpec((1, D_MODEL), lambda i, j: (0, 0)),
            pl.BlockSpec((1, D_MODEL), lambda i, j: (0, 0)),
        ],
        out_specs=pl.BlockSpec((bm, D_MODEL), lambda i, j: (i, 0)),
        out_shape=jax.ShapeDtypeStruct((m, D_MODEL), F32),
        scratch_shapes=[pltpu.VMEM((nj, bm, bn), F32)],
        compiler_params=_cparams("parallel", "arbitrary"),
        name="out_ln",
    )(merged, w_out, x, ln_g.reshape(1, D_MODEL).astype(F32), ln_b.reshape(1, D_MODEL).astype(F32))


def _merge_out(z, row_off, o_ret, o_s5, o_x, x2d, w):
    merged = _merge(o_ret, o_s5, o_x, w["proj_a"], w["proj_b"], w["proj_c"], z, row_off, bm=512, bn=512)
    return _out_ln(merged, w["out"], x2d, w["ln_g"], w["ln_b"], bm=512, bn=512)


def kernel(x_prompt, x_sample, mem_prompt, state_ret, state_s5_re, state_s5_im, cache_mem_k, cache_mem_v, w_in, w_mem_kv, s5_a_re, s5_a_im, s5_log_step, s5_b_re, s5_b_im, s5_c_re, s5_c_im, s5_d, w_glu, w_proj_a, w_proj_b, w_proj_c, w_out, ln_g, ln_b):
    depth = w_in.shape[0]
    assert depth == 1
    l = 0
    n_p = BATCH * SEQ
    n_s = DEC_BATCH * DEC_SEQ
    xp2 = x_prompt.reshape(n_p, D_MODEL)
    xs2 = x_sample.reshape(n_s, D_MODEL)

    w = dict(glu=w_glu[l], proj_a=w_proj_a[l], proj_b=w_proj_b[l], proj_c=w_proj_c[l],
             out=w_out[l].astype(BF16), ln_g=ln_g[l], ln_b=ln_b[l])

    xb = _concat_cast(xp2, xs2, bm=512, out_dtype=BF16)
    z = _colmm([xb], [(0, w_in[l], 0)], [], _cast_epilogue, n_out=IN_WIDTH, bm=1024, bn=1024, out_dtype=BF16,
               name="in_proj")

    memb = mem_prompt.reshape(BATCH * MEM_LEN, D_MODEL).astype(BF16)
    mem_kv = lambda off, name: _colmm([memb], [(0, w_mem_kv[l], off)], [(z, 0, 0)], _cast_epilogue, n_out=X_WIDTH,
                                      bm=BATCH * MEM_LEN, bn=512, out_dtype=F32, name=name)
    mk, mv = mem_kv(0, "mem_k"), mem_kv(X_WIDTH, "mem_v")

    s5_win, s5_wout, s5_tabs = _s5_prepare(s5_a_re[l], s5_a_im[l], s5_log_step[l], s5_b_re[l], s5_b_im[l],
                                           s5_c_re[l], s5_c_im[l])
    d_skip = s5_d[l].reshape(1, S5_WIDTH).astype(F32)

    o_ret_p, ret_p = _retention_prompt(z, BATCH, SEQ)
    gl_p, hre_p, him_p, o_ret_s, ret_s = _s5_prompt_ret_sample(
        z, s5_win, s5_wout, d_skip, s5_tabs, BATCH, SEQ, state_ret[l], n_p, DEC_BATCH, DEC_SEQ, PAST_LEN)
    o_x_p = _xattn(z, mk, mv, row_off=0, nrows=n_p, tq=512, q_per_seq=SEQ, seqs_per_step=1,
                   kv_col_blk=(0, 0), name="xattn_prompt")
    ck = cache_mem_k[l].reshape(DEC_BATCH * MEM_LEN, X_WIDTH)
    cv = cache_mem_v[l].reshape(DEC_BATCH * MEM_LEN, X_WIDTH)
    glu_bm, glu_bn = 512, 512
    guest, guest_bytes = _xattn_sample_guest(z, ck, cv, row_off=n_p, nrows=n_s, q_per_seq=DEC_SEQ,
                                             ncols_host=S5_WIDTH // glu_bn, nrows_host=n_p // glu_bm)
    o_s5_p, o_x_s = _glu(gl_p, w["glu"], z, 0, bm=glu_bm, bn=glu_bn, guest=guest, guest_bytes=guest_bytes)
    y_p = _merge_out(z, 0, o_ret_p, o_s5_p, o_x_p, xp2, w)

    gl_s, hre_s, him_s = _s5_sample(z, s5_win, s5_wout, d_skip, s5_tabs, state_s5_re[l], state_s5_im[l],
                                    n_p, DEC_BATCH, DEC_SEQ)
    o_s5_s = _glu(gl_s, w["glu"], z, n_p, bm=1024, bn=512)
    y_s = _merge_out(z, n_p, o_ret_s, o_s5_s, o_x_s, xs2, w)

    return (y_p.reshape(BATCH, SEQ, D_MODEL), y_s.reshape(DEC_BATCH, DEC_SEQ, D_MODEL),
            ret_p[None], hre_p[None], him_p[None],
            mk.reshape(1, BATCH, MEM_LEN, X_HEADS, X_HD), mv.reshape(1, BATCH, MEM_LEN, X_HEADS, X_HD),
            ret_s[None], hre_s[None], him_s[None])
```

```python
import functools
import math

import jax
import jax.numpy as jnp
import numpy as np
from jax import lax
from jax.experimental import pallas as pl
from jax.experimental.pallas import tpu as pltpu

F32 = jnp.float32
BF16 = jnp.bfloat16

D_MODEL = 4096
BATCH = 4
SEQ = 2048
DEC_BATCH = 128
DEC_SEQ = 8
PAST_LEN = 16384

RET_HEADS = 16
RET_DK = 128
RET_DV = 256
RET_QK = RET_HEADS * RET_DK
RET_V = RET_HEADS * RET_DV
RET_CHUNK = 128
ROPE_BASE = 10000.0

S5_WIDTH = D_MODEL // 2
S5_GROUP = 16
S5_GROUPS = S5_WIDTH // S5_GROUP
S5_STATE = 64

X_HEADS = 4
X_WIDTH = D_MODEL // 2
X_HD = X_WIDTH // X_HEADS
MEM_LEN = 256

DN_ALPHA = 2.0 ** 0.25
LN_EPS = 1e-5
GN_EPS = 1e-5

IN_WIDTH = 2 * RET_QK + 2 * RET_V + 2 * S5_WIDTH + 2 * X_WIDTH + 3 * D_MODEL

Z_Q = 0
Z_K = Z_Q + RET_QK
Z_V = Z_K + RET_QK
Z_GRET = Z_V + RET_V
Z_U = Z_GRET + RET_V
Z_GS5 = Z_U + S5_WIDTH
Z_QX = Z_GS5 + S5_WIDTH
Z_GX = Z_QX + X_WIDTH
Z_MA = Z_GX + X_WIDTH
Z_MB = Z_MA + D_MODEL
Z_MC = Z_MB + D_MODEL

SUBLANES = 8
LANES = 128
VMEM_PHYSICAL_BYTES = 64 * 1024 * 1024
VMEM_LIMIT_BYTES = 56 * 1024 * 1024
VMEM_TEMP_BYTES = 12 * 1024 * 1024

S5_BLK_GROUPS = LANES // S5_GROUP
S5_BLK_STATE = S5_BLK_GROUPS * S5_STATE
S5_NBLK = S5_GROUPS // S5_BLK_GROUPS
SCAN_LEVELS = (1, 2, 4)
T_A = 0
T_POW = 2
T_LVL = T_POW + 2 * SUBLANES
T_CARRY = T_LVL + 2 * len(SCAN_LEVELS)
N_TABS = T_CARRY + 2


def _cparams(*sem, vmem_limit_bytes=VMEM_LIMIT_BYTES):
    return pltpu.CompilerParams(dimension_semantics=sem, vmem_limit_bytes=vmem_limit_bytes)


def _concat_cast_kernel(a_ref, b_ref, o_ref, *, na):
    i = pl.program_id(0)

    @pl.when(i < na)
    def _():
        o_ref[...] = a_ref[...].astype(o_ref.dtype)

    @pl.when(i >= na)
    def _():
        o_ref[...] = b_ref[...].astype(o_ref.dtype)


def _concat_cast(a, b, *, bm, out_dtype):
    k = a.shape[1]
    na, nb = a.shape[0] // bm, b.shape[0] // bm
    return pl.pallas_call(
        functools.partial(_concat_cast_kernel, na=na),
        grid=(na + nb,),
        in_specs=[
            pl.BlockSpec((bm, k), lambda i: (jnp.minimum(i, na - 1), 0)),
            pl.BlockSpec((bm, k), lambda i: (jnp.maximum(i - na, 0), 0), pipeline_mode=pl.Buffered(1)),
        ],
        out_specs=pl.BlockSpec((bm, k), lambda i: (i, 0)),
        out_shape=jax.ShapeDtypeStruct((a.shape[0] + b.shape[0], k), out_dtype),
        compiler_params=_cparams("arbitrary"),
        name="concat_cast",
    )(a, b)


def _colmm_kernel(*refs, n_lhs, terms, n_extra, epilogue, ncols, bn, guest):
    lhs = refs[:n_lhs]
    w_hbm = refs[n_lhs:n_lhs + len(terms)]
    extras = refs[n_lhs + len(terms):n_lhs + len(terms) + n_extra]
    rest = refs[n_lhs + len(terms) + n_extra:]
    n_gin, n_gout, guest_body = guest
    guest_in, o_ref, guest_out = rest[:n_gin], rest[n_gin], rest[n_gin + 1:n_gin + 1 + n_gout]
    stage, wb_scr, sem = rest[n_gin + 1 + n_gout:]
    j = pl.program_id(0)
    i = pl.program_id(1)
    if guest_body is not None:
        guest_body(*guest_in, *guest_out)

    def tile_copies(col):
        return [pltpu.make_async_copy(
            w_hbm[t].at[:, pl.ds(pl.multiple_of(off + col * bn, LANES), bn)],
            stage.at[pl.ds(row0, kt), :], sem.at[t]) for t, (_, row0, kt, off) in enumerate(terms)]

    @pl.when(i == 0)
    def _():
        @pl.when(j == 0)
        def _():
            for c in tile_copies(0):
                c.start()

        for c in tile_copies(j):
            c.wait()
        wb_scr[...] = stage[...].astype(BF16)

        @pl.when(j + 1 < ncols)
        def _():
            for c in tile_copies(j + 1):
                c.start()

    prods = [jnp.dot(lhs[li][...], wb_scr[row0:row0 + kt, :], preferred_element_type=F32)
             for li, row0, kt, _ in terms]
    o_ref[...] = epilogue(prods, [e[...] for e in extras]).astype(o_ref.dtype)


def _colmm(lhs, weights, extras, epilogue, *, n_out, bm, bn, out_dtype, name, guest=None, guest_bytes=0):
    m = lhs[0].shape[0]
    g_args, g_in_specs, g_out_specs, g_out_shape, g_body = guest or ((), [], [], [], None)
    terms, row0 = [], 0
    for li, w, off in weights:
        kt = w.shape[0]
        assert lhs[li].shape == (m, kt) and off % LANES == 0
        terms.append((li, row0, kt, off))
        row0 += kt
    ktot = row0
    lhs_bytes = sum(2 * bm * a.shape[1] * a.dtype.itemsize for a in lhs)
    extra_bytes = sum(2 * bm * bn * a.dtype.itemsize for a, _, _ in extras)
    need = (ktot * bn * 6 + lhs_bytes + extra_bytes + 2 * bm * bn * jnp.dtype(out_dtype).itemsize
            + len(terms) * bm * bn * 4 + guest_bytes)
    limit = min(max(VMEM_LIMIT_BYTES, need + VMEM_TEMP_BYTES), VMEM_PHYSICAL_BYTES - (2 << 20))
    assert need + (4 << 20) <= limit, (name, need, limit)
    kern = functools.partial(_colmm_kernel, n_lhs=len(lhs), terms=tuple(terms), n_extra=len(extras),
                             epilogue=epilogue, ncols=n_out // bn, bn=bn,
                             guest=(len(g_args), len(g_out_specs), g_body))
    extra_spec = lambda rb, cb: pl.BlockSpec((bm, bn), lambda j, i: (rb + i, cb + j))
    outs = pl.pallas_call(
        kern,
        grid=(n_out // bn, m // bm),
        in_specs=([pl.BlockSpec((bm, a.shape[1]), lambda j, i: (i, 0)) for a in lhs]
                  + [pl.BlockSpec(memory_space=pl.ANY) for _ in terms]
                  + [extra_spec(rb, cb) for _, rb, cb in extras] + list(g_in_specs)),
        out_specs=[pl.BlockSpec((bm, bn), lambda j, i: (i, j))] + list(g_out_specs),
        out_shape=[jax.ShapeDtypeStruct((m, n_out), out_dtype)] + list(g_out_shape),
        scratch_shapes=[pltpu.VMEM((ktot, bn), F32), pltpu.VMEM((ktot, bn), BF16),
                        pltpu.SemaphoreType.DMA((len(terms),))],
        compiler_params=_cparams("arbitrary", "arbitrary", vmem_limit_bytes=limit),
        name=name,
    )(*lhs, *[w for _, w, _ in weights], *[a for a, _, _ in extras], *g_args)
    return outs if guest else outs[0]


def _rotate(x, cos, sin_next, sin_prev):
    return x * cos + pltpu.roll(x, LANES - 1, 1) * sin_next + pltpu.roll(x, 1, 1) * sin_prev


def _ret_block(q, k, v, g, cos, sin_next, sin_prev, mask, xi, zeta, gc, states, slen):
    nseq = len(states)
    rows = q.shape[0]
    qr = _rotate(q, cos, sin_next, sin_prev)
    kr = _rotate(k, cos, sin_next, sin_prev) * (RET_DK ** -0.5)
    qb = qr.astype(BF16)
    kb = kr.astype(BF16)
    sc = lax.dot_general(qb, kb, (((1,), (1,)), ((), ())), preferred_element_type=F32) * mask
    o = jnp.dot(sc.astype(BF16), v, preferred_element_type=F32)
    qx = qr * xi
    kzt = (kr * zeta).T.astype(BF16)
    new_states = []
    if nseq == 1:
        s = states[0]
        o = o + jnp.dot(qx.astype(BF16), s.astype(BF16), preferred_element_type=F32)
        new_states.append(gc * s + jnp.dot(kzt, v, preferred_element_type=F32))
    else:
        pair = 2 * slen
        assert pair == 2 * SUBLANES and nseq % 2 == 0
        row_in_pair = lax.broadcasted_iota(jnp.int32, (pair, RET_DV), 0)
        row_seq = lax.broadcasted_iota(jnp.int32, (rows, RET_DV), 0) // slen
        parts = []
        for m in range(nseq // 2):
            qpair = qx[m * pair:(m + 1) * pair].astype(BF16)
            o0 = jnp.dot(qpair, states[2 * m].astype(BF16), preferred_element_type=F32)
            o1 = jnp.dot(qpair, states[2 * m + 1].astype(BF16), preferred_element_type=F32)
            parts.append(jnp.where(row_in_pair < slen, o0, o1))
        o = o + jnp.concatenate(parts, axis=0)
        vf = v.astype(F32)
        for n in range(nseq):
            vn = jnp.where(row_seq == n, vf, 0.0).astype(BF16)
            new_states.append(gc * states[n] + jnp.dot(kzt, vn, preferred_element_type=F32))
    mu = jnp.mean(o, axis=-1, keepdims=True)
    d = o - mu
    var = jnp.mean(d * d, axis=-1, keepdims=True)
    on = d * lax.rsqrt(var + GN_EPS)
    out = (on * (g * jax.nn.sigmoid(g))).astype(BF16)
    return out, new_states


def _ret_prompt_kernel(gc_ref, q_ref, k_ref, v_ref, g_ref, cos_ref, sn_ref, sp_ref, mask_ref, xi_ref, zeta_ref,
                       o_ref, sfin_ref, s_scr, *, chunk, nchunks, hb):
    head0 = pl.program_id(1) * hb
    s_scr[...] = jnp.zeros_like(s_scr)

    def body(c, carry):
        rows = pl.ds(pl.multiple_of(c * chunk, chunk), chunk)
        cos, sn, sp = cos_ref[rows, :], sn_ref[rows, :], sp_ref[rows, :]
        for hh in range(hb):
            qc = slice(hh * RET_DK, (hh + 1) * RET_DK)
            vc = slice(hh * RET_DV, (hh + 1) * RET_DV)
            out, (s_new,) = _ret_block(
                q_ref[rows, qc].astype(F32), k_ref[rows, qc].astype(F32), v_ref[rows, vc],
                g_ref[rows, vc].astype(F32), cos, sn, sp, mask_ref[hh], xi_ref[hh], zeta_ref[hh],
                gc_ref[head0 + hh], [s_scr[hh]], chunk)
            o_ref[rows, vc] = out
            s_scr[hh] = s_new
        return carry

    lax.fori_loop(0, nchunks, body, 0)
    sfin_ref[...] = s_scr[...]


def _ret_sample_body(head, gc_ref, q_ref, k_ref, v_ref, g_ref, cos_ref, sn_ref, sp_ref, mask_ref, xi_ref, zeta_ref,
                     s0_ref, o_ref, sfin_ref, *, slen, nseq):
    gc = gc_ref[head]
    states = [s0_ref[n] for n in range(nseq)]
    out, new_states = _ret_block(
        q_ref[...].astype(F32), k_ref[...].astype(F32), v_ref[...], g_ref[...].astype(F32),
        cos_ref[...], sn_ref[...], sp_ref[...], mask_ref[...], xi_ref[...], zeta_ref[...], gc, states, slen)
    o_ref[...] = out
    for n in range(nseq):
        sfin_ref[n] = new_states[n]


def _rope_tables(pos):
    half = RET_DK // 2
    inv = 1.0 / (ROPE_BASE ** (np.arange(half, dtype=np.float64) / half))
    ang = np.asarray(pos, np.float64)[:, None] * inv[None, :]
    cos = np.repeat(np.cos(ang), 2, axis=1)
    sin = np.repeat(np.sin(ang), 2, axis=1)
    even = (np.arange(RET_DK) % 2) == 0
    return (np.asarray(cos, np.float32), np.asarray(np.where(even, -sin, 0.0), np.float32),
            np.asarray(np.where(even, 0.0, sin), np.float32))


def _decay_tables(slen, nseq):
    lg = np.log1p(-np.exp2(-5.0 - np.arange(RET_HEADS, dtype=np.float64)))
    idx = np.arange(slen, dtype=np.float64)
    rel = idx[:, None] - idx[None, :]
    inner = np.where(rel[None] >= 0, np.exp(lg[:, None, None] * np.maximum(rel, 0.0)[None]), 0.0)
    xi = np.exp(lg[:, None] * (idx + 1.0))
    zeta = np.exp(lg[:, None] * (slen - 1.0 - idx))
    gc = np.exp(lg * slen)
    mask = np.einsum("nm,hij->hnimj", np.eye(nseq), inner).reshape(RET_HEADS, nseq * slen, nseq * slen)
    rows = nseq * slen
    xi_t = np.broadcast_to(np.tile(xi, (1, nseq))[:, :, None], (RET_HEADS, rows, RET_DK))
    zeta_t = np.broadcast_to(np.tile(zeta, (1, nseq))[:, :, None], (RET_HEADS, rows, RET_DK))
    f32 = lambda t: np.ascontiguousarray(t, dtype=np.float32)
    return f32(mask), f32(xi_t), f32(zeta_t), f32(gc)


def _retention_prompt(z, nbatch, seqlen):
    chunk = RET_CHUNK
    hb = 4
    cos, sn, sp = _rope_tables(np.arange(seqlen))
    mask, xi, zeta, gc = _decay_tables(chunk, 1)
    tab = lambda: pl.BlockSpec((seqlen, RET_DK), lambda b, h: (0, 0))
    head_tab = lambda w: pl.BlockSpec((hb, chunk, w), lambda b, h: (h, 0, 0))
    kern = functools.partial(_ret_prompt_kernel, chunk=chunk, nchunks=seqlen // chunk, hb=hb)
    qk_w, v_w = hb * RET_DK, hb * RET_DV
    return pl.pallas_call(
        kern,
        grid=(nbatch, RET_HEADS // hb),
        in_specs=[
            pl.BlockSpec(memory_space=pltpu.SMEM),
            pl.BlockSpec((seqlen, qk_w), lambda b, h: (b, Z_Q // qk_w + h)),
            pl.BlockSpec((seqlen, qk_w), lambda b, h: (b, Z_K // qk_w + h)),
            pl.BlockSpec((seqlen, v_w), lambda b, h: (b, Z_V // v_w + h)),
            pl.BlockSpec((seqlen, v_w), lambda b, h: (b, Z_GRET // v_w + h)),
            tab(), tab(), tab(),
            head_tab(chunk), head_tab(RET_DK), head_tab(RET_DK),
        ],
        out_specs=[
            pl.BlockSpec((seqlen, v_w), lambda b, h: (b, h)),
            pl.BlockSpec((None, hb, RET_DK, RET_DV), lambda b, h: (b, h, 0, 0)),
        ],
        out_shape=[
            jax.ShapeDtypeStruct((nbatch * seqlen, RET_V), BF16),
            jax.ShapeDtypeStruct((nbatch, RET_HEADS, RET_DK, RET_DV), F32),
        ],
        scratch_shapes=[pltpu.VMEM((hb, RET_DK, RET_DV), F32)],
        compiler_params=_cparams("parallel", "parallel"),
        name="retention_prompt",
    )(gc, z, z, z, z, cos, sn, sp, mask, xi, zeta)


def _retention_sample_operands(z, s0, row_off, nbatch, slen, pos0, nseq, block_of):
    rows = nseq * slen
    cos, sn, sp = (np.tile(t, (nseq, 1)) for t in _rope_tables(pos0 + np.arange(slen)))
    mask, xi, zeta, gc = _decay_tables(slen, nseq)
    rb = row_off // rows

    def at(fn):
        return lambda *ids: fn(*block_of(*ids))

    tab = lambda: pl.BlockSpec((rows, RET_DK), lambda *ids: (0, 0))
    head_tab = lambda w: pl.BlockSpec((None, rows, w), at(lambda i, h: (h, 0, 0)))
    state = lambda: pl.BlockSpec((nseq, None, RET_DK, RET_DV), at(lambda i, h: (i, h, 0, 0)))
    in_specs = [
        pl.BlockSpec(memory_space=pltpu.SMEM),
        pl.BlockSpec((rows, RET_DK), at(lambda i, h: (rb + i, Z_Q // RET_DK + h))),
        pl.BlockSpec((rows, RET_DK), at(lambda i, h: (rb + i, Z_K // RET_DK + h))),
        pl.BlockSpec((rows, RET_DV), at(lambda i, h: (rb + i, Z_V // RET_DV + h))),
        pl.BlockSpec((rows, RET_DV), at(lambda i, h: (rb + i, Z_GRET // RET_DV + h))),
        tab(), tab(), tab(),
        head_tab(rows), head_tab(RET_DK), head_tab(RET_DK),
        state(),
    ]
    out_specs = [pl.BlockSpec((rows, RET_DV), at(lambda i, h: (i, h))), state()]
    out_shape = [jax.ShapeDtypeStruct((nbatch * slen, RET_V), BF16),
                 jax.ShapeDtypeStruct((nbatch, RET_HEADS, RET_DK, RET_DV), F32)]
    return (gc, z, z, z, z, cos, sn, sp, mask, xi, zeta, s0), in_specs, out_specs, out_shape


def _s5_prep_kernel(are_ref, aim_ref, dt_ref, are_w_ref, aim_w_ref, dt_w_ref, bre_ref, bim_ref, cim_ref,
                    tabs_ref, bbre_ref, bbim_ref, ncim_ref):
    def abar(ar, ai, dt):
        mag = jnp.exp(dt * ar)
        return mag * jnp.cos(dt * ai), mag * jnp.sin(dt * ai)

    def powers(ar, ai):
        out = [(ar, ai)]
        for _ in range(1, SUBLANES):
            pr, pi = out[-1]
            out.append((pr * ar - pi * ai, pr * ai + pi * ar))
        return out

    pw = powers(*abar(are_ref[...], aim_ref[...], dt_ref[...]))
    qw = powers(*pw[-1])
    k_idx = lax.broadcasted_iota(jnp.int32, (SUBLANES, S5_BLK_STATE), 0)
    for j in range(S5_NBLK):
        every_row = lambda v: jnp.broadcast_to(v[j:j + 1, :], (SUBLANES, S5_BLK_STATE))
        for c in range(2):
            tabs_ref[j, T_A + c] = every_row(pw[0][c])
            for i in range(SUBLANES):
                tabs_ref[j, T_POW + 2 * i + c] = every_row(pw[i][c])
            for l, lvl in enumerate(SCAN_LEVELS):
                tabs_ref[j, T_LVL + 2 * l + c] = jnp.where(k_idx >= lvl, every_row(qw[lvl - 1][c]), 0.0)
            carry = every_row(qw[0][c])
            for k in range(1, SUBLANES):
                carry = jnp.where(k_idx == k, every_row(qw[k][c]), carry)
            tabs_ref[j, T_CARRY + c] = carry

    a_r, a_i = are_w_ref[...], aim_w_ref[...]
    w_r, w_i = abar(a_r, a_i, dt_w_ref[...])
    den = a_r * a_r + a_i * a_i
    x_re = w_r - 1.0
    f_re = (x_re * a_r + w_i * a_i) / den
    f_im = (w_i * a_r - x_re * a_i) / den
    br, bi = bre_ref[...], bim_ref[...]
    bbre_ref[...] = f_re * br - f_im * bi
    bbim_ref[...] = f_re * bi + f_im * br
    ncim_ref[...] = -cim_ref[...]


def _s5_prepare(a_re, a_im, log_step, b_re, b_im, c_re, c_im):
    g, n, p = S5_GROUPS, S5_STATE, S5_GROUP
    dt = jnp.broadcast_to(jnp.exp(log_step.astype(F32))[:, None], (g, n))
    wide = lambda t: jnp.repeat(t, p, axis=1)
    vm = lambda: pl.BlockSpec(memory_space=pltpu.VMEM)
    nb, bg = S5_NBLK, S5_BLK_GROUPS
    per_blk = lambda t: t.reshape(nb, S5_BLK_STATE)
    tabs, bbre, bbim, ncim = pl.pallas_call(
        _s5_prep_kernel,
        in_specs=[vm() for _ in range(9)],
        out_specs=[vm() for _ in range(4)],
        out_shape=[
            jax.ShapeDtypeStruct((nb, N_TABS, SUBLANES, S5_BLK_STATE), F32),
            jax.ShapeDtypeStruct((g, n * p), F32),
            jax.ShapeDtypeStruct((g, n * p), F32),
            jax.ShapeDtypeStruct((g, p * n), F32),
        ],
        name="s5_discretize",
    )(per_blk(a_re.astype(F32)), per_blk(a_im.astype(F32)), per_blk(dt),
      wide(a_re.astype(F32)), wide(a_im.astype(F32)), wide(dt),
      b_re.astype(F32).reshape(g, n * p), b_im.astype(F32).reshape(g, n * p), c_im.astype(F32).reshape(g, p * n))

    eye = jnp.eye(bg, dtype=bool)

    def in_blockdiag(t):
        t = t.reshape(nb, bg, n, p).transpose(0, 1, 3, 2)
        return jnp.where(eye[None, :, None, :, None], t[:, :, :, None, :], 0.0).reshape(nb, bg * p, bg * n)

    def out_blockdiag(t):
        t = t.reshape(nb, bg, p, n).transpose(0, 1, 3, 2)
        return jnp.where(eye[None, :, None, :, None], t[:, :, :, None, :], 0.0).reshape(nb, bg * n, bg * p)

    w_in = jnp.concatenate([in_blockdiag(bbre), in_blockdiag(bbim)], axis=-1).astype(BF16)
    w_out = jnp.concatenate([out_blockdiag(c_re.astype(F32).reshape(g, p * n)), out_blockdiag(ncim)],
                            axis=1).astype(BF16)
    return w_in, w_out, tabs


S5_SUB = SUBLANES * SUBLANES


def _segment_scan(er, ei, tabs_ref, cr, ci):
    for l, lvl in enumerate(SCAN_LEVELS):
        pr = tabs_ref[T_LVL + 2 * l]
        pi = tabs_ref[T_LVL + 2 * l + 1]
        sr = pltpu.roll(er, lvl, 0)
        si = pltpu.roll(ei, lvl, 0)
        er, ei = er + pr * sr - pi * si, ei + pr * si + pi * sr
    rr = tabs_ref[T_CARRY]
    ri = tabs_ref[T_CARRY + 1]
    crb = jnp.broadcast_to(cr, er.shape)
    cib = jnp.broadcast_to(ci, ei.shape)
    return er + rr * crb - ri * cib, ei + rr * cib + ri * crb


def _transpose_tiles(ref, nsub):
    return jnp.concatenate([ref[pl.ds(S5_SUB * m + i, SUBLANES, stride=SUBLANES), :]
                            for m in range(nsub) for i in range(SUBLANES)], axis=0)


def _s5_stream(u_ref, win_ref, wout_ref, d_ref, tabs_ref, uf_scr, bu_scr, h_scr, y_scr, init, long_seq):
    tc = u_ref.shape[0]
    nsub = tc // S5_SUB
    ns = S5_BLK_STATE
    uf = u_ref[...].astype(F32)
    uf_scr[...] = uf
    up = _transpose_tiles(uf_scr, nsub).astype(BF16)
    bu_scr[...] = jnp.dot(up, win_ref[...], preferred_element_type=F32)
    ar = tabs_ref[T_A]
    ai = tabs_ref[T_A + 1]
    finals = []
    for m in range(nsub):
        tile = lambda i: slice(S5_SUB * m + SUBLANES * i, S5_SUB * m + SUBLANES * (i + 1))
        if long_seq:
            hr = hi = None
        else:
            hr = init[0][SUBLANES * m:SUBLANES * (m + 1), :]
            hi = init[1][SUBLANES * m:SUBLANES * (m + 1), :]
        for i in range(SUBLANES):
            xr = bu_scr[tile(i), :ns]
            xi = bu_scr[tile(i), ns:]
            if hr is None:
                hr, hi = xr, xi
            else:
                hr, hi = ar * hr - ai * hi + xr, ar * hi + ai * hr + xi
            h_scr[tile(i), :ns] = hr
            h_scr[tile(i), ns:] = hi
        if long_seq:
            cr, ci = init
            fr, fi = _segment_scan(hr, hi, tabs_ref, cr, ci)
            first = lax.broadcasted_iota(jnp.int32, fr.shape, 0) == 0
            sr = jnp.where(first, jnp.broadcast_to(cr, fr.shape), pltpu.roll(fr, 1, 0))
            si = jnp.where(first, jnp.broadcast_to(ci, fi.shape), pltpu.roll(fi, 1, 0))
            init = (fr[SUBLANES - 1:, :], fi[SUBLANES - 1:, :])
            for i in range(SUBLANES):
                pr = tabs_ref[T_POW + 2 * i]
                pi = tabs_ref[T_POW + 2 * i + 1]
                h_scr[tile(i), :ns] = h_scr[tile(i), :ns] + pr * sr - pi * si
                h_scr[tile(i), ns:] = h_scr[tile(i), ns:] + pr * si + pi * sr
        else:
            finals.append((hr, hi))
    y_scr[...] = jnp.dot(h_scr[...].astype(BF16), wout_ref[...], preferred_element_type=F32)
    y = _transpose_tiles(y_scr, nsub) + d_ref[...] * uf
    return jax.nn.gelu(y).astype(BF16), (init if long_seq else finals)


N_RET_SAMPLE_IN = 12


def _s5_prompt_kernel(*refs, nb, nt, slen, nseq):
    u_refs = refs[:nb]
    win_ref, wout_ref, d_ref, tabs_ref = refs[nb:nb + 4]
    ret_in = refs[nb + 4:nb + 4 + N_RET_SAMPLE_IN]
    mem_ref, wkv_ref = refs[nb + 4 + N_RET_SAMPLE_IN:nb + 6 + N_RET_SAMPLE_IN]
    (gl_ref, hre_ref, him_ref, o_ret_ref, sfin_ref, kv_ref,
     uf_scr, bu_scr, h_scr, y_scr, carry_scr) = refs[nb + 6 + N_RET_SAMPLE_IN:]
    ns = S5_BLK_STATE

    head = (pl.program_id(0) * nt + pl.program_id(2)) % RET_HEADS
    _ret_sample_body(head, *ret_in, o_ret_ref, sfin_ref, slen=slen, nseq=nseq)

    @pl.when(pl.program_id(2) == 0)
    def _():
        kv_ref[...] = jnp.dot(mem_ref[...], wkv_ref[...].astype(BF16), preferred_element_type=F32)

    @pl.when(pl.program_id(2) == 0)
    def _():
        carry_scr[...] = jnp.zeros_like(carry_scr)

    for s in range(nb):
        init = (carry_scr[s, 0:1, :ns], carry_scr[s, 0:1, ns:])
        out, (cr, ci) = _s5_stream(u_refs[s], win_ref, wout_ref, d_ref, tabs_ref, uf_scr.at[s], bu_scr.at[s],
                                   h_scr.at[s], y_scr.at[s], init, True)
        gl_ref[s] = out
        carry_scr[s, 0:1, :ns] = cr
        carry_scr[s, 0:1, ns:] = ci
        hre_ref[s] = cr
        him_ref[s] = ci


def _s5_sample_kernel(u_ref, win_ref, wout_ref, d_ref, tabs_ref, h0re_ref, h0im_ref, gl_ref, hre_ref, him_ref,
                      uf_scr, bu_scr, h_scr, y_scr):
    out, finals = _s5_stream(u_ref, win_ref, wout_ref, d_ref, tabs_ref, uf_scr, bu_scr, h_scr, y_scr,
                             (h0re_ref, h0im_ref), False)
    gl_ref[...] = out
    for m, (hr, hi) in enumerate(finals):
        hre_ref[SUBLANES * m:SUBLANES * (m + 1), :] = hr
        him_ref[SUBLANES * m:SUBLANES * (m + 1), :] = hi


def _s5_weight_specs():
    nst = S5_BLK_STATE
    return [
        pl.BlockSpec((None, LANES, 2 * nst), lambda j, *_: (j, 0, 0)),
        pl.BlockSpec((None, 2 * nst, LANES), lambda j, *_: (j, 0, 0)),
        pl.BlockSpec((1, LANES), lambda j, *_: (0, j)),
        pl.BlockSpec((None, N_TABS, SUBLANES, nst), lambda j, *_: (j, 0, 0, 0)),
    ]


def _s5_scratch(lead, tc):
    nst = S5_BLK_STATE
    return [pltpu.VMEM(lead + (tc, LANES), F32), pltpu.VMEM(lead + (tc, 2 * nst), F32),
            pltpu.VMEM(lead + (tc, 2 * nst), F32), pltpu.VMEM(lead + (tc, LANES), F32)]


def _s5_prompt_ret_sample(z, w_in, w_out, d_skip, tabs, nbatch, seqlen, s0, row_off, nbatch_s, slen, pos0,
                          memb, w_kv):
    tc = 256
    nb = nbatch
    nt = seqlen // tc
    nst = S5_BLK_STATE
    nseq = 16
    steps = S5_NBLK * nt
    assert steps == (nbatch_s // nseq) * RET_HEADS
    ret_args, ret_in_specs, ret_out_specs, ret_out_shape = _retention_sample_operands(
        z, s0, row_off, nbatch_s, slen, pos0, nseq, lambda j, bb, t: divmod(j * nt + t, RET_HEADS))
    assert len(ret_args) == N_RET_SAMPLE_IN
    kern = functools.partial(_s5_prompt_kernel, nb=nb, nt=nt, slen=slen, nseq=nseq)
    u_spec = lambda s: pl.BlockSpec((tc, LANES), lambda j, bb, t: (s * nt + t, Z_U // LANES + j))
    state_spec = lambda: pl.BlockSpec((nb, 1, nst), lambda j, bb, t: (0, 0, j))
    mrows, kdim = memb.shape
    kv_bn = w_kv.shape[1] // S5_NBLK
    gl, hre, him, o_ret, sfin, kv = pl.pallas_call(
        kern,
        grid=(S5_NBLK, 1, nt),
        in_specs=([u_spec(s) for s in range(nb)] + _s5_weight_specs() + ret_in_specs + [
            pl.BlockSpec((mrows, kdim), lambda j, bb, t: (0, 0), pipeline_mode=pl.Buffered(1)),
            pl.BlockSpec((kdim, kv_bn), lambda j, bb, t: (0, j)),
        ]),
        out_specs=([pl.BlockSpec((nb, tc, LANES), lambda j, bb, t: (0, t, j)), state_spec(), state_spec()]
                   + ret_out_specs + [pl.BlockSpec((mrows, kv_bn), lambda j, bb, t: (0, j))]),
        out_shape=[
            jax.ShapeDtypeStruct((nbatch, seqlen, S5_WIDTH), BF16),
            jax.ShapeDtypeStruct((nbatch, 1, S5_GROUPS * S5_STATE), F32),
            jax.ShapeDtypeStruct((nbatch, 1, S5_GROUPS * S5_STATE), F32),
        ] + ret_out_shape + [jax.ShapeDtypeStruct((mrows, w_kv.shape[1]), F32)],
        scratch_shapes=_s5_scratch((nb,), tc) + [pltpu.VMEM((nb, SUBLANES, 2 * nst), F32)],
        compiler_params=_cparams("arbitrary", "arbitrary", "arbitrary"),
        name="s5_prompt_ret_sample",
    )(*([z] * nb), w_in, w_out, d_skip, tabs, *ret_args, memb, w_kv)
    return (gl.reshape(nbatch * seqlen, S5_WIDTH), hre.reshape(nbatch, S5_GROUPS, S5_STATE),
            him.reshape(nbatch, S5_GROUPS, S5_STATE), o_ret, sfin, kv)


def _s5_sample(z, w_in, w_out, d_skip, tabs, h0_re, h0_im, row_off, nbatch, slen):
    assert slen == SUBLANES
    tc = 256
    nseq = tc // slen
    rb = row_off // tc
    nst = S5_BLK_STATE
    state_spec = lambda: pl.BlockSpec((nseq, nst), lambda j, i: (i, j))
    gl, hre, him = pl.pallas_call(
        _s5_sample_kernel,
        grid=(S5_NBLK, nbatch // nseq),
        in_specs=([pl.BlockSpec((tc, LANES), lambda j, i: (rb + i, Z_U // LANES + j))] + _s5_weight_specs()
                  + [state_spec(), state_spec()]),
        out_specs=[pl.BlockSpec((tc, LANES), lambda j, i: (i, j)), state_spec(), state_spec()],
        out_shape=[
            jax.ShapeDtypeStruct((nbatch * slen, S5_WIDTH), BF16),
            jax.ShapeDtypeStruct((nbatch, S5_GROUPS * S5_STATE), F32),
            jax.ShapeDtypeStruct((nbatch, S5_GROUPS * S5_STATE), F32),
        ],
        scratch_shapes=_s5_scratch((), tc),
        compiler_params=_cparams("parallel", "parallel"),
        name="s5_sample",
    )(z, w_in, w_out, d_skip, tabs, h0_re.reshape(nbatch, -1).astype(F32), h0_im.reshape(nbatch, -1).astype(F32))
    return gl, hre.reshape(nbatch, S5_GROUPS, S5_STATE), him.reshape(nbatch, S5_GROUPS, S5_STATE)


def _xattn_kernel(q_ref, g_ref, k_ref, v_ref, o_ref, *, q_per_seq, k_per_seq):
    tq = q_ref.shape[0]
    nk = k_ref.shape[0]
    kb = k_ref[...].astype(BF16)
    vb = v_ref[...].astype(BF16)
    if tq // q_per_seq > 1:
        same = (lax.broadcasted_iota(jnp.int32, (tq, nk), 0) // q_per_seq
                == lax.broadcasted_iota(jnp.int32, (tq, nk), 1) // k_per_seq)
    else:
        same = None
    for h in range(X_HEADS):
        cols = slice(h * X_HD, (h + 1) * X_HD)
        s = lax.dot_general(q_ref[:, cols], kb[:, cols], (((1,), (1,)), ((), ())),
                            preferred_element_type=F32) * (X_HD ** -0.5)
        if same is not None:
            s = jnp.where(same, s, -jnp.inf)
        e = jnp.exp(s - jnp.max(s, axis=-1, keepdims=True))
        oh = jnp.dot(e.astype(BF16), vb[:, cols], preferred_element_type=F32) / jnp.sum(e, axis=-1, keepdims=True)
        g = g_ref[:, cols].astype(F32)
        o_ref[:, cols] = (oh * (g * jax.nn.sigmoid(g))).astype(BF16)


def _xattn(z, mk, mv, *, row_off, nrows, tq, q_per_seq, seqs_per_step, kv_col_blk, name):
    nk = seqs_per_step * MEM_LEN
    steps_per_kv = (seqs_per_step * q_per_seq) // tq if tq < seqs_per_step * q_per_seq else 1
    rb = row_off // tq
    kern = functools.partial(_xattn_kernel, q_per_seq=q_per_seq, k_per_seq=MEM_LEN)
    kmap = lambda cb: (lambda i: (i // steps_per_kv, cb))
    return pl.pallas_call(
        kern,
        grid=(nrows // tq,),
        in_specs=[
            pl.BlockSpec((tq, X_WIDTH), lambda i: (rb + i, Z_QX // X_WIDTH)),
            pl.BlockSpec((tq, X_WIDTH), lambda i: (rb + i, Z_GX // X_WIDTH)),
            pl.BlockSpec((nk, X_WIDTH), kmap(kv_col_blk[0])),
            pl.BlockSpec((nk, X_WIDTH), kmap(kv_col_blk[1])),
        ],
        out_specs=pl.BlockSpec((tq, X_WIDTH), lambda i: (i, 0)),
        out_shape=jax.ShapeDtypeStruct((nrows, X_WIDTH), BF16),
        compiler_params=_cparams("parallel"),
        name=name,
    )(z, z, mk, mv)


def _xattn_sample_guest(z, ck, cv, *, row_off, nrows, q_per_seq, ncols_host, nrows_host):
    seqs = 2
    tq = seqs * q_per_seq
    nk = seqs * MEM_LEN
    assert ncols_host * nrows_host == nrows // tq
    rb = row_off // tq
    blk = lambda j, i: j * nrows_host + i
    in_specs = [
        pl.BlockSpec((tq, X_WIDTH), lambda j, i: (rb + blk(j, i), Z_QX // X_WIDTH)),
        pl.BlockSpec((tq, X_WIDTH), lambda j, i: (rb + blk(j, i), Z_GX // X_WIDTH)),
        pl.BlockSpec((nk, X_WIDTH), lambda j, i: (blk(j, i), 0)),
        pl.BlockSpec((nk, X_WIDTH), lambda j, i: (blk(j, i), 0)),
    ]
    out_specs = [pl.BlockSpec((tq, X_WIDTH), lambda j, i: (blk(j, i), 0))]
    out_shape = [jax.ShapeDtypeStruct((nrows, X_WIDTH), BF16)]
    body = functools.partial(_xattn_kernel, q_per_seq=q_per_seq, k_per_seq=MEM_LEN)
    vmem = 2 * 2 * nk * X_WIDTH * 4 + 2 * nk * X_WIDTH * 2
    return ((z, z, ck, cv), in_specs, out_specs, out_shape, body), vmem


def _cast_epilogue(prods, extras):
    return prods[0]


def _glu_epilogue(prods, extras):
    a, b = prods
    g = extras[0].astype(F32)
    return a * jax.nn.sigmoid(b) * (g * jax.nn.sigmoid(g))


def _glu(gl, w_glu, z, row_off, *, bm, bn, guest=None, guest_bytes=0):
    return _colmm([gl], [(0, w_glu, 0), (0, w_glu, S5_WIDTH)], [(z, row_off // bm, Z_GS5 // bn)], _glu_epilogue,
                  n_out=S5_WIDTH, bm=bm, bn=bn, out_dtype=BF16, name="glu", guest=guest, guest_bytes=guest_bytes)


def _merge_epilogue(prods, extras):
    out = jax.nn.sigmoid(extras[0].astype(F32)) * prods[0]
    for p, m in zip(prods[1:], extras[1:]):
        out = out + jax.nn.sigmoid(m.astype(F32)) * p
    return out


def _merge(o_a, o_b, o_c, w_a, w_b, w_c, z, row_off, *, bm, bn):
    rb = row_off // bm
    return _colmm([o_a, o_b, o_c], [(0, w_a, 0), (1, w_b, 0), (2, w_c, 0)],
                  [(z, rb, Z_MA // bn), (z, rb, Z_MB // bn), (z, rb, Z_MC // bn)], _merge_epilogue,
                  n_out=D_MODEL, bm=bm, bn=bn, out_dtype=BF16, name="merge")


def _residual_epilogue(prods, extras):
    return DN_ALPHA * extras[0] + prods[0]


def _layer_norm_kernel(p_ref, g_ref, b_ref, o_ref):
    x = p_ref[...]
    mu = jnp.mean(x, axis=-1, keepdims=True)
    d = x - mu
    var = jnp.mean(d * d, axis=-1, keepdims=True)
    o_ref[...] = d * lax.rsqrt(var + LN_EPS) * g_ref[...] + b_ref[...]


def _out_ln(merged, w_out, x, ln_g, ln_b, *, bm, bn, bm_ln):
    m, d = x.shape
    pre = _colmm([merged], [(0, w_out, 0)], [(x, 0, 0)], _residual_epilogue, n_out=d, bm=bm, bn=bn,
                 out_dtype=F32, name="out_proj")
    row = lambda: pl.BlockSpec((1, d), lambda i: (0, 0))
    return pl.pallas_call(
        _layer_norm_kernel,
        grid=(m // bm_ln,),
        in_specs=[pl.BlockSpec((bm_ln, d), lambda i: (i, 0)), row(), row()],
        out_specs=pl.BlockSpec((bm_ln, d), lambda i: (i, 0)),
        out_shape=jax.ShapeDtypeStruct((m, d), F32),
        compiler_params=_cparams("parallel"),
        name="layer_norm",
    )(pre, ln_g.reshape(1, d).astype(F32), ln_b.reshape(1, d).astype(F32))


def _merge_out(z, row_off, o_ret, o_s5, o_x, x2d, w):
    merged = _merge(o_ret, o_s5, o_x, w["proj_a"], w["proj_b"], w["proj_c"], z, row_off, bm=512, bn=512)
    return _out_ln(merged, w["out"], x2d, w["ln_g"], w["ln_b"], bm=512, bn=1024, bm_ln=256)


def kernel(x_prompt, x_sample, mem_prompt, state_ret, state_s5_re, state_s5_im, cache_mem_k, cache_mem_v, w_in, w_mem_kv, s5_a_re, s5_a_im, s5_log_step, s5_b_re, s5_b_im, s5_c_re, s5_c_im, s5_d, w_glu, w_proj_a, w_proj_b, w_proj_c, w_out, ln_g, ln_b):
    depth = w_in.shape[0]
    assert depth == 1
    l = 0
    n_p = BATCH * SEQ
    n_s = DEC_BATCH * DEC_SEQ
    xp2 = x_prompt.reshape(n_p, D_MODEL)
    xs2 = x_sample.reshape(n_s, D_MODEL)

    w = dict(glu=w_glu[l], proj_a=w_proj_a[l], proj_b=w_proj_b[l], proj_c=w_proj_c[l],
             out=w_out[l], ln_g=ln_g[l], ln_b=ln_b[l])

    xb = _concat_cast(xp2, xs2, bm=512, out_dtype=BF16)
    z = _colmm([xb], [(0, w_in[l], 0)], [], _cast_epilogue, n_out=IN_WIDTH, bm=1024, bn=1024, out_dtype=BF16,
               name="in_proj")

    memb = mem_prompt.reshape(BATCH * MEM_LEN, D_MODEL).astype(BF16)
    s5_win, s5_wout, s5_tabs = _s5_prepare(s5_a_re[l], s5_a_im[l], s5_log_step[l], s5_b_re[l], s5_b_im[l],
                                           s5_c_re[l], s5_c_im[l])
    d_skip = s5_d[l].reshape(1, S5_WIDTH).astype(F32)

    o_ret_p, ret_p = _retention_prompt(z, BATCH, SEQ)
    gl_p, hre_p, him_p, o_ret_s, ret_s, kv = _s5_prompt_ret_sample(
        z, s5_win, s5_wout, d_skip, s5_tabs, BATCH, SEQ, state_ret[l], n_p, DEC_BATCH, DEC_SEQ, PAST_LEN,
        memb, w_mem_kv[l])
    mk, mv = kv[:, :X_WIDTH], kv[:, X_WIDTH:]
    o_x_p = _xattn(z, kv, kv, row_off=0, nrows=n_p, tq=512, q_per_seq=SEQ, seqs_per_step=1,
                   kv_col_blk=(0, 1), name="xattn_prompt")
    ck = cache_mem_k[l].reshape(DEC_BATCH * MEM_LEN, X_WIDTH)
    cv = cache_mem_v[l].reshape(DEC_BATCH * MEM_LEN, X_WIDTH)
    glu_bm, glu_bn = 512, 512
    guest, guest_bytes = _xattn_sample_guest(z, ck, cv, row_off=n_p, nrows=n_s, q_per_seq=DEC_SEQ,
                                             ncols_host=S5_WIDTH // glu_bn, nrows_host=n_p // glu_bm)
    o_s5_p, o_x_s = _glu(gl_p, w["glu"], z, 0, bm=glu_bm, bn=glu_bn, guest=guest, guest_bytes=guest_bytes)
    y_p = _merge_out(z, 0, o_ret_p, o_s5_p, o_x_p, xp2, w)

    gl_s, hre_s, him_s = _s5_sample(z, s5_win, s5_wout, d_skip, s5_tabs, state_s5_re[l], state_s5_im[l],
                                    n_p, DEC_BATCH, DEC_SEQ)
    o_s5_s = _glu(gl_s, w["glu"], z, n_p, bm=1024, bn=512)
    y_s = _merge_out(z, n_p, o_ret_s, o_s5_s, o_x_s, xs2, w)

    return (y_p.reshape(BATCH, SEQ, D_MODEL), y_s.reshape(DEC_BATCH, DEC_SEQ, D_MODEL),
            ret_p[None], hre_p[None], him_p[None],
            mk.reshape(1, BATCH, MEM_LEN, X_HEADS, X_HD), mv.reshape(1, BATCH, MEM_LEN, X_HEADS, X_HD),
            ret_s[None], hre_s[None], him_s[None])
```

```python
import functools
import math

import jax
import jax.numpy as jnp
import numpy as np
from jax import lax
from jax.experimental import pallas as pl
from jax.experimental.pallas import tpu as pltpu

F32 = jnp.float32
BF16 = jnp.bfloat16

D_MODEL = 4096
BATCH = 4
SEQ = 2048
DEC_BATCH = 128
DEC_SEQ = 8
PAST_LEN = 16384

RET_HEADS = 16
RET_DK = 128
RET_DV = 256
RET_QK = RET_HEADS * RET_DK
RET_V = RET_HEADS * RET_DV
RET_CHUNK = 128
ROPE_BASE = 10000.0

S5_WIDTH = D_MODEL // 2
S5_GROUP = 16
S5_GROUPS = S5_WIDTH // S5_GROUP
S5_STATE = 64

X_HEADS = 4
X_WIDTH = D_MODEL // 2
X_HD = X_WIDTH // X_HEADS
MEM_LEN = 256

DN_ALPHA = 2.0 ** 0.25
LN_EPS = 1e-5
GN_EPS = 1e-5

IN_WIDTH = 2 * RET_QK + 2 * RET_V + 2 * S5_WIDTH + 2 * X_WIDTH + 3 * D_MODEL

Z_Q = 0
Z_K = Z_Q + RET_QK
Z_V = Z_K + RET_QK
Z_GRET = Z_V + RET_V
Z_U = Z_GRET + RET_V
Z_GS5 = Z_U + S5_WIDTH
Z_QX = Z_GS5 + S5_WIDTH
Z_GX = Z_QX + X_WIDTH
Z_MA = Z_GX + X_WIDTH
Z_MB = Z_MA + D_MODEL
Z_MC = Z_MB + D_MODEL

SUBLANES = 8
LANES = 128
VMEM_PHYSICAL_BYTES = 64 * 1024 * 1024
VMEM_LIMIT_BYTES = 56 * 1024 * 1024
VMEM_TEMP_BYTES = 12 * 1024 * 1024

S5_BLK_GROUPS = LANES // S5_GROUP
S5_BLK_STATE = S5_BLK_GROUPS * S5_STATE
S5_NBLK = S5_GROUPS // S5_BLK_GROUPS
SCAN_LEVELS = (1, 2, 4)
T_A = 0
T_POW = 2
T_LVL = T_POW + 2 * SUBLANES
T_CARRY = T_LVL + 2 * len(SCAN_LEVELS)
N_TABS = T_CARRY + 2


def _cparams(*sem, vmem_limit_bytes=VMEM_LIMIT_BYTES):
    return pltpu.CompilerParams(dimension_semantics=sem, vmem_limit_bytes=vmem_limit_bytes)


def _concat_cast_kernel(a_ref, b_ref, o_ref, *, na):
    i = pl.program_id(0)

    @pl.when(i < na)
    def _():
        o_ref[...] = a_ref[...].astype(o_ref.dtype)

    @pl.when(i >= na)
    def _():
        o_ref[...] = b_ref[...].astype(o_ref.dtype)


def _concat_cast(a, b, *, bm, out_dtype):
    k = a.shape[1]
    na, nb = a.shape[0] // bm, b.shape[0] // bm
    return pl.pallas_call(
        functools.partial(_concat_cast_kernel, na=na),
        grid=(na + nb,),
        in_specs=[
            pl.BlockSpec((bm, k), lambda i: (jnp.minimum(i, na - 1), 0)),
            pl.BlockSpec((bm, k), lambda i: (jnp.maximum(i - na, 0), 0), pipeline_mode=pl.Buffered(1)),
        ],
        out_specs=pl.BlockSpec((bm, k), lambda i: (i, 0)),
        out_shape=jax.ShapeDtypeStruct((a.shape[0] + b.shape[0], k), out_dtype),
        compiler_params=_cparams("arbitrary"),
        name="concat_cast",
    )(a, b)


def _colmm_kernel(*refs, n_lhs, terms, n_extra, epilogue, ncols, bn, guest):
    lhs = refs[:n_lhs]
    w_hbm = refs[n_lhs:n_lhs + len(terms)]
    extras = refs[n_lhs + len(terms):n_lhs + len(terms) + n_extra]
    rest = refs[n_lhs + len(terms) + n_extra:]
    n_gin, n_gout, guest_body = guest
    guest_in, o_ref, guest_out = rest[:n_gin], rest[n_gin], rest[n_gin + 1:n_gin + 1 + n_gout]
    stage, wb_scr, sem = rest[n_gin + 1 + n_gout:]
    j = pl.program_id(0)
    i = pl.program_id(1)
    if guest_body is not None:
        guest_body(*guest_in, *guest_out)

    def tile_copies(col):
        return [pltpu.make_async_copy(
            w_hbm[t].at[:, pl.ds(pl.multiple_of(off + col * bn, LANES), bn)],
            stage.at[pl.ds(row0, kt), :], sem.at[t]) for t, (_, row0, kt, off) in enumerate(terms)]

    @pl.when(i == 0)
    def _():
        @pl.when(j == 0)
        def _():
            for c in tile_copies(0):
                c.start()

        for c in tile_copies(j):
            c.wait()
        wb_scr[...] = stage[...].astype(BF16)

        @pl.when(j + 1 < ncols)
        def _():
            for c in tile_copies(j + 1):
                c.start()

    prods = [jnp.dot(lhs[li][...], wb_scr[row0:row0 + kt, :], preferred_element_type=F32)
             for li, row0, kt, _ in terms]
    o_ref[...] = epilogue(prods, [e[...] for e in extras]).astype(o_ref.dtype)


def _colmm(lhs, weights, extras, epilogue, *, n_out, bm, bn, out_dtype, name, guest=None, guest_bytes=0):
    m = lhs[0].shape[0]
    g_args, g_in_specs, g_out_specs, g_out_shape, g_body = guest or ((), [], [], [], None)
    terms, row0 = [], 0
    for li, w, off in weights:
        kt = w.shape[0]
        assert lhs[li].shape == (m, kt) and off % LANES == 0
        terms.append((li, row0, kt, off))
        row0 += kt
    ktot = row0
    lhs_bytes = sum(2 * bm * a.shape[1] * a.dtype.itemsize for a in lhs)
    extra_bytes = sum(2 * bm * bn * a.dtype.itemsize for a, _, _ in extras)
    need = (ktot * bn * 6 + lhs_bytes + extra_bytes + 2 * bm * bn * jnp.dtype(out_dtype).itemsize
            + len(terms) * bm * bn * 4 + guest_bytes)
    limit = min(max(VMEM_LIMIT_BYTES, need + VMEM_TEMP_BYTES), VMEM_PHYSICAL_BYTES - (2 << 20))
    assert need + (4 << 20) <= limit, (name, need, limit)
    kern = functools.partial(_colmm_kernel, n_lhs=len(lhs), terms=tuple(terms), n_extra=len(extras),
                             epilogue=epilogue, ncols=n_out // bn, bn=bn,
                             guest=(len(g_args), len(g_out_specs), g_body))
    extra_spec = lambda rb, cb: pl.BlockSpec((bm, bn), lambda j, i: (rb + i, cb + j))
    outs = pl.pallas_call(
        kern,
        grid=(n_out // bn, m // bm),
        in_specs=([pl.BlockSpec((bm, a.shape[1]), lambda j, i: (i, 0)) for a in lhs]
                  + [pl.BlockSpec(memory_space=pl.ANY) for _ in terms]
                  + [extra_spec(rb, cb) for _, rb, cb in extras] + list(g_in_specs)),
        out_specs=[pl.BlockSpec((bm, bn), lambda j, i: (i, j))] + list(g_out_specs),
        out_shape=[jax.ShapeDtypeStruct((m, n_out), out_dtype)] + list(g_out_shape),
        scratch_shapes=[pltpu.VMEM((ktot, bn), F32), pltpu.VMEM((ktot, bn), BF16),
                        pltpu.SemaphoreType.DMA((len(terms),))],
        compiler_params=_cparams("arbitrary", "arbitrary", vmem_limit_bytes=limit),
        name=name,
    )(*lhs, *[w for _, w, _ in weights], *[a for a, _, _ in extras], *g_args)
    return outs if guest else outs[0]


def _rotate(x, cos, sin_next, sin_prev):
    return x * cos + pltpu.roll(x, LANES - 1, 1) * sin_next + pltpu.roll(x, 1, 1) * sin_prev


def _ret_block(q, k, v, g, cos, sin_next, sin_prev, mask, xi, zeta, gc, states, slen):
    nseq = len(states)
    rows = q.shape[0]
    qr = _rotate(q, cos, sin_next, sin_prev)
    kr = _rotate(k, cos, sin_next, sin_prev) * (RET_DK ** -0.5)
    qb = qr.astype(BF16)
    kb = kr.astype(BF16)
    sc = lax.dot_general(qb, kb, (((1,), (1,)), ((), ())), preferred_element_type=F32) * mask
    o = jnp.dot(sc.astype(BF16), v, preferred_element_type=F32)
    qx = qr * xi
    kzt = (kr * zeta).T.astype(BF16)
    new_states = []
    if nseq == 1:
        s = states[0]
        o = o + jnp.dot(qx.astype(BF16), s.astype(BF16), preferred_element_type=F32)
        new_states.append(gc * s + jnp.dot(kzt, v, preferred_element_type=F32))
    else:
        pair = 2 * slen
        assert pair == 2 * SUBLANES and nseq % 2 == 0
        row_in_pair = lax.broadcasted_iota(jnp.int32, (pair, RET_DV), 0)
        row_seq = lax.broadcasted_iota(jnp.int32, (rows, RET_DV), 0) // slen
        parts = []
        for m in range(nseq // 2):
            qpair = qx[m * pair:(m + 1) * pair].astype(BF16)
            o0 = jnp.dot(qpair, states[2 * m].astype(BF16), preferred_element_type=F32)
            o1 = jnp.dot(qpair, states[2 * m + 1].astype(BF16), preferred_element_type=F32)
            parts.append(jnp.where(row_in_pair < slen, o0, o1))
        o = o + jnp.concatenate(parts, axis=0)
        vf = v.astype(F32)
        for n in range(nseq):
            vn = jnp.where(row_seq == n, vf, 0.0).astype(BF16)
            new_states.append(gc * states[n] + jnp.dot(kzt, vn, preferred_element_type=F32))
    mu = jnp.mean(o, axis=-1, keepdims=True)
    d = o - mu
    var = jnp.mean(d * d, axis=-1, keepdims=True)
    on = d * lax.rsqrt(var + GN_EPS)
    out = (on * (g * jax.nn.sigmoid(g))).astype(BF16)
    return out, new_states


def _ret_prompt_kernel(gc_ref, q_ref, k_ref, v_ref, g_ref, cos_ref, sn_ref, sp_ref, mask_ref, xi_ref, zeta_ref,
                       o_ref, sfin_ref, s_scr, *, chunk, nchunks, hb):
    head0 = pl.program_id(1) * hb
    s_scr[...] = jnp.zeros_like(s_scr)

    def body(c, carry):
        rows = pl.ds(pl.multiple_of(c * chunk, chunk), chunk)
        cos, sn, sp = cos_ref[rows, :], sn_ref[rows, :], sp_ref[rows, :]
        for hh in range(hb):
            qc = slice(hh * RET_DK, (hh + 1) * RET_DK)
            vc = slice(hh * RET_DV, (hh + 1) * RET_DV)
            out, (s_new,) = _ret_block(
                q_ref[rows, qc].astype(F32), k_ref[rows, qc].astype(F32), v_ref[rows, vc],
                g_ref[rows, vc].astype(F32), cos, sn, sp, mask_ref[hh], xi_ref[hh], zeta_ref[hh],
                gc_ref[head0 + hh], [s_scr[hh]], chunk)
            o_ref[rows, vc] = out
            s_scr[hh] = s_new
        return carry

    lax.fori_loop(0, nchunks, body, 0, unroll=2)
    sfin_ref[...] = s_scr[...]


def _ret_sample_body(head, gc_ref, q_ref, k_ref, v_ref, g_ref, cos_ref, sn_ref, sp_ref, mask_ref, xi_ref, zeta_ref,
                     s0_ref, o_ref, sfin_ref, *, slen, nseq):
    gc = gc_ref[head]
    states = [s0_ref[n] for n in range(nseq)]
    out, new_states = _ret_block(
        q_ref[...].astype(F32), k_ref[...].astype(F32), v_ref[...], g_ref[...].astype(F32),
        cos_ref[...], sn_ref[...], sp_ref[...], mask_ref[...], xi_ref[...], zeta_ref[...], gc, states, slen)
    o_ref[...] = out
    for n in range(nseq):
        sfin_ref[n] = new_states[n]


def _rope_tables(pos):
    half = RET_DK // 2
    inv = 1.0 / (ROPE_BASE ** (np.arange(half, dtype=np.float64) / half))
    ang = np.asarray(pos, np.float64)[:, None] * inv[None, :]
    cos = np.repeat(np.cos(ang), 2, axis=1)
    sin = np.repeat(np.sin(ang), 2, axis=1)
    even = (np.arange(RET_DK) % 2) == 0
    return (np.asarray(cos, np.float32), np.asarray(np.where(even, -sin, 0.0), np.float32),
            np.asarray(np.where(even, 0.0, sin), np.float32))


def _decay_tables(slen, nseq):
    lg = np.log1p(-np.exp2(-5.0 - np.arange(RET_HEADS, dtype=np.float64)))
    idx = np.arange(slen, dtype=np.float64)
    rel = idx[:, None] - idx[None, :]
    inner = np.where(rel[None] >= 0, np.exp(lg[:, None, None] * np.maximum(rel, 0.0)[None]), 0.0)
    xi = np.exp(lg[:, None] * (idx + 1.0))
    zeta = np.exp(lg[:, None] * (slen - 1.0 - idx))
    gc = np.exp(lg * slen)
    mask = np.einsum("nm,hij->hnimj", np.eye(nseq), inner).reshape(RET_HEADS, nseq * slen, nseq * slen)
    rows = nseq * slen
    xi_t = np.broadcast_to(np.tile(xi, (1, nseq))[:, :, None], (RET_HEADS, rows, RET_DK))
    zeta_t = np.broadcast_to(np.tile(zeta, (1, nseq))[:, :, None], (RET_HEADS, rows, RET_DK))
    f32 = lambda t: np.ascontiguousarray(t, dtype=np.float32)
    return f32(mask), f32(xi_t), f32(zeta_t), f32(gc)


def _retention_prompt(z, nbatch, seqlen):
    chunk = RET_CHUNK
    hb = 4
    cos, sn, sp = _rope_tables(np.arange(seqlen))
    mask, xi, zeta, gc = _decay_tables(chunk, 1)
    tab = lambda: pl.BlockSpec((seqlen, RET_DK), lambda b, h: (0, 0))
    head_tab = lambda w: pl.BlockSpec((hb, chunk, w), lambda b, h: (h, 0, 0))
    kern = functools.partial(_ret_prompt_kernel, chunk=chunk, nchunks=seqlen // chunk, hb=hb)
    qk_w, v_w = hb * RET_DK, hb * RET_DV
    return pl.pallas_call(
        kern,
        grid=(nbatch, RET_HEADS // hb),
        in_specs=[
            pl.BlockSpec(memory_space=pltpu.SMEM),
            pl.BlockSpec((seqlen, qk_w), lambda b, h: (b, Z_Q // qk_w + h)),
            pl.BlockSpec((seqlen, qk_w), lambda b, h: (b, Z_K // qk_w + h)),
            pl.BlockSpec((seqlen, v_w), lambda b, h: (b, Z_V // v_w + h)),
            pl.BlockSpec((seqlen, v_w), lambda b, h: (b, Z_GRET // v_w + h)),
            tab(), tab(), tab(),
            head_tab(chunk), head_tab(RET_DK), head_tab(RET_DK),
        ],
        out_specs=[
            pl.BlockSpec((seqlen, v_w), lambda b, h: (b, h)),
            pl.BlockSpec((None, hb, RET_DK, RET_DV), lambda b, h: (b, h, 0, 0)),
        ],
        out_shape=[
            jax.ShapeDtypeStruct((nbatch * seqlen, RET_V), BF16),
            jax.ShapeDtypeStruct((nbatch, RET_HEADS, RET_DK, RET_DV), F32),
        ],
        scratch_shapes=[pltpu.VMEM((hb, RET_DK, RET_DV), F32)],
        compiler_params=_cparams("parallel", "parallel"),
        name="retention_prompt",
    )(gc, z, z, z, z, cos, sn, sp, mask, xi, zeta)


def _retention_sample_operands(z, s0, row_off, nbatch, slen, pos0, nseq, block_of):
    rows = nseq * slen
    cos, sn, sp = (np.tile(t, (nseq, 1)) for t in _rope_tables(pos0 + np.arange(slen)))
    mask, xi, zeta, gc = _decay_tables(slen, nseq)
    rb = row_off // rows

    def at(fn):
        return lambda *ids: fn(*block_of(*ids))

    tab = lambda: pl.BlockSpec((rows, RET_DK), lambda *ids: (0, 0))
    head_tab = lambda w: pl.BlockSpec((None, rows, w), at(lambda i, h: (h, 0, 0)))
    state = lambda: pl.BlockSpec((nseq, None, RET_DK, RET_DV), at(lambda i, h: (i, h, 0, 0)))
    in_specs = [
        pl.BlockSpec(memory_space=pltpu.SMEM),
        pl.BlockSpec((rows, RET_DK), at(lambda i, h: (rb + i, Z_Q // RET_DK + h))),
        pl.BlockSpec((rows, RET_DK), at(lambda i, h: (rb + i, Z_K // RET_DK + h))),
        pl.BlockSpec((rows, RET_DV), at(lambda i, h: (rb + i, Z_V // RET_DV + h))),
        pl.BlockSpec((rows, RET_DV), at(lambda i, h: (rb + i, Z_GRET // RET_DV + h))),
        tab(), tab(), tab(),
        head_tab(rows), head_tab(RET_DK), head_tab(RET_DK),
        state(),
    ]
    out_specs = [pl.BlockSpec((rows, RET_DV), at(lambda i, h: (i, h))), state()]
    out_shape = [jax.ShapeDtypeStruct((nbatch * slen, RET_V), BF16),
                 jax.ShapeDtypeStruct((nbatch, RET_HEADS, RET_DK, RET_DV), F32)]
    return (gc, z, z, z, z, cos, sn, sp, mask, xi, zeta, s0), in_specs, out_specs, out_shape


def _s5_prep_kernel(are_ref, aim_ref, dt_ref, are_w_ref, aim_w_ref, dt_w_ref, bre_ref, bim_ref, cim_ref,
                    tabs_ref, bbre_ref, bbim_ref, ncim_ref):
    def abar(ar, ai, dt):
        mag = jnp.exp(dt * ar)
        return mag * jnp.cos(dt * ai), mag * jnp.sin(dt * ai)

    def powers(ar, ai):
        out = [(ar, ai)]
        for _ in range(1, SUBLANES):
            pr, pi = out[-1]
            out.append((pr * ar - pi * ai, pr * ai + pi * ar))
        return out

    pw = powers(*abar(are_ref[...], aim_ref[...], dt_ref[...]))
    qw = powers(*pw[-1])
    k_idx = lax.broadcasted_iota(jnp.int32, (SUBLANES, S5_BLK_STATE), 0)
    for j in range(S5_NBLK):
        every_row = lambda v: jnp.broadcast_to(v[j:j + 1, :], (SUBLANES, S5_BLK_STATE))
        for c in range(2):
            tabs_ref[j, T_A + c] = every_row(pw[0][c])
            for i in range(SUBLANES):
                tabs_ref[j, T_POW + 2 * i + c] = every_row(pw[i][c])
            for l, lvl in enumerate(SCAN_LEVELS):
                tabs_ref[j, T_LVL + 2 * l + c] = jnp.where(k_idx >= lvl, every_row(qw[lvl - 1][c]), 0.0)
            carry = every_row(qw[0][c])
            for k in range(1, SUBLANES):
                carry = jnp.where(k_idx == k, every_row(qw[k][c]), carry)
            tabs_ref[j, T_CARRY + c] = carry

    a_r, a_i = are_w_ref[...], aim_w_ref[...]
    w_r, w_i = abar(a_r, a_i, dt_w_ref[...])
    den = a_r * a_r + a_i * a_i
    x_re = w_r - 1.0
    f_re = (x_re * a_r + w_i * a_i) / den
    f_im = (w_i * a_r - x_re * a_i) / den
    br, bi = bre_ref[...], bim_ref[...]
    bbre_ref[...] = f_re * br - f_im * bi
    bbim_ref[...] = f_re * bi + f_im * br
    ncim_ref[...] = -cim_ref[...]


def _s5_prepare(a_re, a_im, log_step, b_re, b_im, c_re, c_im):
    g, n, p = S5_GROUPS, S5_STATE, S5_GROUP
    dt = jnp.broadcast_to(jnp.exp(log_step.astype(F32))[:, None], (g, n))
    wide = lambda t: jnp.repeat(t, p, axis=1)
    vm = lambda: pl.BlockSpec(memory_space=pltpu.VMEM)
    nb, bg = S5_NBLK, S5_BLK_GROUPS
    per_blk = lambda t: t.reshape(nb, S5_BLK_STATE)
    tabs, bbre, bbim, ncim = pl.pallas_call(
        _s5_prep_kernel,
        in_specs=[vm() for _ in range(9)],
        out_specs=[vm() for _ in range(4)],
        out_shape=[
            jax.ShapeDtypeStruct((nb, N_TABS, SUBLANES, S5_BLK_STATE), F32),
            jax.ShapeDtypeStruct((g, n * p), F32),
            jax.ShapeDtypeStruct((g, n * p), F32),
            jax.ShapeDtypeStruct((g, p * n), F32),
        ],
        name="s5_discretize",
    )(per_blk(a_re.astype(F32)), per_blk(a_im.astype(F32)), per_blk(dt),
      wide(a_re.astype(F32)), wide(a_im.astype(F32)), wide(dt),
      b_re.astype(F32).reshape(g, n * p), b_im.astype(F32).reshape(g, n * p), c_im.astype(F32).reshape(g, p * n))

    eye = jnp.eye(bg, dtype=bool)

    def in_blockdiag(t):
        t = t.reshape(nb, bg, n, p).transpose(0, 1, 3, 2)
        return jnp.where(eye[None, :, None, :, None], t[:, :, :, None, :], 0.0).reshape(nb, bg * p, bg * n)

    def out_blockdiag(t):
        t = t.reshape(nb, bg, p, n).transpose(0, 1, 3, 2)
        return jnp.where(eye[None, :, None, :, None], t[:, :, :, None, :], 0.0).reshape(nb, bg * n, bg * p)

    w_in = jnp.concatenate([in_blockdiag(bbre), in_blockdiag(bbim)], axis=-1).astype(BF16)
    w_out = jnp.concatenate([out_blockdiag(c_re.astype(F32).reshape(g, p * n)), out_blockdiag(ncim)],
                            axis=1).astype(BF16)
    return w_in, w_out, tabs


S5_SUB = SUBLANES * SUBLANES


def _segment_scan(er, ei, tabs_ref, cr, ci):
    for l, lvl in enumerate(SCAN_LEVELS):
        pr = tabs_ref[T_LVL + 2 * l]
        pi = tabs_ref[T_LVL + 2 * l + 1]
        sr = pltpu.roll(er, lvl, 0)
        si = pltpu.roll(ei, lvl, 0)
        er, ei = er + pr * sr - pi * si, ei + pr * si + pi * sr
    rr = tabs_ref[T_CARRY]
    ri = tabs_ref[T_CARRY + 1]
    crb = jnp.broadcast_to(cr, er.shape)
    cib = jnp.broadcast_to(ci, ei.shape)
    return er + rr * crb - ri * cib, ei + rr * cib + ri * crb


def _transpose_tiles(ref, nsub):
    return jnp.concatenate([ref[pl.ds(S5_SUB * m + i, SUBLANES, stride=SUBLANES), :]
                            for m in range(nsub) for i in range(SUBLANES)], axis=0)


def _s5_stream(u_ref, win_ref, wout_ref, d_ref, tabs_ref, uf_scr, bu_scr, h_scr, y_scr, init, long_seq):
    tc = u_ref.shape[0]
    nsub = tc // S5_SUB
    ns = S5_BLK_STATE
    uf = u_ref[...].astype(F32)
    uf_scr[...] = uf
    up = _transpose_tiles(uf_scr, nsub).astype(BF16)
    bu_scr[...] = jnp.dot(up, win_ref[...], preferred_element_type=F32)
    ar = tabs_ref[T_A]
    ai = tabs_ref[T_A + 1]
    finals = []
    for m in range(nsub):
        tile = lambda i: slice(S5_SUB * m + SUBLANES * i, S5_SUB * m + SUBLANES * (i + 1))
        if long_seq:
            hr = hi = None
        else:
            hr = init[0][SUBLANES * m:SUBLANES * (m + 1), :]
            hi = init[1][SUBLANES * m:SUBLANES * (m + 1), :]
        for i in range(SUBLANES):
            xr = bu_scr[tile(i), :ns]
            xi = bu_scr[tile(i), ns:]
            if hr is None:
                hr, hi = xr, xi
            else:
                hr, hi = ar * hr - ai * hi + xr, ar * hi + ai * hr + xi
            h_scr[tile(i), :ns] = hr
            h_scr[tile(i), ns:] = hi
        if long_seq:
            cr, ci = init
            fr, fi = _segment_scan(hr, hi, tabs_ref, cr, ci)
            first = lax.broadcasted_iota(jnp.int32, fr.shape, 0) == 0
            sr = jnp.where(first, jnp.broadcast_to(cr, fr.shape), pltpu.roll(fr, 1, 0))
            si = jnp.where(first, jnp.broadcast_to(ci, fi.shape), pltpu.roll(fi, 1, 0))
            init = (fr[SUBLANES - 1:, :], fi[SUBLANES - 1:, :])
            for i in range(SUBLANES):
                pr = tabs_ref[T_POW + 2 * i]
                pi = tabs_ref[T_POW + 2 * i + 1]
                h_scr[tile(i), :ns] = h_scr[tile(i), :ns] + pr * sr - pi * si
                h_scr[tile(i), ns:] = h_scr[tile(i), ns:] + pr * si + pi * sr
        else:
            finals.append((hr, hi))
    y_scr[...] = jnp.dot(h_scr[...].astype(BF16), wout_ref[...], preferred_element_type=F32)
    y = _transpose_tiles(y_scr, nsub) + d_ref[...] * uf
    return jax.nn.gelu(y).astype(BF16), (init if long_seq else finals)


N_RET_SAMPLE_IN = 12


def _s5_prompt_kernel(*refs, nb, nt, slen, nseq):
    u_refs = refs[:nb]
    win_ref, wout_ref, d_ref, tabs_ref = refs[nb:nb + 4]
    ret_in = refs[nb + 4:nb + 4 + N_RET_SAMPLE_IN]
    mem_ref, wkv_ref = refs[nb + 4 + N_RET_SAMPLE_IN:nb + 6 + N_RET_SAMPLE_IN]
    (gl_ref, hre_ref, him_ref, o_ret_ref, sfin_ref, kv_ref,
     uf_scr, bu_scr, h_scr, y_scr, carry_scr, memb_scr) = refs[nb + 6 + N_RET_SAMPLE_IN:]
    ns = S5_BLK_STATE

    head = (pl.program_id(0) * nt + pl.program_id(2)) % RET_HEADS
    _ret_sample_body(head, *ret_in, o_ret_ref, sfin_ref, slen=slen, nseq=nseq)

    @pl.when(pl.program_id(2) == 0)
    def _():
        @pl.when(pl.program_id(0) == 0)
        def _():
            memb_scr[...] = mem_ref[...].astype(BF16)

        kv_ref[...] = jnp.dot(memb_scr[...], wkv_ref[...].astype(BF16), preferred_element_type=F32)

    @pl.when(pl.program_id(2) == 0)
    def _():
        carry_scr[...] = jnp.zeros_like(carry_scr)

    for s in range(nb):
        init = (carry_scr[s, 0:1, :ns], carry_scr[s, 0:1, ns:])
        out, (cr, ci) = _s5_stream(u_refs[s], win_ref, wout_ref, d_ref, tabs_ref, uf_scr.at[s], bu_scr.at[s],
                                   h_scr.at[s], y_scr.at[s], init, True)
        gl_ref[s] = out
        carry_scr[s, 0:1, :ns] = cr
        carry_scr[s, 0:1, ns:] = ci
        hre_ref[s] = cr
        him_ref[s] = ci


def _s5_sample_kernel(u_ref, win_ref, wout_ref, d_ref, tabs_ref, h0re_ref, h0im_ref, gl_ref, hre_ref, him_ref,
                      uf_scr, bu_scr, h_scr, y_scr):
    out, finals = _s5_stream(u_ref, win_ref, wout_ref, d_ref, tabs_ref, uf_scr, bu_scr, h_scr, y_scr,
                             (h0re_ref, h0im_ref), False)
    gl_ref[...] = out
    for m, (hr, hi) in enumerate(finals):
        hre_ref[SUBLANES * m:SUBLANES * (m + 1), :] = hr
        him_ref[SUBLANES * m:SUBLANES * (m + 1), :] = hi


def _s5_weight_specs():
    nst = S5_BLK_STATE
    return [
        pl.BlockSpec((None, LANES, 2 * nst), lambda j, *_: (j, 0, 0)),
        pl.BlockSpec((None, 2 * nst, LANES), lambda j, *_: (j, 0, 0)),
        pl.BlockSpec((1, LANES), lambda j, *_: (0, j)),
        pl.BlockSpec((None, N_TABS, SUBLANES, nst), lambda j, *_: (j, 0, 0, 0)),
    ]


def _s5_scratch(lead, tc):
    nst = S5_BLK_STATE
    return [pltpu.VMEM(lead + (tc, LANES), F32), pltpu.VMEM(lead + (tc, 2 * nst), F32),
            pltpu.VMEM(lead + (tc, 2 * nst), F32), pltpu.VMEM(lead + (tc, LANES), F32)]


def _s5_prompt_ret_sample(z, w_in, w_out, d_skip, tabs, nbatch, seqlen, s0, row_off, nbatch_s, slen, pos0,
                          mem, w_kv):
    tc = 256
    nb = nbatch
    nt = seqlen // tc
    nst = S5_BLK_STATE
    nseq = 16
    steps = S5_NBLK * nt
    assert steps == (nbatch_s // nseq) * RET_HEADS
    ret_args, ret_in_specs, ret_out_specs, ret_out_shape = _retention_sample_operands(
        z, s0, row_off, nbatch_s, slen, pos0, nseq, lambda j, bb, t: divmod(j * nt + t, RET_HEADS))
    assert len(ret_args) == N_RET_SAMPLE_IN
    kern = functools.partial(_s5_prompt_kernel, nb=nb, nt=nt, slen=slen, nseq=nseq)
    u_spec = lambda s: pl.BlockSpec((tc, LANES), lambda j, bb, t: (s * nt + t, Z_U // LANES + j))
    state_spec = lambda: pl.BlockSpec((nb, 1, nst), lambda j, bb, t: (0, 0, j))
    mrows, kdim = mem.shape
    kv_bn = w_kv.shape[1] // S5_NBLK
    gl, hre, him, o_ret, sfin, kv = pl.pallas_call(
        kern,
        grid=(S5_NBLK, 1, nt),
        in_specs=([u_spec(s) for s in range(nb)] + _s5_weight_specs() + ret_in_specs + [
            pl.BlockSpec((mrows, kdim), lambda j, bb, t: (0, 0), pipeline_mode=pl.Buffered(1)),
            pl.BlockSpec((kdim, kv_bn), lambda j, bb, t: (0, j)),
        ]),
        out_specs=([pl.BlockSpec((nb, tc, LANES), lambda j, bb, t: (0, t, j)), state_spec(), state_spec()]
                   + ret_out_specs + [pl.BlockSpec((mrows, kv_bn), lambda j, bb, t: (0, j))]),
        out_shape=[
            jax.ShapeDtypeStruct((nbatch, seqlen, S5_WIDTH), BF16),
            jax.ShapeDtypeStruct((nbatch, 1, S5_GROUPS * S5_STATE), F32),
            jax.ShapeDtypeStruct((nbatch, 1, S5_GROUPS * S5_STATE), F32),
        ] + ret_out_shape + [jax.ShapeDtypeStruct((mrows, w_kv.shape[1]), F32)],
        scratch_shapes=(_s5_scratch((nb,), tc) + [pltpu.VMEM((nb, SUBLANES, 2 * nst), F32)]
                        + [pltpu.VMEM((mrows, kdim), BF16)]),
        compiler_params=_cparams("arbitrary", "arbitrary", "arbitrary"),
        name="s5_prompt_ret_sample",
    )(*([z] * nb), w_in, w_out, d_skip, tabs, *ret_args, mem, w_kv)
    return (gl.reshape(nbatch * seqlen, S5_WIDTH), hre.reshape(nbatch, S5_GROUPS, S5_STATE),
            him.reshape(nbatch, S5_GROUPS, S5_STATE), o_ret, sfin, kv)


def _s5_sample(z, w_in, w_out, d_skip, tabs, h0_re, h0_im, row_off, nbatch, slen):
    assert slen == SUBLANES
    tc = 256
    nseq = tc // slen
    rb = row_off // tc
    nst = S5_BLK_STATE
    state_spec = lambda: pl.BlockSpec((nseq, nst), lambda j, i: (i, j))
    gl, hre, him = pl.pallas_call(
        _s5_sample_kernel,
        grid=(S5_NBLK, nbatch // nseq),
        in_specs=([pl.BlockSpec((tc, LANES), lambda j, i: (rb + i, Z_U // LANES + j))] + _s5_weight_specs()
                  + [state_spec(), state_spec()]),
        out_specs=[pl.BlockSpec((tc, LANES), lambda j, i: (i, j)), state_spec(), state_spec()],
        out_shape=[
            jax.ShapeDtypeStruct((nbatch * slen, S5_WIDTH), BF16),
            jax.ShapeDtypeStruct((nbatch, S5_GROUPS * S5_STATE), F32),
            jax.ShapeDtypeStruct((nbatch, S5_GROUPS * S5_STATE), F32),
        ],
        scratch_shapes=_s5_scratch((), tc),
        compiler_params=_cparams("parallel", "parallel"),
        name="s5_sample",
    )(z, w_in, w_out, d_skip, tabs, h0_re.reshape(nbatch, -1).astype(F32), h0_im.reshape(nbatch, -1).astype(F32))
    return gl, hre.reshape(nbatch, S5_GROUPS, S5_STATE), him.reshape(nbatch, S5_GROUPS, S5_STATE)


def _xattn_kernel(q_ref, g_ref, k_ref, v_ref, o_ref, *, q_per_seq, k_per_seq):
    tq = q_ref.shape[0]
    nk = k_ref.shape[0]
    kb = k_ref[...].astype(BF16)
    vb = v_ref[...].astype(BF16)
    if tq // q_per_seq > 1:
        same = (lax.broadcasted_iota(jnp.int32, (tq, nk), 0) // q_per_seq
                == lax.broadcasted_iota(jnp.int32, (tq, nk), 1) // k_per_seq)
    else:
        same = None
    for h in range(X_HEADS):
        cols = slice(h * X_HD, (h + 1) * X_HD)
        s = lax.dot_general(q_ref[:, cols], kb[:, cols], (((1,), (1,)), ((), ())),
                            preferred_element_type=F32) * (X_HD ** -0.5)
        if same is not None:
            s = jnp.where(same, s, -jnp.inf)
        e = jnp.exp(s - jnp.max(s, axis=-1, keepdims=True))
        oh = jnp.dot(e.astype(BF16), vb[:, cols], preferred_element_type=F32) / jnp.sum(e, axis=-1, keepdims=True)
        g = g_ref[:, cols].astype(F32)
        o_ref[:, cols] = (oh * (g * jax.nn.sigmoid(g))).astype(BF16)


def _xattn(z, mk, mv, *, row_off, nrows, tq, q_per_seq, seqs_per_step, kv_col_blk, name):
    nk = seqs_per_step * MEM_LEN
    steps_per_kv = (seqs_per_step * q_per_seq) // tq if tq < seqs_per_step * q_per_seq else 1
    rb = row_off // tq
    kern = functools.partial(_xattn_kernel, q_per_seq=q_per_seq, k_per_seq=MEM_LEN)
    kmap = lambda cb: (lambda i: (i // steps_per_kv, cb))
    return pl.pallas_call(
        kern,
        grid=(nrows // tq,),
        in_specs=[
            pl.BlockSpec((tq, X_WIDTH), lambda i: (rb + i, Z_QX // X_WIDTH)),
            pl.BlockSpec((tq, X_WIDTH), lambda i: (rb + i, Z_GX // X_WIDTH)),
            pl.BlockSpec((nk, X_WIDTH), kmap(kv_col_blk[0])),
            pl.BlockSpec((nk, X_WIDTH), kmap(kv_col_blk[1])),
        ],
        out_specs=pl.BlockSpec((tq, X_WIDTH), lambda i: (i, 0)),
        out_shape=jax.ShapeDtypeStruct((nrows, X_WIDTH), BF16),
        compiler_params=_cparams("parallel"),
        name=name,
    )(z, z, mk, mv)


def _xattn_sample_guest(z, ck, cv, *, row_off, nrows, q_per_seq, ncols_host, nrows_host):
    seqs = 2
    tq = seqs * q_per_seq
    nk = seqs * MEM_LEN
    assert ncols_host * nrows_host == nrows // tq
    rb = row_off // tq
    blk = lambda j, i: j * nrows_host + i
    in_specs = [
        pl.BlockSpec((tq, X_WIDTH), lambda j, i: (rb + blk(j, i), Z_QX // X_WIDTH)),
        pl.BlockSpec((tq, X_WIDTH), lambda j, i: (rb + blk(j, i), Z_GX // X_WIDTH)),
        pl.BlockSpec((nk, X_WIDTH), lambda j, i: (blk(j, i), 0)),
        pl.BlockSpec((nk, X_WIDTH), lambda j, i: (blk(j, i), 0)),
    ]
    out_specs = [pl.BlockSpec((tq, X_WIDTH), lambda j, i: (blk(j, i), 0))]
    out_shape = [jax.ShapeDtypeStruct((nrows, X_WIDTH), BF16)]
    body = functools.partial(_xattn_kernel, q_per_seq=q_per_seq, k_per_seq=MEM_LEN)
    vmem = 2 * 2 * nk * X_WIDTH * 4 + 2 * nk * X_WIDTH * 2
    return ((z, z, ck, cv), in_specs, out_specs, out_shape, body), vmem


def _cast_epilogue(prods, extras):
    return prods[0]


def _glu_epilogue(prods, extras):
    a, b = prods
    g = extras[0].astype(F32)
    return a * jax.nn.sigmoid(b) * (g * jax.nn.sigmoid(g))


def _glu(gl, w_glu, z, row_off, *, bm, bn, guest=None, guest_bytes=0):
    return _colmm([gl], [(0, w_glu, 0), (0, w_glu, S5_WIDTH)], [(z, row_off // bm, Z_GS5 // bn)], _glu_epilogue,
                  n_out=S5_WIDTH, bm=bm, bn=bn, out_dtype=BF16, name="glu", guest=guest, guest_bytes=guest_bytes)


def _merge_epilogue(prods, extras):
    out = jax.nn.sigmoid(extras[0].astype(F32)) * prods[0]
    for p, m in zip(prods[1:], extras[1:]):
        out = out + jax.nn.sigmoid(m.astype(F32)) * p
    return out


def _merge(o_a, o_b, o_c, w_a, w_b, w_c, z, row_off, *, bm, bn):
    rb = row_off // bm
    return _colmm([o_a, o_b, o_c], [(0, w_a, 0), (1, w_b, 0), (2, w_c, 0)],
                  [(z, rb, Z_MA // bn), (z, rb, Z_MB // bn), (z, rb, Z_MC // bn)], _merge_epilogue,
                  n_out=D_MODEL, bm=bm, bn=bn, out_dtype=BF16, name="merge")


def _residual_epilogue(prods, extras):
    return DN_ALPHA * extras[0] + prods[0]


def _layer_norm_kernel(p_ref, g_ref, b_ref, o_ref):
    x = p_ref[...]
    mu = jnp.mean(x, axis=-1, keepdims=True)
    d = x - mu
    var = jnp.mean(d * d, axis=-1, keepdims=True)
    o_ref[...] = d * lax.rsqrt(var + LN_EPS) * g_ref[...] + b_ref[...]


def _out_ln(merged, w_out, x, ln_g, ln_b, *, bm, bn, bm_ln):
    m, d = x.shape
    pre = _colmm([merged], [(0, w_out, 0)], [(x, 0, 0)], _residual_epilogue, n_out=d, bm=bm, bn=bn,
                 out_dtype=F32, name="out_proj")
    row = lambda: pl.BlockSpec((1, d), lambda i: (0, 0))
    return pl.pallas_call(
        _layer_norm_kernel,
        grid=(m // bm_ln,),
        in_specs=[pl.BlockSpec((bm_ln, d), lambda i: (i, 0)), row(), row()],
        out_specs=pl.BlockSpec((bm_ln, d), lambda i: (i, 0)),
        out_shape=jax.ShapeDtypeStruct((m, d), F32),
        compiler_params=_cparams("parallel"),
        name="layer_norm",
    )(pre, ln_g.reshape(1, d).astype(F32), ln_b.reshape(1, d).astype(F32))


def _merge_out(z, row_off, o_ret, o_s5, o_x, x2d, w):
    merged = _merge(o_ret, o_s5, o_x, w["proj_a"], w["proj_b"], w["proj_c"], z, row_off, bm=512, bn=512)
    return _out_ln(merged, w["out"], x2d, w["ln_g"], w["ln_b"], bm=512, bn=1024, bm_ln=256)


def kernel(x_prompt, x_sample, mem_prompt, state_ret, state_s5_re, state_s5_im, cache_mem_k, cache_mem_v, w_in, w_mem_kv, s5_a_re, s5_a_im, s5_log_step, s5_b_re, s5_b_im, s5_c_re, s5_c_im, s5_d, w_glu, w_proj_a, w_proj_b, w_proj_c, w_out, ln_g, ln_b):
    depth = w_in.shape[0]
    assert depth == 1
    l = 0
    n_p = BATCH * SEQ
    n_s = DEC_BATCH * DEC_SEQ
    xp2 = x_prompt.reshape(n_p, D_MODEL)
    xs2 = x_sample.reshape(n_s, D_MODEL)

    w = dict(glu=w_glu[l], proj_a=w_proj_a[l], proj_b=w_proj_b[l], proj_c=w_proj_c[l],
             out=w_out[l], ln_g=ln_g[l], ln_b=ln_b[l])

    xb = _concat_cast(xp2, xs2, bm=512, out_dtype=BF16)
    z = _colmm([xb], [(0, w_in[l], 0)], [], _cast_epilogue, n_out=IN_WIDTH, bm=1024, bn=1024, out_dtype=BF16,
               name="in_proj")

    mem2 = mem_prompt.reshape(BATCH * MEM_LEN, D_MODEL)
    s5_win, s5_wout, s5_tabs = _s5_prepare(s5_a_re[l], s5_a_im[l], s5_log_step[l], s5_b_re[l], s5_b_im[l],
                                           s5_c_re[l], s5_c_im[l])
    d_skip = s5_d[l].reshape(1, S5_WIDTH).astype(F32)

    o_ret_p, ret_p = _retention_prompt(z, BATCH, SEQ)
    gl_p, hre_p, him_p, o_ret_s, ret_s, kv = _s5_prompt_ret_sample(
        z, s5_win, s5_wout, d_skip, s5_tabs, BATCH, SEQ, state_ret[l], n_p, DEC_BATCH, DEC_SEQ, PAST_LEN,
        mem2, w_mem_kv[l])
    mk, mv = kv[:, :X_WIDTH], kv[:, X_WIDTH:]
    o_x_p = _xattn(z, kv, kv, row_off=0, nrows=n_p, tq=1024, q_per_seq=SEQ, seqs_per_step=1,
                   kv_col_blk=(0, 1), name="xattn_prompt")
    ck = cache_mem_k[l].reshape(DEC_BATCH * MEM_LEN, X_WIDTH)
    cv = cache_mem_v[l].reshape(DEC_BATCH * MEM_LEN, X_WIDTH)
    glu_bm, glu_bn = 512, 512
    guest, guest_bytes = _xattn_sample_guest(z, ck, cv, row_off=n_p, nrows=n_s, q_per_seq=DEC_SEQ,
                                             ncols_host=S5_WIDTH // glu_bn, nrows_host=n_p // glu_bm)
    o_s5_p, o_x_s = _glu(gl_p, w["glu"], z, 0, bm=glu_bm, bn=glu_bn, guest=guest, guest_bytes=guest_bytes)
    y_p = _merge_out(z, 0, o_ret_p, o_s5_p, o_x_p, xp2, w)

    gl_s, hre_s, him_s = _s5_sample(z, s5_win, s5_wout, d_skip, s5_tabs, state_s5_re[l], state_s5_im[l],
                                    n_p, DEC_BATCH, DEC_SEQ)
    o_s5_s = _glu(gl_s, w["glu"], z, n_p, bm=1024, bn=512)
    y_s = _merge_out(z, n_p, o_ret_s, o_s5_s, o_x_s, xs2, w)

    return (y_p.reshape(BATCH, SEQ, D_MODEL), y_s.reshape(DEC_BATCH, DEC_SEQ, D_MODEL),
            ret_p[None], hre_p[None], him_p[None],
            mk.reshape(1, BATCH, MEM_LEN, X_HEADS, X_HD), mv.reshape(1, BATCH, MEM_LEN, X_HEADS, X_HD),
            ret_s[None], hre_s[None], him_s[None])
```

```python
import functools
import math

import jax
import jax.numpy as jnp
import numpy as np
from jax import lax
from jax.experimental import pallas as pl
from jax.experimental.pallas import tpu as pltpu

F32 = jnp.float32
BF16 = jnp.bfloat16

D_MODEL = 4096
BATCH = 4
SEQ = 2048
DEC_BATCH = 128
DEC_SEQ = 8
PAST_LEN = 16384

RET_HEADS = 16
RET_DK = 128
RET_DV = 256
RET_QK = RET_HEADS * RET_DK
RET_V = RET_HEADS * RET_DV
RET_CHUNK = 128
ROPE_BASE = 10000.0

S5_WIDTH = D_MODEL // 2
S5_GROUP = 16
S5_GROUPS = S5_WIDTH // S5_GROUP
S5_STATE = 64

X_HEADS = 4
X_WIDTH = D_MODEL // 2
X_HD = X_WIDTH // X_HEADS
MEM_LEN = 256

DN_ALPHA = 2.0 ** 0.25
LN_EPS = 1e-5
GN_EPS = 1e-5

IN_WIDTH = 2 * RET_QK + 2 * RET_V + 2 * S5_WIDTH + 2 * X_WIDTH + 3 * D_MODEL

Z_Q = 0
Z_K = Z_Q + RET_QK
Z_V = Z_K + RET_QK
Z_GRET = Z_V + RET_V
Z_U = Z_GRET + RET_V
Z_GS5 = Z_U + S5_WIDTH
Z_QX = Z_GS5 + S5_WIDTH
Z_GX = Z_QX + X_WIDTH
Z_MA = Z_GX + X_WIDTH
Z_MB = Z_MA + D_MODEL
Z_MC = Z_MB + D_MODEL

SUBLANES = 8
LANES = 128
VMEM_PHYSICAL_BYTES = 64 * 1024 * 1024
VMEM_LIMIT_BYTES = 56 * 1024 * 1024
VMEM_TEMP_BYTES = 12 * 1024 * 1024

S5_BLK_GROUPS = LANES // S5_GROUP
S5_BLK_STATE = S5_BLK_GROUPS * S5_STATE
S5_NBLK = S5_GROUPS // S5_BLK_GROUPS
SCAN_LEVELS = (1, 2, 4)
T_A = 0
T_POW = 2
T_LVL = T_POW + 2 * SUBLANES
T_CARRY = T_LVL + 2 * len(SCAN_LEVELS)
N_TABS = T_CARRY + 2


def _cparams(*sem, vmem_limit_bytes=VMEM_LIMIT_BYTES):
    return pltpu.CompilerParams(dimension_semantics=sem, vmem_limit_bytes=vmem_limit_bytes)


def _concat_cast_kernel(a_ref, b_ref, o_ref, *, na):
    i = pl.program_id(0)

    @pl.when(i < na)
    def _():
        o_ref[...] = a_ref[...].astype(o_ref.dtype)

    @pl.when(i >= na)
    def _():
        o_ref[...] = b_ref[...].astype(o_ref.dtype)


def _concat_cast(a, b, *, bm, out_dtype):
    k = a.shape[1]
    na, nb = a.shape[0] // bm, b.shape[0] // bm
    return pl.pallas_call(
        functools.partial(_concat_cast_kernel, na=na),
        grid=(na + nb,),
        in_specs=[
            pl.BlockSpec((bm, k), lambda i: (jnp.minimum(i, na - 1), 0)),
            pl.BlockSpec((bm, k), lambda i: (jnp.maximum(i - na, 0), 0), pipeline_mode=pl.Buffered(1)),
        ],
        out_specs=pl.BlockSpec((bm, k), lambda i: (i, 0)),
        out_shape=jax.ShapeDtypeStruct((a.shape[0] + b.shape[0], k), out_dtype),
        compiler_params=_cparams("arbitrary"),
        name="concat_cast",
    )(a, b)


def _colmm_kernel(*refs, n_lhs, terms, n_extra, epilogue, ncols, bn, guest):
    lhs = refs[:n_lhs]
    w_hbm = refs[n_lhs:n_lhs + len(terms)]
    extras = refs[n_lhs + len(terms):n_lhs + len(terms) + n_extra]
    rest = refs[n_lhs + len(terms) + n_extra:]
    n_gin, n_gout, guest_body = guest
    guest_in, o_ref, guest_out = rest[:n_gin], rest[n_gin], rest[n_gin + 1:n_gin + 1 + n_gout]
    stage, wb_scr, sem = rest[n_gin + 1 + n_gout:]
    j = pl.program_id(0)
    i = pl.program_id(1)
    if guest_body is not None:
        guest_body(*guest_in, *guest_out)

    def tile_copies(col):
        return [pltpu.make_async_copy(
            w_hbm[t].at[:, pl.ds(pl.multiple_of(off + col * bn, LANES), bn)],
            stage.at[pl.ds(row0, kt), :], sem.at[t]) for t, (_, row0, kt, off) in enumerate(terms)]

    @pl.when(i == 0)
    def _():
        @pl.when(j == 0)
        def _():
            for c in tile_copies(0):
                c.start()

        for c in tile_copies(j):
            c.wait()
        wb_scr[...] = stage[...].astype(BF16)

        @pl.when(j + 1 < ncols)
        def _():
            for c in tile_copies(j + 1):
                c.start()

    prods = [jnp.dot(lhs[li][...], wb_scr[row0:row0 + kt, :], preferred_element_type=F32)
             for li, row0, kt, _ in terms]
    o_ref[...] = epilogue(prods, [e[...] for e in extras]).astype(o_ref.dtype)


def _colmm(lhs, weights, extras, epilogue, *, n_out, bm, bn, out_dtype, name, guest=None, guest_bytes=0):
    m = lhs[0].shape[0]
    g_args, g_in_specs, g_out_specs, g_out_shape, g_body = guest or ((), [], [], [], None)
    terms, row0 = [], 0
    for li, w, off in weights:
        kt = w.shape[0]
        assert lhs[li].shape == (m, kt) and off % LANES == 0
        terms.append((li, row0, kt, off))
        row0 += kt
    ktot = row0
    lhs_bytes = sum(2 * bm * a.shape[1] * a.dtype.itemsize for a in lhs)
    extra_bytes = sum(2 * bm * bn * a.dtype.itemsize for a, _, _ in extras)
    need = (ktot * bn * 6 + lhs_bytes + extra_bytes + 2 * bm * bn * jnp.dtype(out_dtype).itemsize
            + len(terms) * bm * bn * 4 + guest_bytes)
    limit = min(max(VMEM_LIMIT_BYTES, need + VMEM_TEMP_BYTES), VMEM_PHYSICAL_BYTES - (2 << 20))
    assert need + (4 << 20) <= limit, (name, need, limit)
    kern = functools.partial(_colmm_kernel, n_lhs=len(lhs), terms=tuple(terms), n_extra=len(extras),
                             epilogue=epilogue, ncols=n_out // bn, bn=bn,
                             guest=(len(g_args), len(g_out_specs), g_body))
    extra_spec = lambda rb, cb: pl.BlockSpec((bm, bn), lambda j, i: (rb + i, cb + j))
    outs = pl.pallas_call(
        kern,
        grid=(n_out // bn, m // bm),
        in_specs=([pl.BlockSpec((bm, a.shape[1]), lambda j, i: (i, 0)) for a in lhs]
                  + [pl.BlockSpec(memory_space=pl.ANY) for _ in terms]
                  + [extra_spec(rb, cb) for _, rb, cb in extras] + list(g_in_specs)),
        out_specs=[pl.BlockSpec((bm, bn), lambda j, i: (i, j))] + list(g_out_specs),
        out_shape=[jax.ShapeDtypeStruct((m, n_out), out_dtype)] + list(g_out_shape),
        scratch_shapes=[pltpu.VMEM((ktot, bn), F32), pltpu.VMEM((ktot, bn), BF16),
                        pltpu.SemaphoreType.DMA((len(terms),))],
        compiler_params=_cparams("arbitrary", "arbitrary", vmem_limit_bytes=limit),
        name=name,
    )(*lhs, *[w for _, w, _ in weights], *[a for a, _, _ in extras], *g_args)
    return outs if guest else outs[0]


def _rotate(x, cos, sin_next, sin_prev):
    return x * cos + pltpu.roll(x, LANES - 1, 1) * sin_next + pltpu.roll(x, 1, 1) * sin_prev


def _ret_block(q, k, v, g, cos, sin_next, sin_prev, mask, xi, zeta, gc, states, slen):
    nseq = len(states)
    rows = q.shape[0]
    qr = _rotate(q, cos, sin_next, sin_prev)
    kr = _rotate(k, cos, sin_next, sin_prev) * (RET_DK ** -0.5)
    qb = qr.astype(BF16)
    kb = kr.astype(BF16)
    sc = lax.dot_general(qb, kb, (((1,), (1,)), ((), ())), preferred_element_type=F32) * mask
    o = jnp.dot(sc.astype(BF16), v, preferred_element_type=F32)
    qx = qr * xi
    kzt = (kr * zeta).T.astype(BF16)
    new_states = []
    if nseq == 1:
        s = states[0]
        o = o + jnp.dot(qx.astype(BF16), s.astype(BF16), preferred_element_type=F32)
        new_states.append(gc * s + jnp.dot(kzt, v, preferred_element_type=F32))
    else:
        pair = 2 * slen
        assert pair == 2 * SUBLANES and nseq % 2 == 0
        row_in_pair = lax.broadcasted_iota(jnp.int32, (pair, RET_DV), 0)
        row_seq = lax.broadcasted_iota(jnp.int32, (rows, RET_DV), 0) // slen
        parts = []
        for m in range(nseq // 2):
            qpair = qx[m * pair:(m + 1) * pair].astype(BF16)
            o0 = jnp.dot(qpair, states[2 * m].astype(BF16), preferred_element_type=F32)
            o1 = jnp.dot(qpair, states[2 * m + 1].astype(BF16), preferred_element_type=F32)
            parts.append(jnp.where(row_in_pair < slen, o0, o1))
        o = o + jnp.concatenate(parts, axis=0)
        vf = v.astype(F32)
        for n in range(nseq):
            vn = jnp.where(row_seq == n, vf, 0.0).astype(BF16)
            new_states.append(gc * states[n] + jnp.dot(kzt, vn, preferred_element_type=F32))
    mu = jnp.mean(o, axis=-1, keepdims=True)
    d = o - mu
    var = jnp.mean(d * d, axis=-1, keepdims=True)
    on = d * lax.rsqrt(var + GN_EPS)
    out = (on * (g * jax.nn.sigmoid(g))).astype(BF16)
    return out, new_states


def _ret_prompt_kernel(gc_ref, q_ref, k_ref, v_ref, g_ref, cos_ref, sn_ref, sp_ref, mask_ref, xi_ref, zeta_ref,
                       o_ref, sfin_ref, s_scr, *, chunk, nchunks, hb):
    head0 = pl.program_id(1) * hb
    s_scr[...] = jnp.zeros_like(s_scr)

    def body(c, carry):
        rows = pl.ds(pl.multiple_of(c * chunk, chunk), chunk)
        cos, sn, sp = cos_ref[rows, :], sn_ref[rows, :], sp_ref[rows, :]
        for hh in range(hb):
            qc = slice(hh * RET_DK, (hh + 1) * RET_DK)
            vc = slice(hh * RET_DV, (hh + 1) * RET_DV)
            out, (s_new,) = _ret_block(
                q_ref[rows, qc].astype(F32), k_ref[rows, qc].astype(F32), v_ref[rows, vc],
                g_ref[rows, vc].astype(F32), cos, sn, sp, mask_ref[hh], xi_ref[hh], zeta_ref[hh],
                gc_ref[head0 + hh], [s_scr[hh]], chunk)
            o_ref[rows, vc] = out
            s_scr[hh] = s_new
        return carry

    lax.fori_loop(0, nchunks, body, 0, unroll=2)
    sfin_ref[...] = s_scr[...]


def _ret_sample_body(head, gc_ref, q_ref, k_ref, v_ref, g_ref, cos_ref, sn_ref, sp_ref, mask_ref, xi_ref, zeta_ref,
                     s0_ref, o_ref, sfin_ref, *, slen, nseq):
    gc = gc_ref[head]
    states = [s0_ref[n] for n in range(nseq)]
    out, new_states = _ret_block(
        q_ref[...].astype(F32), k_ref[...].astype(F32), v_ref[...], g_ref[...].astype(F32),
        cos_ref[...], sn_ref[...], sp_ref[...], mask_ref[...], xi_ref[...], zeta_ref[...], gc, states, slen)
    o_ref[...] = out
    for n in range(nseq):
        sfin_ref[n] = new_states[n]


def _rope_tables(pos):
    half = RET_DK // 2
    inv = 1.0 / (ROPE_BASE ** (np.arange(half, dtype=np.float64) / half))
    ang = np.asarray(pos, np.float64)[:, None] * inv[None, :]
    cos = np.repeat(np.cos(ang), 2, axis=1)
    sin = np.repeat(np.sin(ang), 2, axis=1)
    even = (np.arange(RET_DK) % 2) == 0
    return (np.asarray(cos, np.float32), np.asarray(np.where(even, -sin, 0.0), np.float32),
            np.asarray(np.where(even, 0.0, sin), np.float32))


def _decay_tables(slen, nseq):
    lg = np.log1p(-np.exp2(-5.0 - np.arange(RET_HEADS, dtype=np.float64)))
    idx = np.arange(slen, dtype=np.float64)
    rel = idx[:, None] - idx[None, :]
    inner = np.where(rel[None] >= 0, np.exp(lg[:, None, None] * np.maximum(rel, 0.0)[None]), 0.0)
    xi = np.exp(lg[:, None] * (idx + 1.0))
    zeta = np.exp(lg[:, None] * (slen - 1.0 - idx))
    gc = np.exp(lg * slen)
    mask = np.einsum("nm,hij->hnimj", np.eye(nseq), inner).reshape(RET_HEADS, nseq * slen, nseq * slen)
    rows = nseq * slen
    xi_t = np.broadcast_to(np.tile(xi, (1, nseq))[:, :, None], (RET_HEADS, rows, RET_DK))
    zeta_t = np.broadcast_to(np.tile(zeta, (1, nseq))[:, :, None], (RET_HEADS, rows, RET_DK))
    f32 = lambda t: np.ascontiguousarray(t, dtype=np.float32)
    return f32(mask), f32(xi_t), f32(zeta_t), f32(gc)


def _retention_prompt(z, nbatch, seqlen):
    chunk = RET_CHUNK
    hb = 4
    cos, sn, sp = _rope_tables(np.arange(seqlen))
    mask, xi, zeta, gc = _decay_tables(chunk, 1)
    tab = lambda: pl.BlockSpec((seqlen, RET_DK), lambda b, h: (0, 0))
    head_tab = lambda w: pl.BlockSpec((hb, chunk, w), lambda b, h: (h, 0, 0))
    kern = functools.partial(_ret_prompt_kernel, chunk=chunk, nchunks=seqlen // chunk, hb=hb)
    qk_w, v_w = hb * RET_DK, hb * RET_DV
    return pl.pallas_call(
        kern,
        grid=(nbatch, RET_HEADS // hb),
        in_specs=[
            pl.BlockSpec(memory_space=pltpu.SMEM),
            pl.BlockSpec((seqlen, qk_w), lambda b, h: (b, Z_Q // qk_w + h)),
            pl.BlockSpec((seqlen, qk_w), lambda b, h: (b, Z_K // qk_w + h)),
            pl.BlockSpec((seqlen, v_w), lambda b, h: (b, Z_V // v_w + h)),
            pl.BlockSpec((seqlen, v_w), lambda b, h: (b, Z_GRET // v_w + h)),
            tab(), tab(), tab(),
            head_tab(chunk), head_tab(RET_DK), head_tab(RET_DK),
        ],
        out_specs=[
            pl.BlockSpec((seqlen, v_w), lambda b, h: (b, h)),
            pl.BlockSpec((None, hb, RET_DK, RET_DV), lambda b, h: (b, h, 0, 0)),
        ],
        out_shape=[
            jax.ShapeDtypeStruct((nbatch * seqlen, RET_V), BF16),
            jax.ShapeDtypeStruct((nbatch, RET_HEADS, RET_DK, RET_DV), F32),
        ],
        scratch_shapes=[pltpu.VMEM((hb, RET_DK, RET_DV), F32)],
        compiler_params=_cparams("parallel", "parallel"),
        name="retention_prompt",
    )(gc, z, z, z, z, cos, sn, sp, mask, xi, zeta)


def _retention_sample_operands(z, s0, row_off, nbatch, slen, pos0, nseq, block_of):
    rows = nseq * slen
    cos, sn, sp = (np.tile(t, (nseq, 1)) for t in _rope_tables(pos0 + np.arange(slen)))
    mask, xi, zeta, gc = _decay_tables(slen, nseq)
    rb = row_off // rows

    def at(fn):
        return lambda *ids: fn(*block_of(*ids))

    tab = lambda: pl.BlockSpec((rows, RET_DK), lambda *ids: (0, 0))
    head_tab = lambda w: pl.BlockSpec((None, rows, w), at(lambda i, h: (h, 0, 0)))
    state = lambda: pl.BlockSpec((nseq, None, RET_DK, RET_DV), at(lambda i, h: (i, h, 0, 0)))
    in_specs = [
        pl.BlockSpec(memory_space=pltpu.SMEM),
        pl.BlockSpec((rows, RET_DK), at(lambda i, h: (rb + i, Z_Q // RET_DK + h))),
        pl.BlockSpec((rows, RET_DK), at(lambda i, h: (rb + i, Z_K // RET_DK + h))),
        pl.BlockSpec((rows, RET_DV), at(lambda i, h: (rb + i, Z_V // RET_DV + h))),
        pl.BlockSpec((rows, RET_DV), at(lambda i, h: (rb + i, Z_GRET // RET_DV + h))),
        tab(), tab(), tab(),
        head_tab(rows), head_tab(RET_DK), head_tab(RET_DK),
        state(),
    ]
    out_specs = [pl.BlockSpec((rows, RET_DV), at(lambda i, h: (i, h))), state()]
    out_shape = [jax.ShapeDtypeStruct((nbatch * slen, RET_V), BF16),
                 jax.ShapeDtypeStruct((nbatch, RET_HEADS, RET_DK, RET_DV), F32)]
    return (gc, z, z, z, z, cos, sn, sp, mask, xi, zeta, s0), in_specs, out_specs, out_shape


def _s5_prep_kernel(are_ref, aim_ref, dt_ref, are_w_ref, aim_w_ref, dt_w_ref, bre_ref, bim_ref, cim_ref,
                    tabs_ref, bbre_ref, bbim_ref, ncim_ref):
    def abar(ar, ai, dt):
        mag = jnp.exp(dt * ar)
        return mag * jnp.cos(dt * ai), mag * jnp.sin(dt * ai)

    def powers(ar, ai):
        out = [(ar, ai)]
        for _ in range(1, SUBLANES):
            pr, pi = out[-1]
            out.append((pr * ar - pi * ai, pr * ai + pi * ar))
        return out

    pw = powers(*abar(are_ref[...], aim_ref[...], dt_ref[...]))
    qw = powers(*pw[-1])
    k_idx = lax.broadcasted_iota(jnp.int32, (SUBLANES, S5_BLK_STATE), 0)
    for j in range(S5_NBLK):
        every_row = lambda v: jnp.broadcast_to(v[j:j + 1, :], (SUBLANES, S5_BLK_STATE))
        for c in range(2):
            tabs_ref[j, T_A + c] = every_row(pw[0][c])
            for i in range(SUBLANES):
                tabs_ref[j, T_POW + 2 * i + c] = every_row(pw[i][c])
            for l, lvl in enumerate(SCAN_LEVELS):
                tabs_ref[j, T_LVL + 2 * l + c] = jnp.where(k_idx >= lvl, every_row(qw[lvl - 1][c]), 0.0)
            carry = every_row(qw[0][c])
            for k in range(1, SUBLANES):
                carry = jnp.where(k_idx == k, every_row(qw[k][c]), carry)
            tabs_ref[j, T_CARRY + c] = carry

    a_r, a_i = are_w_ref[...], aim_w_ref[...]
    w_r, w_i = abar(a_r, a_i, dt_w_ref[...])
    den = a_r * a_r + a_i * a_i
    x_re = w_r - 1.0
    f_re = (x_re * a_r + w_i * a_i) / den
    f_im = (w_i * a_r - x_re * a_i) / den
    br, bi = bre_ref[...], bim_ref[...]
    bbre_ref[...] = f_re * br - f_im * bi
    bbim_ref[...] = f_re * bi + f_im * br
    ncim_ref[...] = -cim_ref[...]


def _s5_prepare(a_re, a_im, log_step, b_re, b_im, c_re, c_im):
    g, n, p = S5_GROUPS, S5_STATE, S5_GROUP
    dt = jnp.broadcast_to(jnp.exp(log_step.astype(F32))[:, None], (g, n))
    wide = lambda t: jnp.repeat(t, p, axis=1)
    vm = lambda: pl.BlockSpec(memory_space=pltpu.VMEM)
    nb, bg = S5_NBLK, S5_BLK_GROUPS
    per_blk = lambda t: t.reshape(nb, S5_BLK_STATE)
    tabs, bbre, bbim, ncim = pl.pallas_call(
        _s5_prep_kernel,
        in_specs=[vm() for _ in range(9)],
        out_specs=[vm() for _ in range(4)],
        out_shape=[
            jax.ShapeDtypeStruct((nb, N_TABS, SUBLANES, S5_BLK_STATE), F32),
            jax.ShapeDtypeStruct((g, n * p), F32),
            jax.ShapeDtypeStruct((g, n * p), F32),
            jax.ShapeDtypeStruct((g, p * n), F32),
        ],
        name="s5_discretize",
    )(per_blk(a_re.astype(F32)), per_blk(a_im.astype(F32)), per_blk(dt),
      wide(a_re.astype(F32)), wide(a_im.astype(F32)), wide(dt),
      b_re.astype(F32).reshape(g, n * p), b_im.astype(F32).reshape(g, n * p), c_im.astype(F32).reshape(g, p * n))

    eye = jnp.eye(bg, dtype=bool)

    def in_blockdiag(t):
        t = t.reshape(nb, bg, n, p).transpose(0, 1, 3, 2)
        return jnp.where(eye[None, :, None, :, None], t[:, :, :, None, :], 0.0).reshape(nb, bg * p, bg * n)

    def out_blockdiag(t):
        t = t.reshape(nb, bg, p, n).transpose(0, 1, 3, 2)
        return jnp.where(eye[None, :, None, :, None], t[:, :, :, None, :], 0.0).reshape(nb, bg * n, bg * p)

    w_in = jnp.concatenate([in_blockdiag(bbre), in_blockdiag(bbim)], axis=-1).astype(BF16)
    w_out = jnp.concatenate([out_blockdiag(c_re.astype(F32).reshape(g, p * n)), out_blockdiag(ncim)],
                            axis=1).astype(BF16)
    return w_in, w_out, tabs


S5_SUB = SUBLANES * SUBLANES


def _segment_scan(er, ei, tabs_ref, cr, ci):
    for l, lvl in enumerate(SCAN_LEVELS):
        pr = tabs_ref[T_LVL + 2 * l]
        pi = tabs_ref[T_LVL + 2 * l + 1]
        sr = pltpu.roll(er, lvl, 0)
        si = pltpu.roll(ei, lvl, 0)
        er, ei = er + pr * sr - pi * si, ei + pr * si + pi * sr
    rr = tabs_ref[T_CARRY]
    ri = tabs_ref[T_CARRY + 1]
    crb = jnp.broadcast_to(cr, er.shape)
    cib = jnp.broadcast_to(ci, ei.shape)
    return er + rr * crb - ri * cib, ei + rr * cib + ri * crb


def _transpose_tiles(ref, nsub):
    return jnp.concatenate([ref[pl.ds(S5_SUB * m + i, SUBLANES, stride=SUBLANES), :]
                            for m in range(nsub) for i in range(SUBLANES)], axis=0)


def _s5_stream(u_ref, win_ref, wout_ref, d_ref, tabs_ref, uf_scr, bu_scr, h_scr, y_scr, init, long_seq):
    tc = u_ref.shape[0]
    nsub = tc // S5_SUB
    ns = S5_BLK_STATE
    uf = u_ref[...].astype(F32)
    uf_scr[...] = uf
    up = _transpose_tiles(uf_scr, nsub).astype(BF16)
    bu_scr[...] = jnp.dot(up, win_ref[...], preferred_element_type=F32)
    ar = tabs_ref[T_A]
    ai = tabs_ref[T_A + 1]
    finals = []
    for m in range(nsub):
        tile = lambda i: slice(S5_SUB * m + SUBLANES * i, S5_SUB * m + SUBLANES * (i + 1))
        if long_seq:
            hr = hi = None
        else:
            hr = init[0][SUBLANES * m:SUBLANES * (m + 1), :]
            hi = init[1][SUBLANES * m:SUBLANES * (m + 1), :]
        for i in range(SUBLANES):
            xr = bu_scr[tile(i), :ns]
            xi = bu_scr[tile(i), ns:]
            if hr is None:
                hr, hi = xr, xi
            else:
                hr, hi = ar * hr - ai * hi + xr, ar * hi + ai * hr + xi
            h_scr[tile(i), :ns] = hr
            h_scr[tile(i), ns:] = hi
        if long_seq:
            cr, ci = init
            fr, fi = _segment_scan(hr, hi, tabs_ref, cr, ci)
            first = lax.broadcasted_iota(jnp.int32, fr.shape, 0) == 0
            sr = jnp.where(first, jnp.broadcast_to(cr, fr.shape), pltpu.roll(fr, 1, 0))
            si = jnp.where(first, jnp.broadcast_to(ci, fi.shape), pltpu.roll(fi, 1, 0))
            init = (fr[SUBLANES - 1:, :], fi[SUBLANES - 1:, :])
            for i in range(SUBLANES):
                pr = tabs_ref[T_POW + 2 * i]
                pi = tabs_ref[T_POW + 2 * i + 1]
                h_scr[tile(i), :ns] = h_scr[tile(i), :ns] + pr * sr - pi * si
                h_scr[tile(i), ns:] = h_scr[tile(i), ns:] + pr * si + pi * sr
        else:
            finals.append((hr, hi))
    y_scr[...] = jnp.dot(h_scr[...].astype(BF16), wout_ref[...], preferred_element_type=F32)
    y = _transpose_tiles(y_scr, nsub) + d_ref[...] * uf
    return jax.nn.gelu(y).astype(BF16), (init if long_seq else finals)


N_RET_SAMPLE_IN = 12


def _s5_prompt_kernel(*refs, nb, nt, slen, nseq):
    u_refs = refs[:nb]
    win_ref, wout_ref, d_ref, tabs_ref = refs[nb:nb + 4]
    ret_in = refs[nb + 4:nb + 4 + N_RET_SAMPLE_IN]
    mem_ref, wkv_ref = refs[nb + 4 + N_RET_SAMPLE_IN:nb + 6 + N_RET_SAMPLE_IN]
    (gl_ref, hre_ref, him_ref, o_ret_ref, sfin_ref, kv_ref,
     uf_scr, bu_scr, h_scr, y_scr, carry_scr, memb_scr) = refs[nb + 6 + N_RET_SAMPLE_IN:]
    ns = S5_BLK_STATE

    head = (pl.program_id(0) * nt + pl.program_id(2)) % RET_HEADS
    _ret_sample_body(head, *ret_in, o_ret_ref, sfin_ref, slen=slen, nseq=nseq)

    @pl.when(pl.program_id(2) == 0)
    def _():
        @pl.when(pl.program_id(0) == 0)
        def _():
            memb_scr[...] = mem_ref[...].astype(BF16)

        kv_ref[...] = jnp.dot(memb_scr[...], wkv_ref[...].astype(BF16), preferred_element_type=F32)

    @pl.when(pl.program_id(2) == 0)
    def _():
        carry_scr[...] = jnp.zeros_like(carry_scr)

    for s in range(nb):
        init = (carry_scr[s, 0:1, :ns], carry_scr[s, 0:1, ns:])
        out, (cr, ci) = _s5_stream(u_refs[s], win_ref, wout_ref, d_ref, tabs_ref, uf_scr.at[s], bu_scr.at[s],
                                   h_scr.at[s], y_scr.at[s], init, True)
        gl_ref[s] = out
        carry_scr[s, 0:1, :ns] = cr
        carry_scr[s, 0:1, ns:] = ci
        hre_ref[s] = cr
        him_ref[s] = ci


def _s5_sample_kernel(u_ref, win_ref, wout_ref, d_ref, tabs_ref, h0re_ref, h0im_ref, gl_ref, hre_ref, him_ref,
                      uf_scr, bu_scr, h_scr, y_scr):
    out, finals = _s5_stream(u_ref, win_ref, wout_ref, d_ref, tabs_ref, uf_scr, bu_scr, h_scr, y_scr,
                             (h0re_ref, h0im_ref), False)
    gl_ref[...] = out
    for m, (hr, hi) in enumerate(finals):
        hre_ref[SUBLANES * m:SUBLANES * (m + 1), :] = hr
        him_ref[SUBLANES * m:SUBLANES * (m + 1), :] = hi


def _s5_weight_specs():
    nst = S5_BLK_STATE
    return [
        pl.BlockSpec((None, LANES, 2 * nst), lambda j, *_: (j, 0, 0)),
        pl.BlockSpec((None, 2 * nst, LANES), lambda j, *_: (j, 0, 0)),
        pl.BlockSpec((1, LANES), lambda j, *_: (0, j)),
        pl.BlockSpec((None, N_TABS, SUBLANES, nst), lambda j, *_: (j, 0, 0, 0)),
    ]


def _s5_scratch(lead, tc):
    nst = S5_BLK_STATE
    return [pltpu.VMEM(lead + (tc, LANES), F32), pltpu.VMEM(lead + (tc, 2 * nst), F32),
            pltpu.VMEM(lead + (tc, 2 * nst), F32), pltpu.VMEM(lead + (tc, LANES), F32)]


def _s5_prompt_ret_sample(z, w_in, w_out, d_skip, tabs, nbatch, seqlen, s0, row_off, nbatch_s, slen, pos0,
                          mem, w_kv):
    tc = 256
    nb = nbatch
    nt = seqlen // tc
    nst = S5_BLK_STATE
    nseq = 16
    steps = S5_NBLK * nt
    assert steps == (nbatch_s // nseq) * RET_HEADS
    ret_args, ret_in_specs, ret_out_specs, ret_out_shape = _retention_sample_operands(
        z, s0, row_off, nbatch_s, slen, pos0, nseq, lambda j, bb, t: divmod(j * nt + t, RET_HEADS))
    assert len(ret_args) == N_RET_SAMPLE_IN
    kern = functools.partial(_s5_prompt_kernel, nb=nb, nt=nt, slen=slen, nseq=nseq)
    u_spec = lambda s: pl.BlockSpec((tc, LANES), lambda j, bb, t: (s * nt + t, Z_U // LANES + j))
    state_spec = lambda: pl.BlockSpec((nb, 1, nst), lambda j, bb, t: (0, 0, j))
    mrows, kdim = mem.shape
    kv_bn = w_kv.shape[1] // S5_NBLK
    gl, hre, him, o_ret, sfin, kv = pl.pallas_call(
        kern,
        grid=(S5_NBLK, 1, nt),
        in_specs=([u_spec(s) for s in range(nb)] + _s5_weight_specs() + ret_in_specs + [
            pl.BlockSpec((mrows, kdim), lambda j, bb, t: (0, 0), pipeline_mode=pl.Buffered(1)),
            pl.BlockSpec((kdim, kv_bn), lambda j, bb, t: (0, j)),
        ]),
        out_specs=([pl.BlockSpec((nb, tc, LANES), lambda j, bb, t: (0, t, j)), state_spec(), state_spec()]
                   + ret_out_specs + [pl.BlockSpec((mrows, kv_bn), lambda j, bb, t: (0, j))]),
        out_shape=[
            jax.ShapeDtypeStruct((nbatch, seqlen, S5_WIDTH), BF16),
            jax.ShapeDtypeStruct((nbatch, 1, S5_GROUPS * S5_STATE), F32),
            jax.ShapeDtypeStruct((nbatch, 1, S5_GROUPS * S5_STATE), F32),
        ] + ret_out_shape + [jax.ShapeDtypeStruct((mrows, w_kv.shape[1]), F32)],
        scratch_shapes=(_s5_scratch((nb,), tc) + [pltpu.VMEM((nb, SUBLANES, 2 * nst), F32)]
                        + [pltpu.VMEM((mrows, kdim), BF16)]),
        compiler_params=_cparams("arbitrary", "arbitrary", "arbitrary"),
        name="s5_prompt_ret_sample",
    )(*([z] * nb), w_in, w_out, d_skip, tabs, *ret_args, mem, w_kv)
    return (gl.reshape(nbatch * seqlen, S5_WIDTH), hre.reshape(nbatch, S5_GROUPS, S5_STATE),
            him.reshape(nbatch, S5_GROUPS, S5_STATE), o_ret, sfin, kv)


def _s5_sample(z, w_in, w_out, d_skip, tabs, h0_re, h0_im, row_off, nbatch, slen):
    assert slen == SUBLANES
    tc = 256
    nseq = tc // slen
    rb = row_off // tc
    nst = S5_BLK_STATE
    state_spec = lambda: pl.BlockSpec((nseq, nst), lambda j, i: (i, j))
    gl, hre, him = pl.pallas_call(
        _s5_sample_kernel,
        grid=(S5_NBLK, nbatch // nseq),
        in_specs=([pl.BlockSpec((tc, LANES), lambda j, i: (rb + i, Z_U // LANES + j))] + _s5_weight_specs()
                  + [state_spec(), state_spec()]),
        out_specs=[pl.BlockSpec((tc, LANES), lambda j, i: (i, j)), state_spec(), state_spec()],
        out_shape=[
            jax.ShapeDtypeStruct((nbatch * slen, S5_WIDTH), BF16),
            jax.ShapeDtypeStruct((nbatch, S5_GROUPS * S5_STATE), F32),
            jax.ShapeDtypeStruct((nbatch, S5_GROUPS * S5_STATE), F32),
        ],
        scratch_shapes=_s5_scratch((), tc),
        compiler_params=_cparams("parallel", "parallel"),
        name="s5_sample",
    )(z, w_in, w_out, d_skip, tabs, h0_re.reshape(nbatch, -1).astype(F32), h0_im.reshape(nbatch, -1).astype(F32))
    return gl, hre.reshape(nbatch, S5_GROUPS, S5_STATE), him.reshape(nbatch, S5_GROUPS, S5_STATE)


def _xattn_kernel(q_ref, g_ref, k_ref, v_ref, o_ref, *, q_per_seq, k_per_seq):
    tq = q_ref.shape[0]
    nk = k_ref.shape[0]
    kb = k_ref[...].astype(BF16)
    vb = v_ref[...].astype(BF16)
    if tq // q_per_seq > 1:
        same = (lax.broadcasted_iota(jnp.int32, (tq, nk), 0) // q_per_seq
                == lax.broadcasted_iota(jnp.int32, (tq, nk), 1) // k_per_seq)
    else:
        same = None
    for h in range(X_HEADS):
        cols = slice(h * X_HD, (h + 1) * X_HD)
        s = lax.dot_general(q_ref[:, cols], kb[:, cols], (((1,), (1,)), ((), ())),
                            preferred_element_type=F32) * (X_HD ** -0.5)
        if same is not None:
            s = jnp.where(same, s, -jnp.inf)
        e = jnp.exp(s - jnp.max(s, axis=-1, keepdims=True))
        oh = jnp.dot(e.astype(BF16), vb[:, cols], preferred_element_type=F32) / jnp.sum(e, axis=-1, keepdims=True)
        g = g_ref[:, cols].astype(F32)
        o_ref[:, cols] = (oh * (g * jax.nn.sigmoid(g))).astype(BF16)


def _xattn(z, mk, mv, *, row_off, nrows, tq, q_per_seq, seqs_per_step, kv_col_blk, name):
    nk = seqs_per_step * MEM_LEN
    steps_per_kv = (seqs_per_step * q_per_seq) // tq if tq < seqs_per_step * q_per_seq else 1
    rb = row_off // tq
    kern = functools.partial(_xattn_kernel, q_per_seq=q_per_seq, k_per_seq=MEM_LEN)
    kmap = lambda cb: (lambda i: (i // steps_per_kv, cb))
    return pl.pallas_call(
        kern,
        grid=(nrows // tq,),
        in_specs=[
            pl.BlockSpec((tq, X_WIDTH), lambda i: (rb + i, Z_QX // X_WIDTH)),
            pl.BlockSpec((tq, X_WIDTH), lambda i: (rb + i, Z_GX // X_WIDTH)),
            pl.BlockSpec((nk, X_WIDTH), kmap(kv_col_blk[0])),
            pl.BlockSpec((nk, X_WIDTH), kmap(kv_col_blk[1])),
        ],
        out_specs=pl.BlockSpec((tq, X_WIDTH), lambda i: (i, 0)),
        out_shape=jax.ShapeDtypeStruct((nrows, X_WIDTH), BF16),
        compiler_params=_cparams("parallel"),
        name=name,
    )(z, z, mk, mv)


def _xattn_sample_guest(z, ck, cv, *, row_off, nrows, q_per_seq, ncols_host, nrows_host):
    seqs = 2
    tq = seqs * q_per_seq
    nk = seqs * MEM_LEN
    assert ncols_host * nrows_host == nrows // tq
    rb = row_off // tq
    blk = lambda j, i: j * nrows_host + i
    in_specs = [
        pl.BlockSpec((tq, X_WIDTH), lambda j, i: (rb + blk(j, i), Z_QX // X_WIDTH)),
        pl.BlockSpec((tq, X_WIDTH), lambda j, i: (rb + blk(j, i), Z_GX // X_WIDTH)),
        pl.BlockSpec((nk, X_WIDTH), lambda j, i: (blk(j, i), 0)),
        pl.BlockSpec((nk, X_WIDTH), lambda j, i: (blk(j, i), 0)),
    ]
    out_specs = [pl.BlockSpec((tq, X_WIDTH), lambda j, i: (blk(j, i), 0))]
    out_shape = [jax.ShapeDtypeStruct((nrows, X_WIDTH), BF16)]
    body = functools.partial(_xattn_kernel, q_per_seq=q_per_seq, k_per_seq=MEM_LEN)
    vmem = 2 * 2 * nk * X_WIDTH * 4 + 2 * nk * X_WIDTH * 2
    return ((z, z, ck, cv), in_specs, out_specs, out_shape, body), vmem


def _cast_epilogue(prods, extras):
    return prods[0]


def _glu_epilogue(prods, extras):
    a, b = prods
    g = extras[0].astype(F32)
    return a * jax.nn.sigmoid(b) * (g * jax.nn.sigmoid(g))


def _glu(gl, w_glu, z, row_off, *, bm, bn, guest=None, guest_bytes=0):
    return _colmm([gl], [(0, w_glu, 0), (0, w_glu, S5_WIDTH)], [(z, row_off // bm, Z_GS5 // bn)], _glu_epilogue,
                  n_out=S5_WIDTH, bm=bm, bn=bn, out_dtype=BF16, name="glu", guest=guest, guest_bytes=guest_bytes)


def _merge_epilogue(prods, extras):
    out = jax.nn.sigmoid(extras[0].astype(F32)) * prods[0]
    for p, m in zip(prods[1:], extras[1:]):
        out = out + jax.nn.sigmoid(m.astype(F32)) * p
    return out


def _merge(o_a, o_b, o_c, w_a, w_b, w_c, z, row_off, *, bm, bn):
    rb = row_off // bm
    return _colmm([o_a, o_b, o_c], [(0, w_a, 0), (1, w_b, 0), (2, w_c, 0)],
                  [(z, rb, Z_MA // bn), (z, rb, Z_MB // bn), (z, rb, Z_MC // bn)], _merge_epilogue,
                  n_out=D_MODEL, bm=bm, bn=bn, out_dtype=BF16, name="merge")


def _residual_epilogue(prods, extras):
    return DN_ALPHA * extras[0] + prods[0]


def _layer_norm_kernel(p_ref, g_ref, b_ref, o_ref):
    x = p_ref[...]
    mu = jnp.mean(x, axis=-1, keepdims=True)
    d = x - mu
    var = jnp.mean(d * d, axis=-1, keepdims=True)
    o_ref[...] = d * lax.rsqrt(var + LN_EPS) * g_ref[...] + b_ref[...]


def _layer_norm_operands(pre, ln_g, ln_b, bm_ln, block_of):
    m, d = pre.shape
    row = lambda: pl.BlockSpec((1, d), lambda *ids: (0, 0))
    blk = lambda: pl.BlockSpec((bm_ln, d), lambda *ids: (block_of(*ids), 0))
    return ((pre, ln_g.reshape(1, d).astype(F32), ln_b.reshape(1, d).astype(F32)), [blk(), row(), row()], [blk()],
            [jax.ShapeDtypeStruct((m, d), F32)])


def _layer_norm(pre, ln_g, ln_b, *, bm_ln):
    args, in_specs, out_specs, out_shape = _layer_norm_operands(pre, ln_g, ln_b, bm_ln, lambda i: i)
    return pl.pallas_call(
        _layer_norm_kernel,
        grid=(pre.shape[0] // bm_ln,),
        in_specs=in_specs,
        out_specs=out_specs[0],
        out_shape=out_shape[0],
        compiler_params=_cparams("parallel"),
        name="layer_norm",
    )(*args)


def _merge_out(z, row_off, o_ret, o_s5, o_x, x2d, w):
    merged = _merge(o_ret, o_s5, o_x, w["proj_a"], w["proj_b"], w["proj_c"], z, row_off, bm=512, bn=512)
    return _colmm([merged], [(0, w["out"], 0)], [(x2d, 0, 0)], _residual_epilogue, n_out=D_MODEL, bm=512, bn=1024,
                  out_dtype=F32, name="out_proj")


def kernel(x_prompt, x_sample, mem_prompt, state_ret, state_s5_re, state_s5_im, cache_mem_k, cache_mem_v, w_in, w_mem_kv, s5_a_re, s5_a_im, s5_log_step, s5_b_re, s5_b_im, s5_c_re, s5_c_im, s5_d, w_glu, w_proj_a, w_proj_b, w_proj_c, w_out, ln_g, ln_b):
    depth = w_in.shape[0]
    assert depth == 1
    l = 0
    n_p = BATCH * SEQ
    n_s = DEC_BATCH * DEC_SEQ
    xp2 = x_prompt.reshape(n_p, D_MODEL)
    xs2 = x_sample.reshape(n_s, D_MODEL)

    w = dict(glu=w_glu[l], proj_a=w_proj_a[l], proj_b=w_proj_b[l], proj_c=w_proj_c[l],
             out=w_out[l], ln_g=ln_g[l], ln_b=ln_b[l])

    xb = _concat_cast(xp2, xs2, bm=512, out_dtype=BF16)
    z = _colmm([xb], [(0, w_in[l], 0)], [], _cast_epilogue, n_out=IN_WIDTH, bm=1024, bn=1024, out_dtype=BF16,
               name="in_proj")

    mem2 = mem_prompt.reshape(BATCH * MEM_LEN, D_MODEL)
    s5_win, s5_wout, s5_tabs = _s5_prepare(s5_a_re[l], s5_a_im[l], s5_log_step[l], s5_b_re[l], s5_b_im[l],
                                           s5_c_re[l], s5_c_im[l])
    d_skip = s5_d[l].reshape(1, S5_WIDTH).astype(F32)

    o_ret_p, ret_p = _retention_prompt(z, BATCH, SEQ)
    gl_p, hre_p, him_p, o_ret_s, ret_s, kv = _s5_prompt_ret_sample(
        z, s5_win, s5_wout, d_skip, s5_tabs, BATCH, SEQ, state_ret[l], n_p, DEC_BATCH, DEC_SEQ, PAST_LEN,
        mem2, w_mem_kv[l])
    mk, mv = kv[:, :X_WIDTH], kv[:, X_WIDTH:]
    o_x_p = _xattn(z, kv, kv, row_off=0, nrows=n_p, tq=1024, q_per_seq=SEQ, seqs_per_step=1,
                   kv_col_blk=(0, 1), name="xattn_prompt")
    ck = cache_mem_k[l].reshape(DEC_BATCH * MEM_LEN, X_WIDTH)
    cv = cache_mem_v[l].reshape(DEC_BATCH * MEM_LEN, X_WIDTH)
    glu_bm, glu_bn = 512, 512
    guest, guest_bytes = _xattn_sample_guest(z, ck, cv, row_off=n_p, nrows=n_s, q_per_seq=DEC_SEQ,
                                             ncols_host=S5_WIDTH // glu_bn, nrows_host=n_p // glu_bm)
    o_s5_p, o_x_s = _glu(gl_p, w["glu"], z, 0, bm=glu_bm, bn=glu_bn, guest=guest, guest_bytes=guest_bytes)
    pre_p = _merge_out(z, 0, o_ret_p, o_s5_p, o_x_p, xp2, w)

    gl_s, hre_s, him_s = _s5_sample(z, s5_win, s5_wout, d_skip, s5_tabs, state_s5_re[l], state_s5_im[l],
                                    n_p, DEC_BATCH, DEC_SEQ)
    bm_ln = 256
    glu_s_bm, glu_s_bn = 256, 256
    glu_s_rows = n_s // glu_s_bm
    assert (S5_WIDTH // glu_s_bn) * glu_s_rows == n_p // bm_ln
    ln_args, ln_in, ln_out, ln_shape = _layer_norm_operands(pre_p, w["ln_g"], w["ln_b"], bm_ln,
                                                            lambda j, i: j * glu_s_rows + i)
    o_s5_s, y_p = _glu(gl_s, w["glu"], z, n_p, bm=glu_s_bm, bn=glu_s_bn,
                       guest=(ln_args, ln_in, ln_out, ln_shape, _layer_norm_kernel),
                       guest_bytes=6 * bm_ln * D_MODEL * 4)
    y_s = _layer_norm(_merge_out(z, n_p, o_ret_s, o_s5_s, o_x_s, xs2, w), w["ln_g"], w["ln_b"], bm_ln=bm_ln)

    return (y_p.reshape(BATCH, SEQ, D_MODEL), y_s.reshape(DEC_BATCH, DEC_SEQ, D_MODEL),
            ret_p[None], hre_p[None], him_p[None],
            mk.reshape(1, BATCH, MEM_LEN, X_HEADS, X_HD), mv.reshape(1, BATCH, MEM_LEN, X_HEADS, X_HD),
            ret_s[None], hre_s[None], him_s[None])
```

```python
import functools
import math

import jax
import jax.numpy as jnp
import numpy as np
from jax import lax
from jax.experimental import pallas as pl
from jax.experimental.pallas import tpu as pltpu

F32 = jnp.float32
BF16 = jnp.bfloat16

D_MODEL = 4096
BATCH = 4
SEQ = 2048
DEC_BATCH = 128
DEC_SEQ = 8
PAST_LEN = 16384

RET_HEADS = 16
RET_DK = 128
RET_DV = 256
RET_QK = RET_HEADS * RET_DK
RET_V = RET_HEADS * RET_DV
RET_CHUNK = 128
ROPE_BASE = 10000.0

S5_WIDTH = D_MODEL // 2
S5_GROUP = 16
S5_GROUPS = S5_WIDTH // S5_GROUP
S5_STATE = 64

X_HEADS = 4
X_WIDTH = D_MODEL // 2
X_HD = X_WIDTH // X_HEADS
MEM_LEN = 256

DN_ALPHA = 2.0 ** 0.25
LN_EPS = 1e-5
GN_EPS = 1e-5

IN_WIDTH = 2 * RET_QK + 2 * RET_V + 2 * S5_WIDTH + 2 * X_WIDTH + 3 * D_MODEL

Z_Q = 0
Z_K = Z_Q + RET_QK
Z_V = Z_K + RET_QK
Z_GRET = Z_V + RET_V
Z_U = Z_GRET + RET_V
Z_GS5 = Z_U + S5_WIDTH
Z_QX = Z_GS5 + S5_WIDTH
Z_GX = Z_QX + X_WIDTH
Z_MA = Z_GX + X_WIDTH
Z_MB = Z_MA + D_MODEL
Z_MC = Z_MB + D_MODEL

SUBLANES = 8
LANES = 128
VMEM_PHYSICAL_BYTES = 64 * 1024 * 1024
VMEM_LIMIT_BYTES = 56 * 1024 * 1024
VMEM_TEMP_BYTES = 12 * 1024 * 1024

S5_BLK_GROUPS = LANES // S5_GROUP
S5_BLK_STATE = S5_BLK_GROUPS * S5_STATE
S5_NBLK = S5_GROUPS // S5_BLK_GROUPS
SCAN_LEVELS = (1, 2, 4)
T_A = 0
T_POW = 2
T_LVL = T_POW + 2 * SUBLANES
T_CARRY = T_LVL + 2 * len(SCAN_LEVELS)
N_TABS = T_CARRY + 2


def _cparams(*sem, vmem_limit_bytes=VMEM_LIMIT_BYTES):
    return pltpu.CompilerParams(dimension_semantics=sem, vmem_limit_bytes=vmem_limit_bytes)


def _concat_cast_kernel(a_ref, b_ref, o_ref, *, na):
    i = pl.program_id(0)

    @pl.when(i < na)
    def _():
        o_ref[...] = a_ref[...].astype(o_ref.dtype)

    @pl.when(i >= na)
    def _():
        o_ref[...] = b_ref[...].astype(o_ref.dtype)


def _concat_cast(a, b, *, bm, out_dtype):
    k = a.shape[1]
    na, nb = a.shape[0] // bm, b.shape[0] // bm
    return pl.pallas_call(
        functools.partial(_concat_cast_kernel, na=na),
        grid=(na + nb,),
        in_specs=[
            pl.BlockSpec((bm, k), lambda i: (jnp.minimum(i, na - 1), 0)),
            pl.BlockSpec((bm, k), lambda i: (jnp.maximum(i - na, 0), 0), pipeline_mode=pl.Buffered(1)),
        ],
        out_specs=pl.BlockSpec((bm, k), lambda i: (i, 0)),
        out_shape=jax.ShapeDtypeStruct((a.shape[0] + b.shape[0], k), out_dtype),
        compiler_params=_cparams("arbitrary"),
        name="concat_cast",
    )(a, b)


def _colmm_kernel(*refs, n_lhs, terms, n_extra, epilogue, ncols, bn, guest):
    lhs = refs[:n_lhs]
    w_hbm = refs[n_lhs:n_lhs + len(terms)]
    extras = refs[n_lhs + len(terms):n_lhs + len(terms) + n_extra]
    rest = refs[n_lhs + len(terms) + n_extra:]
    n_gin, n_gout, guest_body = guest
    guest_in, o_ref, guest_out = rest[:n_gin], rest[n_gin], rest[n_gin + 1:n_gin + 1 + n_gout]
    stage, wb_scr, sem = rest[n_gin + 1 + n_gout:]
    j = pl.program_id(0)
    i = pl.program_id(1)
    if guest_body is not None:
        guest_body(*guest_in, *guest_out)

    def tile_copies(col):
        return [pltpu.make_async_copy(
            w_hbm[t].at[:, pl.ds(pl.multiple_of(off + col * bn, LANES), bn)],
            stage.at[pl.ds(row0, kt), :], sem.at[t]) for t, (_, row0, kt, off) in enumerate(terms)]

    @pl.when(i == 0)
    def _():
        @pl.when(j == 0)
        def _():
            for c in tile_copies(0):
                c.start(priority=1)

        for c in tile_copies(j):
            c.wait()
        wb_scr[...] = stage[...].astype(BF16)

        @pl.when(j + 1 < ncols)
        def _():
            for c in tile_copies(j + 1):
                c.start(priority=1)

    prods = [jnp.dot(lhs[li][...], wb_scr[row0:row0 + kt, :], preferred_element_type=F32)
             for li, row0, kt, _ in terms]
    o_ref[...] = epilogue(prods, [e[...] for e in extras]).astype(o_ref.dtype)


def _colmm(lhs, weights, extras, epilogue, *, n_out, bm, bn, out_dtype, name, guest=None, guest_bytes=0):
    m = lhs[0].shape[0]
    g_args, g_in_specs, g_out_specs, g_out_shape, g_body = guest or ((), [], [], [], None)
    terms, row0 = [], 0
    for li, w, off in weights:
        kt = w.shape[0]
        assert lhs[li].shape == (m, kt) and off % LANES == 0
        terms.append((li, row0, kt, off))
        row0 += kt
    ktot = row0
    lhs_bytes = sum(2 * bm * a.shape[1] * a.dtype.itemsize for a in lhs)
    extra_bytes = sum(2 * bm * bn * a.dtype.itemsize for a, _, _ in extras)
    need = (ktot * bn * 6 + lhs_bytes + extra_bytes + 2 * bm * bn * jnp.dtype(out_dtype).itemsize
            + len(terms) * bm * bn * 4 + guest_bytes)
    limit = min(max(VMEM_LIMIT_BYTES, need + VMEM_TEMP_BYTES), VMEM_PHYSICAL_BYTES - (2 << 20))
    assert need + (4 << 20) <= limit, (name, need, limit)
    kern = functools.partial(_colmm_kernel, n_lhs=len(lhs), terms=tuple(terms), n_extra=len(extras),
                             epilogue=epilogue, ncols=n_out // bn, bn=bn,
                             guest=(len(g_args), len(g_out_specs), g_body))
    extra_spec = lambda rb, cb: pl.BlockSpec((bm, bn), lambda j, i: (rb + i, cb + j))
    outs = pl.pallas_call(
        kern,
        grid=(n_out // bn, m // bm),
        in_specs=([pl.BlockSpec((bm, a.shape[1]), lambda j, i: (i, 0)) for a in lhs]
                  + [pl.BlockSpec(memory_space=pl.ANY) for _ in terms]
                  + [extra_spec(rb, cb) for _, rb, cb in extras] + list(g_in_specs)),
        out_specs=[pl.BlockSpec((bm, bn), lambda j, i: (i, j))] + list(g_out_specs),
        out_shape=[jax.ShapeDtypeStruct((m, n_out), out_dtype)] + list(g_out_shape),
        scratch_shapes=[pltpu.VMEM((ktot, bn), F32), pltpu.VMEM((ktot, bn), BF16),
                        pltpu.SemaphoreType.DMA((len(terms),))],
        compiler_params=_cparams("arbitrary", "arbitrary", vmem_limit_bytes=limit),
        name=name,
    )(*lhs, *[w for _, w, _ in weights], *[a for a, _, _ in extras], *g_args)
    return outs if guest else outs[0]


def _rotate(x, cos, sin_next, sin_prev):
    return x * cos + pltpu.roll(x, LANES - 1, 1) * sin_next + pltpu.roll(x, 1, 1) * sin_prev


def _ret_block(q, k, v, g, cos, sin_next, sin_prev, mask, xi, zeta, gc, states, slen):
    nseq = len(states)
    rows = q.shape[0]
    qr = _rotate(q, cos, sin_next, sin_prev)
    kr = _rotate(k, cos, sin_next, sin_prev) * (RET_DK ** -0.5)
    qb = qr.astype(BF16)
    kb = kr.astype(BF16)
    sc = lax.dot_general(qb, kb, (((1,), (1,)), ((), ())), preferred_element_type=F32) * mask
    o = jnp.dot(sc.astype(BF16), v, preferred_element_type=F32)
    qx = qr * xi
    kzt = (kr * zeta).T.astype(BF16)
    new_states = []
    if nseq == 1:
        s = states[0]
        o = o + jnp.dot(qx.astype(BF16), s.astype(BF16), preferred_element_type=F32)
        new_states.append(gc * s + jnp.dot(kzt, v, preferred_element_type=F32))
    else:
        pair = 2 * slen
        assert pair == 2 * SUBLANES and nseq % 2 == 0
        row_in_pair = lax.broadcasted_iota(jnp.int32, (pair, RET_DV), 0)
        row_seq = lax.broadcasted_iota(jnp.int32, (rows, RET_DV), 0) // slen
        parts = []
        for m in range(nseq // 2):
            qpair = qx[m * pair:(m + 1) * pair].astype(BF16)
            o0 = jnp.dot(qpair, states[2 * m].astype(BF16), preferred_element_type=F32)
            o1 = jnp.dot(qpair, states[2 * m + 1].astype(BF16), preferred_element_type=F32)
            parts.append(jnp.where(row_in_pair < slen, o0, o1))
        o = o + jnp.concatenate(parts, axis=0)
        vf = v.astype(F32)
        for n in range(nseq):
            vn = jnp.where(row_seq == n, vf, 0.0).astype(BF16)
            new_states.append(gc * states[n] + jnp.dot(kzt, vn, preferred_element_type=F32))
    mu = jnp.mean(o, axis=-1, keepdims=True)
    d = o - mu
    var = jnp.mean(d * d, axis=-1, keepdims=True)
    on = d * lax.rsqrt(var + GN_EPS)
    out = (on * (g * jax.nn.sigmoid(g))).astype(BF16)
    return out, new_states


def _ret_prompt_kernel(gc_ref, q_ref, k_ref, v_ref, g_ref, cos_ref, sn_ref, sp_ref, mask_ref, xi_ref, zeta_ref,
                       o_ref, sfin_ref, s_scr, *, chunk, nchunks, hb):
    head0 = pl.program_id(1) * hb
    s_scr[...] = jnp.zeros_like(s_scr)

    def body(c, carry):
        rows = pl.ds(pl.multiple_of(c * chunk, chunk), chunk)
        cos, sn, sp = cos_ref[rows, :], sn_ref[rows, :], sp_ref[rows, :]
        for hh in range(hb):
            qc = slice(hh * RET_DK, (hh + 1) * RET_DK)
            vc = slice(hh * RET_DV, (hh + 1) * RET_DV)
            out, (s_new,) = _ret_block(
                q_ref[rows, qc].astype(F32), k_ref[rows, qc].astype(F32), v_ref[rows, vc],
                g_ref[rows, vc].astype(F32), cos, sn, sp, mask_ref[hh], xi_ref[hh], zeta_ref[hh],
                gc_ref[head0 + hh], [s_scr[hh]], chunk)
            o_ref[rows, vc] = out
            s_scr[hh] = s_new
        return carry

    lax.fori_loop(0, nchunks, body, 0)
    sfin_ref[...] = s_scr[...]


def _ret_sample_body(head, gc_ref, q_ref, k_ref, v_ref, g_ref, cos_ref, sn_ref, sp_ref, mask_ref, xi_ref, zeta_ref,
                     s0_ref, o_ref, sfin_ref, *, slen, nseq):
    gc = gc_ref[head]
    states = [s0_ref[n] for n in range(nseq)]
    out, new_states = _ret_block(
        q_ref[...].astype(F32), k_ref[...].astype(F32), v_ref[...], g_ref[...].astype(F32),
        cos_ref[...], sn_ref[...], sp_ref[...], mask_ref[...], xi_ref[...], zeta_ref[...], gc, states, slen)
    o_ref[...] = out
    for n in range(nseq):
        sfin_ref[n] = new_states[n]


def _rope_tables(pos):
    half = RET_DK // 2
    inv = 1.0 / (ROPE_BASE ** (np.arange(half, dtype=np.float64) / half))
    ang = np.asarray(pos, np.float64)[:, None] * inv[None, :]
    cos = np.repeat(np.cos(ang), 2, axis=1)
    sin = np.repeat(np.sin(ang), 2, axis=1)
    even = (np.arange(RET_DK) % 2) == 0
    return (np.asarray(cos, np.float32), np.asarray(np.where(even, -sin, 0.0), np.float32),
            np.asarray(np.where(even, 0.0, sin), np.float32))


def _decay_tables(slen, nseq):
    lg = np.log1p(-np.exp2(-5.0 - np.arange(RET_HEADS, dtype=np.float64)))
    idx = np.arange(slen, dtype=np.float64)
    rel = idx[:, None] - idx[None, :]
    inner = np.where(rel[None] >= 0, np.exp(lg[:, None, None] * np.maximum(rel, 0.0)[None]), 0.0)
    xi = np.exp(lg[:, None] * (idx + 1.0))
    zeta = np.exp(lg[:, None] * (slen - 1.0 - idx))
    gc = np.exp(lg * slen)
    mask = np.einsum("nm,hij->hnimj", np.eye(nseq), inner).reshape(RET_HEADS, nseq * slen, nseq * slen)
    rows = nseq * slen
    xi_t = np.broadcast_to(np.tile(xi, (1, nseq))[:, :, None], (RET_HEADS, rows, RET_DK))
    zeta_t = np.broadcast_to(np.tile(zeta, (1, nseq))[:, :, None], (RET_HEADS, rows, RET_DK))
    f32 = lambda t: np.ascontiguousarray(t, dtype=np.float32)
    return f32(mask), f32(xi_t), f32(zeta_t), f32(gc)


def _retention_prompt(z, nbatch, seqlen):
    chunk = RET_CHUNK
    hb = 4
    cos, sn, sp = _rope_tables(np.arange(seqlen))
    mask, xi, zeta, gc = _decay_tables(chunk, 1)
    tab = lambda: pl.BlockSpec((seqlen, RET_DK), lambda b, h: (0, 0))
    head_tab = lambda w: pl.BlockSpec((hb, chunk, w), lambda b, h: (h, 0, 0))
    kern = functools.partial(_ret_prompt_kernel, chunk=chunk, nchunks=seqlen // chunk, hb=hb)
    qk_w, v_w = hb * RET_DK, hb * RET_DV
    return pl.pallas_call(
        kern,
        grid=(nbatch, RET_HEADS // hb),
        in_specs=[
            pl.BlockSpec(memory_space=pltpu.SMEM),
            pl.BlockSpec((seqlen, qk_w), lambda b, h: (b, Z_Q // qk_w + h)),
            pl.BlockSpec((seqlen, qk_w), lambda b, h: (b, Z_K // qk_w + h)),
            pl.BlockSpec((seqlen, v_w), lambda b, h: (b, Z_V // v_w + h)),
            pl.BlockSpec((seqlen, v_w), lambda b, h: (b, Z_GRET // v_w + h)),
            tab(), tab(), tab(),
            head_tab(chunk), head_tab(RET_DK), head_tab(RET_DK),
        ],
        out_specs=[
            pl.BlockSpec((seqlen, v_w), lambda b, h: (b, h)),
            pl.BlockSpec((None, hb, RET_DK, RET_DV), lambda b, h: (b, h, 0, 0)),
        ],
        out_shape=[
            jax.ShapeDtypeStruct((nbatch * seqlen, RET_V), BF16),
            jax.ShapeDtypeStruct((nbatch, RET_HEADS, RET_DK, RET_DV), F32),
        ],
        scratch_shapes=[pltpu.VMEM((hb, RET_DK, RET_DV), F32)],
        compiler_params=_cparams("parallel", "parallel"),
        name="retention_prompt",
    )(gc, z, z, z, z, cos, sn, sp, mask, xi, zeta)


def _retention_sample_operands(z, s0, row_off, nbatch, slen, pos0, nseq, block_of):
    rows = nseq * slen
    cos, sn, sp = (np.tile(t, (nseq, 1)) for t in _rope_tables(pos0 + np.arange(slen)))
    mask, xi, zeta, gc = _decay_tables(slen, nseq)
    rb = row_off // rows

    def at(fn):
        return lambda *ids: fn(*block_of(*ids))

    tab = lambda: pl.BlockSpec((rows, RET_DK), lambda *ids: (0, 0))
    head_tab = lambda w: pl.BlockSpec((None, rows, w), at(lambda i, h: (h, 0, 0)))
    state = lambda: pl.BlockSpec((nseq, None, RET_DK, RET_DV), at(lambda i, h: (i, h, 0, 0)))
    in_specs = [
        pl.BlockSpec(memory_space=pltpu.SMEM),
        pl.BlockSpec((rows, RET_DK), at(lambda i, h: (rb + i, Z_Q // RET_DK + h))),
        pl.BlockSpec((rows, RET_DK), at(lambda i, h: (rb + i, Z_K // RET_DK + h))),
        pl.BlockSpec((rows, RET_DV), at(lambda i, h: (rb + i, Z_V // RET_DV + h))),
        pl.BlockSpec((rows, RET_DV), at(lambda i, h: (rb + i, Z_GRET // RET_DV + h))),
        tab(), tab(), tab(),
        head_tab(rows), head_tab(RET_DK), head_tab(RET_DK),
        state(),
    ]
    out_specs = [pl.BlockSpec((rows, RET_DV), at(lambda i, h: (i, h))), state()]
    out_shape = [jax.ShapeDtypeStruct((nbatch * slen, RET_V), BF16),
                 jax.ShapeDtypeStruct((nbatch, RET_HEADS, RET_DK, RET_DV), F32)]
    return (gc, z, z, z, z, cos, sn, sp, mask, xi, zeta, s0), in_specs, out_specs, out_shape


def _s5_prep_kernel(are_ref, aim_ref, dt_ref, are_w_ref, aim_w_ref, dt_w_ref, bre_ref, bim_ref, cim_ref,
                    tabs_ref, bbre_ref, bbim_ref, ncim_ref):
    def abar(ar, ai, dt):
        mag = jnp.exp(dt * ar)
        return mag * jnp.cos(dt * ai), mag * jnp.sin(dt * ai)

    def powers(ar, ai):
        out = [(ar, ai)]
        for _ in range(1, SUBLANES):
            pr, pi = out[-1]
            out.append((pr * ar - pi * ai, pr * ai + pi * ar))
        return out

    pw = powers(*abar(are_ref[...], aim_ref[...], dt_ref[...]))
    qw = powers(*pw[-1])
    k_idx = lax.broadcasted_iota(jnp.int32, (SUBLANES, S5_BLK_STATE), 0)
    for j in range(S5_NBLK):
        every_row = lambda v: jnp.broadcast_to(v[j:j + 1, :], (SUBLANES, S5_BLK_STATE))
        for c in range(2):
            tabs_ref[j, T_A + c] = every_row(pw[0][c])
            for i in range(SUBLANES):
                tabs_ref[j, T_POW + 2 * i + c] = every_row(pw[i][c])
            for l, lvl in enumerate(SCAN_LEVELS):
                tabs_ref[j, T_LVL + 2 * l + c] = jnp.where(k_idx >= lvl, every_row(qw[lvl - 1][c]), 0.0)
            carry = every_row(qw[0][c])
            for k in range(1, SUBLANES):
                carry = jnp.where(k_idx == k, every_row(qw[k][c]), carry)
            tabs_ref[j, T_CARRY + c] = carry

    a_r, a_i = are_w_ref[...], aim_w_ref[...]
    w_r, w_i = abar(a_r, a_i, dt_w_ref[...])
    den = a_r * a_r + a_i * a_i
    x_re = w_r - 1.0
    f_re = (x_re * a_r + w_i * a_i) / den
    f_im = (w_i * a_r - x_re * a_i) / den
    br, bi = bre_ref[...], bim_ref[...]
    bbre_ref[...] = f_re * br - f_im * bi
    bbim_ref[...] = f_re * bi + f_im * br
    ncim_ref[...] = -cim_ref[...]


def _s5_prepare(a_re, a_im, log_step, b_re, b_im, c_re, c_im):
    g, n, p = S5_GROUPS, S5_STATE, S5_GROUP
    dt = jnp.broadcast_to(jnp.exp(log_step.astype(F32))[:, None], (g, n))
    wide = lambda t: jnp.repeat(t, p, axis=1)
    vm = lambda: pl.BlockSpec(memory_space=pltpu.VMEM)
    nb, bg = S5_NBLK, S5_BLK_GROUPS
    per_blk = lambda t: t.reshape(nb, S5_BLK_STATE)
    tabs, bbre, bbim, ncim = pl.pallas_call(
        _s5_prep_kernel,
        in_specs=[vm() for _ in range(9)],
        out_specs=[vm() for _ in range(4)],
        out_shape=[
            jax.ShapeDtypeStruct((nb, N_TABS, SUBLANES, S5_BLK_STATE), F32),
            jax.ShapeDtypeStruct((g, n * p), F32),
            jax.ShapeDtypeStruct((g, n * p), F32),
            jax.ShapeDtypeStruct((g, p * n), F32),
        ],
        name="s5_discretize",
    )(per_blk(a_re.astype(F32)), per_blk(a_im.astype(F32)), per_blk(dt),
      wide(a_re.astype(F32)), wide(a_im.astype(F32)), wide(dt),
      b_re.astype(F32).reshape(g, n * p), b_im.astype(F32).reshape(g, n * p), c_im.astype(F32).reshape(g, p * n))

    eye = jnp.eye(bg, dtype=bool)

    def in_blockdiag(t):
        t = t.reshape(nb, bg, n, p).transpose(0, 1, 3, 2)
        return jnp.where(eye[None, :, None, :, None], t[:, :, :, None, :], 0.0).reshape(nb, bg * p, bg * n)

    def out_blockdiag(t):
        t = t.reshape(nb, bg, p, n).transpose(0, 1, 3, 2)
        return jnp.where(eye[None, :, None, :, None], t[:, :, :, None, :], 0.0).reshape(nb, bg * n, bg * p)

    w_in = jnp.concatenate([in_blockdiag(bbre), in_blockdiag(bbim)], axis=-1).astype(BF16)
    w_out = jnp.concatenate([out_blockdiag(c_re.astype(F32).reshape(g, p * n)), out_blockdiag(ncim)],
                            axis=1).astype(BF16)
    return w_in, w_out, tabs


S5_SUB = SUBLANES * SUBLANES


def _segment_scan(er, ei, tabs_ref, cr, ci):
    for l, lvl in enumerate(SCAN_LEVELS):
        pr = tabs_ref[T_LVL + 2 * l]
        pi = tabs_ref[T_LVL + 2 * l + 1]
        sr = pltpu.roll(er, lvl, 0)
        si = pltpu.roll(ei, lvl, 0)
        er, ei = er + pr * sr - pi * si, ei + pr * si + pi * sr
    rr = tabs_ref[T_CARRY]
    ri = tabs_ref[T_CARRY + 1]
    crb = jnp.broadcast_to(cr, er.shape)
    cib = jnp.broadcast_to(ci, ei.shape)
    return er + rr * crb - ri * cib, ei + rr * cib + ri * crb


def _transpose_tiles(ref, nsub):
    return jnp.concatenate([ref[pl.ds(S5_SUB * m + i, SUBLANES, stride=SUBLANES), :]
                            for m in range(nsub) for i in range(SUBLANES)], axis=0)


def _s5_stream(u_ref, win_ref, wout_ref, d_ref, tabs_ref, uf_scr, bu_scr, h_scr, y_scr, init, long_seq):
    tc = u_ref.shape[0]
    nsub = tc // S5_SUB
    ns = S5_BLK_STATE
    uf = u_ref[...].astype(F32)
    uf_scr[...] = uf
    up = _transpose_tiles(uf_scr, nsub).astype(BF16)
    bu_scr[...] = jnp.dot(up, win_ref[...], preferred_element_type=F32)
    ar = tabs_ref[T_A]
    ai = tabs_ref[T_A + 1]
    finals = []
    for m in range(nsub):
        tile = lambda i: slice(S5_SUB * m + SUBLANES * i, S5_SUB * m + SUBLANES * (i + 1))
        if long_seq:
            hr = hi = None
        else:
            hr = init[0][SUBLANES * m:SUBLANES * (m + 1), :]
            hi = init[1][SUBLANES * m:SUBLANES * (m + 1), :]
        for i in range(SUBLANES):
            xr = bu_scr[tile(i), :ns]
            xi = bu_scr[tile(i), ns:]
            if hr is None:
                hr, hi = xr, xi
            else:
                hr, hi = ar * hr - ai * hi + xr, ar * hi + ai * hr + xi
            h_scr[tile(i), :ns] = hr
            h_scr[tile(i), ns:] = hi
        if long_seq:
            cr, ci = init
            fr, fi = _segment_scan(hr, hi, tabs_ref, cr, ci)
            first = lax.broadcasted_iota(jnp.int32, fr.shape, 0) == 0
            sr = jnp.where(first, jnp.broadcast_to(cr, fr.shape), pltpu.roll(fr, 1, 0))
            si = jnp.where(first, jnp.broadcast_to(ci, fi.shape), pltpu.roll(fi, 1, 0))
            init = (fr[SUBLANES - 1:, :], fi[SUBLANES - 1:, :])
            for i in range(SUBLANES):
                pr = tabs_ref[T_POW + 2 * i]
                pi = tabs_ref[T_POW + 2 * i + 1]
                h_scr[tile(i), :ns] = h_scr[tile(i), :ns] + pr * sr - pi * si
                h_scr[tile(i), ns:] = h_scr[tile(i), ns:] + pr * si + pi * sr
        else:
            finals.append((hr, hi))
    y_scr[...] = jnp.dot(h_scr[...].astype(BF16), wout_ref[...], preferred_element_type=F32)
    y = _transpose_tiles(y_scr, nsub) + d_ref[...] * uf
    return jax.nn.gelu(y).astype(BF16), (init if long_seq else finals)


N_RET_SAMPLE_IN = 12


def _s5_prompt_kernel(*refs, nb, nt, slen, nseq):
    u_refs = refs[:nb]
    win_ref, wout_ref, d_ref, tabs_ref = refs[nb:nb + 4]
    ret_in = refs[nb + 4:nb + 4 + N_RET_SAMPLE_IN]
    mem_ref, wkv_ref = refs[nb + 4 + N_RET_SAMPLE_IN:nb + 6 + N_RET_SAMPLE_IN]
    (gl_ref, hre_ref, him_ref, o_ret_ref, sfin_ref, kv_ref,
     uf_scr, bu_scr, h_scr, y_scr, carry_scr) = refs[nb + 6 + N_RET_SAMPLE_IN:]
    ns = S5_BLK_STATE

    head = (pl.program_id(0) * nt + pl.program_id(2)) % RET_HEADS
    _ret_sample_body(head, *ret_in, o_ret_ref, sfin_ref, slen=slen, nseq=nseq)

    @pl.when(pl.program_id(2) == 0)
    def _():
        kv_ref[...] = jnp.dot(mem_ref[...], wkv_ref[...].astype(BF16), preferred_element_type=F32)

    @pl.when(pl.program_id(2) == 0)
    def _():
        carry_scr[...] = jnp.zeros_like(carry_scr)

    for s in range(nb):
        init = (carry_scr[s, 0:1, :ns], carry_scr[s, 0:1, ns:])
        out, (cr, ci) = _s5_stream(u_refs[s], win_ref, wout_ref, d_ref, tabs_ref, uf_scr.at[s], bu_scr.at[s],
                                   h_scr.at[s], y_scr.at[s], init, True)
        gl_ref[s] = out
        carry_scr[s, 0:1, :ns] = cr
        carry_scr[s, 0:1, ns:] = ci
        hre_ref[s] = cr
        him_ref[s] = ci


def _s5_sample_kernel(u_ref, win_ref, wout_ref, d_ref, tabs_ref, h0re_ref, h0im_ref, gl_ref, hre_ref, him_ref,
                      uf_scr, bu_scr, h_scr, y_scr):
    out, finals = _s5_stream(u_ref, win_ref, wout_ref, d_ref, tabs_ref, uf_scr, bu_scr, h_scr, y_scr,
                             (h0re_ref, h0im_ref), False)
    gl_ref[...] = out
    for m, (hr, hi) in enumerate(finals):
        hre_ref[SUBLANES * m:SUBLANES * (m + 1), :] = hr
        him_ref[SUBLANES * m:SUBLANES * (m + 1), :] = hi


def _s5_weight_specs():
    nst = S5_BLK_STATE
    return [
        pl.BlockSpec((None, LANES, 2 * nst), lambda j, *_: (j, 0, 0)),
        pl.BlockSpec((None, 2 * nst, LANES), lambda j, *_: (j, 0, 0)),
        pl.BlockSpec((1, LANES), lambda j, *_: (0, j)),
        pl.BlockSpec((None, N_TABS, SUBLANES, nst), lambda j, *_: (j, 0, 0, 0)),
    ]


def _s5_scratch(lead, tc):
    nst = S5_BLK_STATE
    return [pltpu.VMEM(lead + (tc, LANES), F32), pltpu.VMEM(lead + (tc, 2 * nst), F32),
            pltpu.VMEM(lead + (tc, 2 * nst), F32), pltpu.VMEM(lead + (tc, LANES), F32)]


def _s5_prompt_ret_sample(z, w_in, w_out, d_skip, tabs, nbatch, seqlen, s0, row_off, nbatch_s, slen, pos0,
                          memb, w_kv):
    tc = 256
    nb = nbatch
    nt = seqlen // tc
    nst = S5_BLK_STATE
    nseq = 16
    steps = S5_NBLK * nt
    assert steps == (nbatch_s // nseq) * RET_HEADS
    ret_args, ret_in_specs, ret_out_specs, ret_out_shape = _retention_sample_operands(
        z, s0, row_off, nbatch_s, slen, pos0, nseq, lambda j, bb, t: divmod(j * nt + t, RET_HEADS))
    assert len(ret_args) == N_RET_SAMPLE_IN
    kern = functools.partial(_s5_prompt_kernel, nb=nb, nt=nt, slen=slen, nseq=nseq)
    u_spec = lambda s: pl.BlockSpec((tc, LANES), lambda j, bb, t: (s * nt + t, Z_U // LANES + j))
    state_spec = lambda: pl.BlockSpec((nb, 1, nst), lambda j, bb, t: (0, 0, j))
    mrows, kdim = memb.shape
    kv_bn = w_kv.shape[1] // S5_NBLK
    gl, hre, him, o_ret, sfin, kv = pl.pallas_call(
        kern,
        grid=(S5_NBLK, 1, nt),
        in_specs=([u_spec(s) for s in range(nb)] + _s5_weight_specs() + ret_in_specs + [
            pl.BlockSpec((mrows, kdim), lambda j, bb, t: (0, 0), pipeline_mode=pl.Buffered(1)),
            pl.BlockSpec((kdim, kv_bn), lambda j, bb, t: (0, j)),
        ]),
        out_specs=([pl.BlockSpec((nb, tc, LANES), lambda j, bb, t: (0, t, j)), state_spec(), state_spec()]
                   + ret_out_specs + [pl.BlockSpec((mrows, kv_bn), lambda j, bb, t: (0, j))]),
        out_shape=[
            jax.ShapeDtypeStruct((nbatch, seqlen, S5_WIDTH), BF16),
            jax.ShapeDtypeStruct((nbatch, 1, S5_GROUPS * S5_STATE), F32),
            jax.ShapeDtypeStruct((nbatch, 1, S5_GROUPS * S5_STATE), F32),
        ] + ret_out_shape + [jax.ShapeDtypeStruct((mrows, w_kv.shape[1]), F32)],
        scratch_shapes=_s5_scratch((nb,), tc) + [pltpu.VMEM((nb, SUBLANES, 2 * nst), F32)],
        compiler_params=_cparams("arbitrary", "arbitrary", "arbitrary"),
        name="s5_prompt_ret_sample",
    )(*([z] * nb), w_in, w_out, d_skip, tabs, *ret_args, memb, w_kv)
    return (gl.reshape(nbatch * seqlen, S5_WIDTH), hre.reshape(nbatch, S5_GROUPS, S5_STATE),
            him.reshape(nbatch, S5_GROUPS, S5_STATE), o_ret, sfin, kv)


def _s5_sample(z, w_in, w_out, d_skip, tabs, h0_re, h0_im, row_off, nbatch, slen):
    assert slen == SUBLANES
    tc = 256
    nseq = tc // slen
    rb = row_off // tc
    nst = S5_BLK_STATE
    state_spec = lambda: pl.BlockSpec((nseq, nst), lambda j, i: (i, j))
    gl, hre, him = pl.pallas_call(
        _s5_sample_kernel,
        grid=(S5_NBLK, nbatch // nseq),
        in_specs=([pl.BlockSpec((tc, LANES), lambda j, i: (rb + i, Z_U // LANES + j))] + _s5_weight_specs()
                  + [state_spec(), state_spec()]),
        out_specs=[pl.BlockSpec((tc, LANES), lambda j, i: (i, j)), state_spec(), state_spec()],
        out_shape=[
            jax.ShapeDtypeStruct((nbatch * slen, S5_WIDTH), BF16),
            jax.ShapeDtypeStruct((nbatch, S5_GROUPS * S5_STATE), F32),
            jax.ShapeDtypeStruct((nbatch, S5_GROUPS * S5_STATE), F32),
        ],
        scratch_shapes=_s5_scratch((), tc),
        compiler_params=_cparams("parallel", "parallel"),
        name="s5_sample",
    )(z, w_in, w_out, d_skip, tabs, h0_re.reshape(nbatch, -1).astype(F32), h0_im.reshape(nbatch, -1).astype(F32))
    return gl, hre.reshape(nbatch, S5_GROUPS, S5_STATE), him.reshape(nbatch, S5_GROUPS, S5_STATE)


def _xattn_kernel(q_ref, g_ref, k_ref, v_ref, o_ref, *, q_per_seq, k_per_seq):
    tq = q_ref.shape[0]
    nk = k_ref.shape[0]
    kb = k_ref[...].astype(BF16)
    vb = v_ref[...].astype(BF16)
    if tq // q_per_seq > 1:
        same = (lax.broadcasted_iota(jnp.int32, (tq, nk), 0) // q_per_seq
                == lax.broadcasted_iota(jnp.int32, (tq, nk), 1) // k_per_seq)
    else:
        same = None
    for h in range(X_HEADS):
        cols = slice(h * X_HD, (h + 1) * X_HD)
        s = lax.dot_general(q_ref[:, cols], kb[:, cols], (((1,), (1,)), ((), ())),
                            preferred_element_type=F32) * (X_HD ** -0.5)
        if same is not None:
            s = jnp.where(same, s, -jnp.inf)
        e = jnp.exp(s - jnp.max(s, axis=-1, keepdims=True))
        oh = jnp.dot(e.astype(BF16), vb[:, cols], preferred_element_type=F32) / jnp.sum(e, axis=-1, keepdims=True)
        g = g_ref[:, cols].astype(F32)
        o_ref[:, cols] = (oh * (g * jax.nn.sigmoid(g))).astype(BF16)


def _xattn(z, mk, mv, *, row_off, nrows, tq, q_per_seq, seqs_per_step, kv_col_blk, name):
    nk = seqs_per_step * MEM_LEN
    steps_per_kv = (seqs_per_step * q_per_seq) // tq if tq < seqs_per_step * q_per_seq else 1
    rb = row_off // tq
    kern = functools.partial(_xattn_kernel, q_per_seq=q_per_seq, k_per_seq=MEM_LEN)
    kmap = lambda cb: (lambda i: (i // steps_per_kv, cb))
    return pl.pallas_call(
        kern,
        grid=(nrows // tq,),
        in_specs=[
            pl.BlockSpec((tq, X_WIDTH), lambda i: (rb + i, Z_QX // X_WIDTH)),
            pl.BlockSpec((tq, X_WIDTH), lambda i: (rb + i, Z_GX // X_WIDTH)),
            pl.BlockSpec((nk, X_WIDTH), kmap(kv_col_blk[0])),
            pl.BlockSpec((nk, X_WIDTH), kmap(kv_col_blk[1])),
        ],
        out_specs=pl.BlockSpec((tq, X_WIDTH), lambda i: (i, 0)),
        out_shape=jax.ShapeDtypeStruct((nrows, X_WIDTH), BF16),
        compiler_params=_cparams("parallel"),
        name=name,
    )(z, z, mk, mv)


def _xattn_sample_guest(z, ck, cv, *, row_off, nrows, q_per_seq, ncols_host, nrows_host):
    seqs = 2
    tq = seqs * q_per_seq
    nk = seqs * MEM_LEN
    assert ncols_host * nrows_host == nrows // tq
    rb = row_off // tq
    blk = lambda j, i: j * nrows_host + i
    in_specs = [
        pl.BlockSpec((tq, X_WIDTH), lambda j, i: (rb + blk(j, i), Z_QX // X_WIDTH)),
        pl.BlockSpec((tq, X_WIDTH), lambda j, i: (rb + blk(j, i), Z_GX // X_WIDTH)),
        pl.BlockSpec((nk, X_WIDTH), lambda j, i: (blk(j, i), 0)),
        pl.BlockSpec((nk, X_WIDTH), lambda j, i: (blk(j, i), 0)),
    ]
    out_specs = [pl.BlockSpec((tq, X_WIDTH), lambda j, i: (blk(j, i), 0))]
    out_shape = [jax.ShapeDtypeStruct((nrows, X_WIDTH), BF16)]
    body = functools.partial(_xattn_kernel, q_per_seq=q_per_seq, k_per_seq=MEM_LEN)
    vmem = 2 * 2 * nk * X_WIDTH * 4 + 2 * nk * X_WIDTH * 2
    return ((z, z, ck, cv), in_specs, out_specs, out_shape, body), vmem


def _cast_epilogue(prods, extras):
    return prods[0]


def _glu_epilogue(prods, extras):
    a, b = prods
    g = extras[0].astype(F32)
    return a * jax.nn.sigmoid(b) * (g * jax.nn.sigmoid(g))


def _glu(gl, w_glu, z, row_off, *, bm, bn, guest=None, guest_bytes=0):
    return _colmm([gl], [(0, w_glu, 0), (0, w_glu, S5_WIDTH)], [(z, row_off // bm, Z_GS5 // bn)], _glu_epilogue,
                  n_out=S5_WIDTH, bm=bm, bn=bn, out_dtype=BF16, name="glu", guest=guest, guest_bytes=guest_bytes)


def _merge_epilogue(prods, extras):
    out = jax.nn.sigmoid(extras[0].astype(F32)) * prods[0]
    for p, m in zip(prods[1:], extras[1:]):
        out = out + jax.nn.sigmoid(m.astype(F32)) * p
    return out


def _merge(o_a, o_b, o_c, w_a, w_b, w_c, z, row_off, *, bm, bn):
    rb = row_off // bm
    return _colmm([o_a, o_b, o_c], [(0, w_a, 0), (1, w_b, 0), (2, w_c, 0)],
                  [(z, rb, Z_MA // bn), (z, rb, Z_MB // bn), (z, rb, Z_MC // bn)], _merge_epilogue,
                  n_out=D_MODEL, bm=bm, bn=bn, out_dtype=BF16, name="merge")


def _residual_epilogue(prods, extras):
    return DN_ALPHA * extras[0] + prods[0]


def _layer_norm_kernel(p_ref, g_ref, b_ref, o_ref):
    x = p_ref[...]
    mu = jnp.mean(x, axis=-1, keepdims=True)
    d = x - mu
    var = jnp.mean(d * d, axis=-1, keepdims=True)
    o_ref[...] = d * lax.rsqrt(var + LN_EPS) * g_ref[...] + b_ref[...]


def _out_ln(merged, w_out, x, ln_g, ln_b, *, bm, bn, bm_ln):
    m, d = x.shape
    pre = _colmm([merged], [(0, w_out, 0)], [(x, 0, 0)], _residual_epilogue, n_out=d, bm=bm, bn=bn,
                 out_dtype=F32, name="out_proj")
    row = lambda: pl.BlockSpec((1, d), lambda i: (0, 0))
    return pl.pallas_call(
        _layer_norm_kernel,
        grid=(m // bm_ln,),
        in_specs=[pl.BlockSpec((bm_ln, d), lambda i: (i, 0)), row(), row()],
        out_specs=pl.BlockSpec((bm_ln, d), lambda i: (i, 0)),
        out_shape=jax.ShapeDtypeStruct((m, d), F32),
        compiler_params=_cparams("parallel"),
        name="layer_norm",
    )(pre, ln_g.reshape(1, d).astype(F32), ln_b.reshape(1, d).astype(F32))


def _merge_out(z, row_off, o_ret, o_s5, o_x, x2d, w):
    merged = _merge(o_ret, o_s5, o_x, w["proj_a"], w["proj_b"], w["proj_c"], z, row_off, bm=512, bn=512)
    return _out_ln(merged, w["out"], x2d, w["ln_g"], w["ln_b"], bm=512, bn=1024, bm_ln=256)


def kernel(x_prompt, x_sample, mem_prompt, state_ret, state_s5_re, state_s5_im, cache_mem_k, cache_mem_v, w_in, w_mem_kv, s5_a_re, s5_a_im, s5_log_step, s5_b_re, s5_b_im, s5_c_re, s5_c_im, s5_d, w_glu, w_proj_a, w_proj_b, w_proj_c, w_out, ln_g, ln_b):
    depth = w_in.shape[0]
    assert depth == 1
    l = 0
    n_p = BATCH * SEQ
    n_s = DEC_BATCH * DEC_SEQ
    xp2 = x_prompt.reshape(n_p, D_MODEL)
    xs2 = x_sample.reshape(n_s, D_MODEL)

    w = dict(glu=w_glu[l], proj_a=w_proj_a[l], proj_b=w_proj_b[l], proj_c=w_proj_c[l],
             out=w_out[l], ln_g=ln_g[l], ln_b=ln_b[l])

    xb = _concat_cast(xp2, xs2, bm=512, out_dtype=BF16)
    z = _colmm([xb], [(0, w_in[l], 0)], [], _cast_epilogue, n_out=IN_WIDTH, bm=1024, bn=1024, out_dtype=BF16,
               name="in_proj")

    memb = mem_prompt.reshape(BATCH * MEM_LEN, D_MODEL).astype(BF16)
    s5_win, s5_wout, s5_tabs = _s5_prepare(s5_a_re[l], s5_a_im[l], s5_log_step[l], s5_b_re[l], s5_b_im[l],
                                           s5_c_re[l], s5_c_im[l])
    d_skip = s5_d[l].reshape(1, S5_WIDTH).astype(F32)

    o_ret_p, ret_p = _retention_prompt(z, BATCH, SEQ)
    gl_p, hre_p, him_p, o_ret_s, ret_s, kv = _s5_prompt_ret_sample(
        z, s5_win, s5_wout, d_skip, s5_tabs, BATCH, SEQ, state_ret[l], n_p, DEC_BATCH, DEC_SEQ, PAST_LEN,
        memb, w_mem_kv[l])
    mk, mv = kv[:, :X_WIDTH], kv[:, X_WIDTH:]
    o_x_p = _xattn(z, kv, kv, row_off=0, nrows=n_p, tq=512, q_per_seq=SEQ, seqs_per_step=1,
                   kv_col_blk=(0, 1), name="xattn_prompt")
    ck = cache_mem_k[l].reshape(DEC_BATCH * MEM_LEN, X_WIDTH)
    cv = cache_mem_v[l].reshape(DEC_BATCH * MEM_LEN, X_WIDTH)
    glu_bm, glu_bn = 512, 512
    guest, guest_bytes = _xattn_sample_guest(z, ck, cv, row_off=n_p, nrows=n_s, q_per_seq=DEC_SEQ,
                                             ncols_host=S5_WIDTH // glu_bn, nrows_host=n_p // glu_bm)
    o_s5_p, o_x_s = _glu(gl_p, w["glu"], z, 0, bm=glu_bm, bn=glu_bn, guest=guest, guest_bytes=guest_bytes)
    y_p = _merge_out(z, 0, o_ret_p, o_s5_p, o_x_p, xp2, w)

    gl_s, hre_s, him_s = _s5_sample(z, s5_win, s5_wout, d_skip, s5_tabs, state_s5_re[l], state_s5_im[l],
                                    n_p, DEC_BATCH, DEC_SEQ)
    o_s5_s = _glu(gl_s, w["glu"], z, n_p, bm=1024, bn=512)
    y_s = _merge_out(z, n_p, o_ret_s, o_s5_s, o_x_s, xs2, w)

    return (y_p.reshape(BATCH, SEQ, D_MODEL), y_s.reshape(DEC_BATCH, DEC_SEQ, D_MODEL),
            ret_p[None], hre_p[None], him_p[None],
            mk.reshape(1, BATCH, MEM_LEN, X_HEADS, X_HD), mv.reshape(1, BATCH, MEM_LEN, X_HEADS, X_HD),
            ret_s[None], hre_s[None], him_s[None])
```

```python
import functools
import math

import jax
import jax.numpy as jnp
import numpy as np
from jax import lax
from jax.experimental import pallas as pl
from jax.experimental.pallas import tpu as pltpu

F32 = jnp.float32
BF16 = jnp.bfloat16

D_MODEL = 4096
BATCH = 4
SEQ = 2048
DEC_BATCH = 128
DEC_SEQ = 8
PAST_LEN = 16384

RET_HEADS = 16
RET_DK = 128
RET_DV = 256
RET_QK = RET_HEADS * RET_DK
RET_V = RET_HEADS * RET_DV
RET_CHUNK = 128
ROPE_BASE = 10000.0

S5_WIDTH = D_MODEL // 2
S5_GROUP = 16
S5_GROUPS = S5_WIDTH // S5_GROUP
S5_STATE = 64

X_HEADS = 4
X_WIDTH = D_MODEL // 2
X_HD = X_WIDTH // X_HEADS
MEM_LEN = 256

DN_ALPHA = 2.0 ** 0.25
LN_EPS = 1e-5
GN_EPS = 1e-5

IN_WIDTH = 2 * RET_QK + 2 * RET_V + 2 * S5_WIDTH + 2 * X_WIDTH + 3 * D_MODEL

Z_Q = 0
Z_K = Z_Q + RET_QK
Z_V = Z_K + RET_QK
Z_GRET = Z_V + RET_V
Z_U = Z_GRET + RET_V
Z_GS5 = Z_U + S5_WIDTH
Z_QX = Z_GS5 + S5_WIDTH
Z_GX = Z_QX + X_WIDTH
Z_MA = Z_GX + X_WIDTH
Z_MB = Z_MA + D_MODEL
Z_MC = Z_MB + D_MODEL

SUBLANES = 8
LANES = 128
VMEM_PHYSICAL_BYTES = 64 * 1024 * 1024
VMEM_LIMIT_BYTES = 56 * 1024 * 1024
VMEM_TEMP_BYTES = 12 * 1024 * 1024

S5_BLK_GROUPS = LANES // S5_GROUP
S5_BLK_STATE = S5_BLK_GROUPS * S5_STATE
S5_NBLK = S5_GROUPS // S5_BLK_GROUPS
SCAN_LEVELS = (1, 2, 4)
T_A = 0
T_POW = 2
T_LVL = T_POW + 2 * SUBLANES
T_CARRY = T_LVL + 2 * len(SCAN_LEVELS)
N_TABS = T_CARRY + 2


def _cparams(*sem, vmem_limit_bytes=VMEM_LIMIT_BYTES):
    return pltpu.CompilerParams(dimension_semantics=sem, vmem_limit_bytes=vmem_limit_bytes)


def _concat_cast_kernel(a_ref, b_ref, o_ref, *, na):
    i = pl.program_id(0)

    @pl.when(i < na)
    def _():
        o_ref[...] = a_ref[...].astype(o_ref.dtype)

    @pl.when(i >= na)
    def _():
        o_ref[...] = b_ref[...].astype(o_ref.dtype)


def _concat_cast(a, b, *, bm, out_dtype):
    k = a.shape[1]
    na, nb = a.shape[0] // bm, b.shape[0] // bm
    return pl.pallas_call(
        functools.partial(_concat_cast_kernel, na=na),
        grid=(na + nb,),
        in_specs=[
            pl.BlockSpec((bm, k), lambda i: (jnp.minimum(i, na - 1), 0)),
            pl.BlockSpec((bm, k), lambda i: (jnp.maximum(i - na, 0), 0), pipeline_mode=pl.Buffered(1)),
        ],
        out_specs=pl.BlockSpec((bm, k), lambda i: (i, 0)),
        out_shape=jax.ShapeDtypeStruct((a.shape[0] + b.shape[0], k), out_dtype),
        compiler_params=_cparams("arbitrary"),
        name="concat_cast",
    )(a, b)


def _colmm_kernel(*refs, n_lhs, terms, n_extra, epilogue, ncols, bn, guest):
    lhs = refs[:n_lhs]
    w_hbm = refs[n_lhs:n_lhs + len(terms)]
    extras = refs[n_lhs + len(terms):n_lhs + len(terms) + n_extra]
    rest = refs[n_lhs + len(terms) + n_extra:]
    n_gin, n_gout, guest_body = guest
    guest_in, o_ref, guest_out = rest[:n_gin], rest[n_gin], rest[n_gin + 1:n_gin + 1 + n_gout]
    stage, wb_scr, sem = rest[n_gin + 1 + n_gout:]
    j = pl.program_id(0)
    i = pl.program_id(1)
    if guest_body is not None:
        guest_body(*guest_in, *guest_out)

    def tile_copies(col):
        return [pltpu.make_async_copy(
            w_hbm[t].at[:, pl.ds(pl.multiple_of(off + col * bn, LANES), bn)],
            stage.at[pl.ds(row0, kt), :], sem.at[t]) for t, (_, row0, kt, off) in enumerate(terms)]

    @pl.when(i == 0)
    def _():
        @pl.when(j == 0)
        def _():
            for c in tile_copies(0):
                c.start()

        for c in tile_copies(j):
            c.wait()
        wb_scr[...] = stage[...].astype(BF16)

        @pl.when(j + 1 < ncols)
        def _():
            for c in tile_copies(j + 1):
                c.start()

    prods = [jnp.dot(lhs[li][...], wb_scr[row0:row0 + kt, :], preferred_element_type=F32)
             for li, row0, kt, _ in terms]
    o_ref[...] = epilogue(prods, [e[...] for e in extras]).astype(o_ref.dtype)


def _colmm(lhs, weights, extras, epilogue, *, n_out, bm, bn, out_dtype, name, guest=None, guest_bytes=0):
    m = lhs[0].shape[0]
    g_args, g_in_specs, g_out_specs, g_out_shape, g_body = guest or ((), [], [], [], None)
    terms, row0 = [], 0
    for li, w, off in weights:
        kt = w.shape[0]
        assert lhs[li].shape == (m, kt) and off % LANES == 0
        terms.append((li, row0, kt, off))
        row0 += kt
    ktot = row0
    lhs_bytes = sum(2 * bm * a.shape[1] * a.dtype.itemsize for a in lhs)
    extra_bytes = sum(2 * bm * bn * a.dtype.itemsize for a, _, _ in extras)
    need = (ktot * bn * 6 + lhs_bytes + extra_bytes + 2 * bm * bn * jnp.dtype(out_dtype).itemsize
            + len(terms) * bm * bn * 4 + guest_bytes)
    limit = min(max(VMEM_LIMIT_BYTES, need + VMEM_TEMP_BYTES), VMEM_PHYSICAL_BYTES - (2 << 20))
    assert need + (4 << 20) <= limit, (name, need, limit)
    kern = functools.partial(_colmm_kernel, n_lhs=len(lhs), terms=tuple(terms), n_extra=len(extras),
                             epilogue=epilogue, ncols=n_out // bn, bn=bn,
                             guest=(len(g_args), len(g_out_specs), g_body))
    extra_spec = lambda rb, cb: pl.BlockSpec((bm, bn), lambda j, i: (rb + i, cb + j))
    outs = pl.pallas_call(
        kern,
        grid=(n_out // bn, m // bm),
        in_specs=([pl.BlockSpec((bm, a.shape[1]), lambda j, i: (i, 0)) for a in lhs]
                  + [pl.BlockSpec(memory_space=pl.ANY) for _ in terms]
                  + [extra_spec(rb, cb) for _, rb, cb in extras] + list(g_in_specs)),
        out_specs=[pl.BlockSpec((bm, bn), lambda j, i: (i, j))] + list(g_out_specs),
        out_shape=[jax.ShapeDtypeStruct((m, n_out), out_dtype)] + list(g_out_shape),
        scratch_shapes=[pltpu.VMEM((ktot, bn), F32), pltpu.VMEM((ktot, bn), BF16),
                        pltpu.SemaphoreType.DMA((len(terms),))],
        compiler_params=_cparams("arbitrary", "arbitrary", vmem_limit_bytes=limit),
        name=name,
    )(*lhs, *[w for _, w, _ in weights], *[a for a, _, _ in extras], *g_args)
    return outs if guest else outs[0]


def _rotate(x, cos, sin_next, sin_prev):
    return x * cos + pltpu.roll(x, LANES - 1, 1) * sin_next + pltpu.roll(x, 1, 1) * sin_prev


def _ret_block(q, k, v, g, cos, sin_next, sin_prev, mask, xi, zeta, gc, states, slen):
    nseq = len(states)
    rows = q.shape[0]
    qr = _rotate(q, cos, sin_next, sin_prev)
    kr = _rotate(k, cos, sin_next, sin_prev) * (RET_DK ** -0.5)
    qb = qr.astype(BF16)
    kb = kr.astype(BF16)
    sc = lax.dot_general(qb, kb, (((1,), (1,)), ((), ())), preferred_element_type=F32) * mask
    o = jnp.dot(sc.astype(BF16), v, preferred_element_type=F32)
    qx = qr * xi
    kzt = (kr * zeta).T.astype(BF16)
    new_states = []
    if nseq == 1:
        s = states[0]
        o = o + jnp.dot(qx.astype(BF16), s.astype(BF16), preferred_element_type=F32)
        new_states.append(gc * s + jnp.dot(kzt, v, preferred_element_type=F32))
    else:
        pair = 2 * slen
        assert pair == 2 * SUBLANES and nseq % 2 == 0
        row_in_pair = lax.broadcasted_iota(jnp.int32, (pair, RET_DV), 0)
        row_seq = lax.broadcasted_iota(jnp.int32, (rows, RET_DV), 0) // slen
        parts = []
        for m in range(nseq // 2):
            qpair = qx[m * pair:(m + 1) * pair].astype(BF16)
            o0 = jnp.dot(qpair, states[2 * m].astype(BF16), preferred_element_type=F32)
            o1 = jnp.dot(qpair, states[2 * m + 1].astype(BF16), preferred_element_type=F32)
            parts.append(jnp.where(row_in_pair < slen, o0, o1))
        o = o + jnp.concatenate(parts, axis=0)
        vf = v.astype(F32)
        for n in range(nseq):
            vn = jnp.where(row_seq == n, vf, 0.0).astype(BF16)
            new_states.append(gc * states[n] + jnp.dot(kzt, vn, preferred_element_type=F32))
    mu = jnp.mean(o, axis=-1, keepdims=True)
    d = o - mu
    var = jnp.mean(d * d, axis=-1, keepdims=True)
    on = d * lax.rsqrt(var + GN_EPS)
    out = (on * (g * jax.nn.sigmoid(g))).astype(BF16)
    return out, new_states


def _ret_prompt_kernel(gc_ref, q_ref, k_ref, v_ref, g_ref, cos_ref, sn_ref, sp_ref, mask_ref, xi_ref, zeta_ref,
                       o_ref, sfin_ref, s_scr, *, chunk, nchunks, hb):
    head0 = pl.program_id(1) * hb
    s_scr[...] = jnp.zeros_like(s_scr)

    def body(c, carry):
        rows = pl.ds(pl.multiple_of(c * chunk, chunk), chunk)
        cos, sn, sp = cos_ref[rows, :], sn_ref[rows, :], sp_ref[rows, :]
        for hh in range(hb):
            qc = slice(hh * RET_DK, (hh + 1) * RET_DK)
            vc = slice(hh * RET_DV, (hh + 1) * RET_DV)
            out, (s_new,) = _ret_block(
                q_ref[rows, qc].astype(F32), k_ref[rows, qc].astype(F32), v_ref[rows, vc],
                g_ref[rows, vc].astype(F32), cos, sn, sp, mask_ref[hh], xi_ref[hh], zeta_ref[hh],
                gc_ref[head0 + hh], [s_scr[hh]], chunk)
            o_ref[rows, vc] = out
            s_scr[hh] = s_new
        return carry

    lax.fori_loop(0, nchunks, body, 0)
    sfin_ref[...] = s_scr[...]


def _ret_sample_body(head, gc_ref, q_ref, k_ref, v_ref, g_ref, cos_ref, sn_ref, sp_ref, mask_ref, xi_ref, zeta_ref,
                     s0_ref, o_ref, sfin_ref, *, slen, nseq):
    gc = gc_ref[head]
    states = [s0_ref[n] for n in range(nseq)]
    out, new_states = _ret_block(
        q_ref[...].astype(F32), k_ref[...].astype(F32), v_ref[...], g_ref[...].astype(F32),
        cos_ref[...], sn_ref[...], sp_ref[...], mask_ref[...], xi_ref[...], zeta_ref[...], gc, states, slen)
    o_ref[...] = out
    for n in range(nseq):
        sfin_ref[n] = new_states[n]


def _rope_tables(pos):
    half = RET_DK // 2
    inv = 1.0 / (ROPE_BASE ** (np.arange(half, dtype=np.float64) / half))
    ang = np.asarray(pos, np.float64)[:, None] * inv[None, :]
    cos = np.repeat(np.cos(ang), 2, axis=1)
    sin = np.repeat(np.sin(ang), 2, axis=1)
    even = (np.arange(RET_DK) % 2) == 0
    return (np.asarray(cos, np.float32), np.asarray(np.where(even, -sin, 0.0), np.float32),
            np.asarray(np.where(even, 0.0, sin), np.float32))


def _decay_tables(slen, nseq):
    lg = np.log1p(-np.exp2(-5.0 - np.arange(RET_HEADS, dtype=np.float64)))
    idx = np.arange(slen, dtype=np.float64)
    rel = idx[:, None] - idx[None, :]
    inner = np.where(rel[None] >= 0, np.exp(lg[:, None, None] * np.maximum(rel, 0.0)[None]), 0.0)
    xi = np.exp(lg[:, None] * (idx + 1.0))
    zeta = np.exp(lg[:, None] * (slen - 1.0 - idx))
    gc = np.exp(lg * slen)
    mask = np.einsum("nm,hij->hnimj", np.eye(nseq), inner).reshape(RET_HEADS, nseq * slen, nseq * slen)
    rows = nseq * slen
    xi_t = np.broadcast_to(np.tile(xi, (1, nseq))[:, :, None], (RET_HEADS, rows, RET_DK))
    zeta_t = np.broadcast_to(np.tile(zeta, (1, nseq))[:, :, None], (RET_HEADS, rows, RET_DK))
    f32 = lambda t: np.ascontiguousarray(t, dtype=np.float32)
    return f32(mask), f32(xi_t), f32(zeta_t), f32(gc)


def _retention_prompt(z, nbatch, seqlen):
    chunk = RET_CHUNK
    hb = 4
    cos, sn, sp = _rope_tables(np.arange(seqlen))
    mask, xi, zeta, gc = _decay_tables(chunk, 1)
    tab = lambda: pl.BlockSpec((seqlen, RET_DK), lambda b, h: (0, 0))
    head_tab = lambda w: pl.BlockSpec((hb, chunk, w), lambda b, h: (h, 0, 0))
    kern = functools.partial(_ret_prompt_kernel, chunk=chunk, nchunks=seqlen // chunk, hb=hb)
    qk_w, v_w = hb * RET_DK, hb * RET_DV
    return pl.pallas_call(
        kern,
        grid=(nbatch, RET_HEADS // hb),
        in_specs=[
            pl.BlockSpec(memory_space=pltpu.SMEM),
            pl.BlockSpec((seqlen, qk_w), lambda b, h: (b, Z_Q // qk_w + h)),
            pl.BlockSpec((seqlen, qk_w), lambda b, h: (b, Z_K // qk_w + h)),
            pl.BlockSpec((seqlen, v_w), lambda b, h: (b, Z_V // v_w + h)),
            pl.BlockSpec((seqlen, v_w), lambda b, h: (b, Z_GRET // v_w + h)),
            tab(), tab(), tab(),
            head_tab(chunk), head_tab(RET_DK), head_tab(RET_DK),
        ],
        out_specs=[
            pl.BlockSpec((seqlen, v_w), lambda b, h: (b, h)),
            pl.BlockSpec((None, hb, RET_DK, RET_DV), lambda b, h: (b, h, 0, 0)),
        ],
        out_shape=[
            jax.ShapeDtypeStruct((nbatch * seqlen, RET_V), BF16),
            jax.ShapeDtypeStruct((nbatch, RET_HEADS, RET_DK, RET_DV), F32),
        ],
        scratch_shapes=[pltpu.VMEM((hb, RET_DK, RET_DV), F32)],
        compiler_params=_cparams("parallel", "parallel"),
        name="retention_prompt",
    )(gc, z, z, z, z, cos, sn, sp, mask, xi, zeta)


def _retention_sample_operands(z, s0, row_off, nbatch, slen, pos0, nseq, block_of):
    rows = nseq * slen
    cos, sn, sp = (np.tile(t, (nseq, 1)) for t in _rope_tables(pos0 + np.arange(slen)))
    mask, xi, zeta, gc = _decay_tables(slen, nseq)
    rb = row_off // rows

    def at(fn):
        return lambda *ids: fn(*block_of(*ids))

    tab = lambda: pl.BlockSpec((rows, RET_DK), lambda *ids: (0, 0))
    head_tab = lambda w: pl.BlockSpec((None, rows, w), at(lambda i, h: (h, 0, 0)))
    state = lambda: pl.BlockSpec((nseq, None, RET_DK, RET_DV), at(lambda i, h: (i, h, 0, 0)))
    in_specs = [
        pl.BlockSpec(memory_space=pltpu.SMEM),
        pl.BlockSpec((rows, RET_DK), at(lambda i, h: (rb + i, Z_Q // RET_DK + h))),
        pl.BlockSpec((rows, RET_DK), at(lambda i, h: (rb + i, Z_K // RET_DK + h))),
        pl.BlockSpec((rows, RET_DV), at(lambda i, h: (rb + i, Z_V // RET_DV + h))),
        pl.BlockSpec((rows, RET_DV), at(lambda i, h: (rb + i, Z_GRET // RET_DV + h))),
        tab(), tab(), tab(),
        head_tab(rows), head_tab(RET_DK), head_tab(RET_DK),
        state(),
    ]
    out_specs = [pl.BlockSpec((rows, RET_DV), at(lambda i, h: (i, h))), state()]
    out_shape = [jax.ShapeDtypeStruct((nbatch * slen, RET_V), BF16),
                 jax.ShapeDtypeStruct((nbatch, RET_HEADS, RET_DK, RET_DV), F32)]
    return (gc, z, z, z, z, cos, sn, sp, mask, xi, zeta, s0), in_specs, out_specs, out_shape


def _s5_prep_kernel(are_ref, aim_ref, dt_ref, are_w_ref, aim_w_ref, dt_w_ref, bre_ref, bim_ref, cim_ref,
                    tabs_ref, bbre_ref, bbim_ref, ncim_ref):
    def abar(ar, ai, dt):
        mag = jnp.exp(dt * ar)
        return mag * jnp.cos(dt * ai), mag * jnp.sin(dt * ai)

    def powers(ar, ai):
        out = [(ar, ai)]
        for _ in range(1, SUBLANES):
            pr, pi = out[-1]
            out.append((pr * ar - pi * ai, pr * ai + pi * ar))
        return out

    pw = powers(*abar(are_ref[...], aim_ref[...], dt_ref[...]))
    qw = powers(*pw[-1])
    k_idx = lax.broadcasted_iota(jnp.int32, (SUBLANES, S5_BLK_STATE), 0)
    for j in range(S5_NBLK):
        every_row = lambda v: jnp.broadcast_to(v[j:j + 1, :], (SUBLANES, S5_BLK_STATE))
        for c in range(2):
            tabs_ref[j, T_A + c] = every_row(pw[0][c])
            for i in range(SUBLANES):
                tabs_ref[j, T_POW + 2 * i + c] = every_row(pw[i][c])
            for l, lvl in enumerate(SCAN_LEVELS):
                tabs_ref[j, T_LVL + 2 * l + c] = jnp.where(k_idx >= lvl, every_row(qw[lvl - 1][c]), 0.0)
            carry = every_row(qw[0][c])
            for k in range(1, SUBLANES):
                carry = jnp.where(k_idx == k, every_row(qw[k][c]), carry)
            tabs_ref[j, T_CARRY + c] = carry

    a_r, a_i = are_w_ref[...], aim_w_ref[...]
    w_r, w_i = abar(a_r, a_i, dt_w_ref[...])
    den = a_r * a_r + a_i * a_i
    x_re = w_r - 1.0
    f_re = (x_re * a_r + w_i * a_i) / den
    f_im = (w_i * a_r - x_re * a_i) / den
    br, bi = bre_ref[...], bim_ref[...]
    bbre_ref[...] = f_re * br - f_im * bi
    bbim_ref[...] = f_re * bi + f_im * br
    ncim_ref[...] = -cim_ref[...]


def _s5_prepare(a_re, a_im, log_step, b_re, b_im, c_re, c_im):
    g, n, p = S5_GROUPS, S5_STATE, S5_GROUP
    dt = jnp.broadcast_to(jnp.exp(log_step.astype(F32))[:, None], (g, n))
    wide = lambda t: jnp.repeat(t, p, axis=1)
    vm = lambda: pl.BlockSpec(memory_space=pltpu.VMEM)
    nb, bg = S5_NBLK, S5_BLK_GROUPS
    per_blk = lambda t: t.reshape(nb, S5_BLK_STATE)
    tabs, bbre, bbim, ncim = pl.pallas_call(
        _s5_prep_kernel,
        in_specs=[vm() for _ in range(9)],
        out_specs=[vm() for _ in range(4)],
        out_shape=[
            jax.ShapeDtypeStruct((nb, N_TABS, SUBLANES, S5_BLK_STATE), F32),
            jax.ShapeDtypeStruct((g, n * p), F32),
            jax.ShapeDtypeStruct((g, n * p), F32),
            jax.ShapeDtypeStruct((g, p * n), F32),
        ],
        name="s5_discretize",
    )(per_blk(a_re.astype(F32)), per_blk(a_im.astype(F32)), per_blk(dt),
      wide(a_re.astype(F32)), wide(a_im.astype(F32)), wide(dt),
      b_re.astype(F32).reshape(g, n * p), b_im.astype(F32).reshape(g, n * p), c_im.astype(F32).reshape(g, p * n))

    eye = jnp.eye(bg, dtype=bool)

    def in_blockdiag(t):
        t = t.reshape(nb, bg, n, p).transpose(0, 1, 3, 2)
        return jnp.where(eye[None, :, None, :, None], t[:, :, :, None, :], 0.0).reshape(nb, bg * p, bg * n)

    def out_blockdiag(t):
        t = t.reshape(nb, bg, p, n).transpose(0, 1, 3, 2)
        return jnp.where(eye[None, :, None, :, None], t[:, :, :, None, :], 0.0).reshape(nb, bg * n, bg * p)

    w_in = jnp.concatenate([in_blockdiag(bbre), in_blockdiag(bbim)], axis=-1).astype(BF16)
    w_out = jnp.concatenate([out_blockdiag(c_re.astype(F32).reshape(g, p * n)), out_blockdiag(ncim)],
                            axis=1).astype(BF16)
    return w_in, w_out, tabs


S5_SUB = SUBLANES * SUBLANES


def _segment_scan(er, ei, tabs_ref, cr, ci):
    for l, lvl in enumerate(SCAN_LEVELS):
        pr = tabs_ref[T_LVL + 2 * l]
        pi = tabs_ref[T_LVL + 2 * l + 1]
        sr = pltpu.roll(er, lvl, 0)
        si = pltpu.roll(ei, lvl, 0)
        er, ei = er + pr * sr - pi * si, ei + pr * si + pi * sr
    rr = tabs_ref[T_CARRY]
    ri = tabs_ref[T_CARRY + 1]
    crb = jnp.broadcast_to(cr, er.shape)
    cib = jnp.broadcast_to(ci, ei.shape)
    return er + rr * crb - ri * cib, ei + rr * cib + ri * crb


def _transpose_tiles(ref, nsub):
    return jnp.concatenate([ref[pl.ds(S5_SUB * m + i, SUBLANES, stride=SUBLANES), :]
                            for m in range(nsub) for i in range(SUBLANES)], axis=0)


def _s5_stream(u_ref, win_ref, wout_ref, d_ref, tabs_ref, uf_scr, bu_scr, h_scr, y_scr, init, long_seq):
    tc = u_ref.shape[0]
    nsub = tc // S5_SUB
    ns = S5_BLK_STATE
    uf = u_ref[...].astype(F32)
    uf_scr[...] = uf
    up = _transpose_tiles(uf_scr, nsub).astype(BF16)
    bu_scr[...] = jnp.dot(up, win_ref[...], preferred_element_type=F32)
    ar = tabs_ref[T_A]
    ai = tabs_ref[T_A + 1]
    finals = []
    for m in range(nsub):
        tile = lambda i: slice(S5_SUB * m + SUBLANES * i, S5_SUB * m + SUBLANES * (i + 1))
        if long_seq:
            hr = hi = None
        else:
            hr = init[0][SUBLANES * m:SUBLANES * (m + 1), :]
            hi = init[1][SUBLANES * m:SUBLANES * (m + 1), :]
        for i in range(SUBLANES):
            xr = bu_scr[tile(i), :ns]
            xi = bu_scr[tile(i), ns:]
            if hr is None:
                hr, hi = xr, xi
            else:
                hr, hi = ar * hr - ai * hi + xr, ar * hi + ai * hr + xi
            h_scr[tile(i), :ns] = hr
            h_scr[tile(i), ns:] = hi
        if long_seq:
            cr, ci = init
            fr, fi = _segment_scan(hr, hi, tabs_ref, cr, ci)
            first = lax.broadcasted_iota(jnp.int32, fr.shape, 0) == 0
            sr = jnp.where(first, jnp.broadcast_to(cr, fr.shape), pltpu.roll(fr, 1, 0))
            si = jnp.where(first, jnp.broadcast_to(ci, fi.shape), pltpu.roll(fi, 1, 0))
            init = (fr[SUBLANES - 1:, :], fi[SUBLANES - 1:, :])
            for i in range(SUBLANES):
                pr = tabs_ref[T_POW + 2 * i]
                pi = tabs_ref[T_POW + 2 * i + 1]
                h_scr[tile(i), :ns] = h_scr[tile(i), :ns] + pr * sr - pi * si
                h_scr[tile(i), ns:] = h_scr[tile(i), ns:] + pr * si + pi * sr
        else:
            finals.append((hr, hi))
    y_scr[...] = jnp.dot(h_scr[...].astype(BF16), wout_ref[...], preferred_element_type=F32)
    y = _transpose_tiles(y_scr, nsub) + d_ref[...] * uf
    return jax.nn.gelu(y).astype(BF16), (init if long_seq else finals)


N_RET_SAMPLE_IN = 12


def _s5_prompt_kernel(*refs, nb, nt, slen, nseq):
    u_refs = refs[:nb]
    win_ref, wout_ref, d_ref, tabs_ref = refs[nb:nb + 4]
    ret_in = refs[nb + 4:nb + 4 + N_RET_SAMPLE_IN]
    mem_ref, wkv_ref = refs[nb + 4 + N_RET_SAMPLE_IN:nb + 6 + N_RET_SAMPLE_IN]
    (gl_ref, hre_ref, him_ref, o_ret_ref, sfin_ref, kv_ref,
     uf_scr, bu_scr, h_scr, y_scr, carry_scr) = refs[nb + 6 + N_RET_SAMPLE_IN:]
    ns = S5_BLK_STATE

    head = (pl.program_id(0) * nt + pl.program_id(2)) % RET_HEADS
    _ret_sample_body(head, *ret_in, o_ret_ref, sfin_ref, slen=slen, nseq=nseq)

    @pl.when(pl.program_id(2) == 0)
    def _():
        kv_ref[...] = jnp.dot(mem_ref[...], wkv_ref[...].astype(BF16), preferred_element_type=F32)

    @pl.when(pl.program_id(2) == 0)
    def _():
        carry_scr[...] = jnp.zeros_like(carry_scr)

    for s in range(nb):
        init = (carry_scr[s, 0:1, :ns], carry_scr[s, 0:1, ns:])
        out, (cr, ci) = _s5_stream(u_refs[s], win_ref, wout_ref, d_ref, tabs_ref, uf_scr.at[s], bu_scr.at[s],
                                   h_scr.at[s], y_scr.at[s], init, True)
        gl_ref[s] = out
        carry_scr[s, 0:1, :ns] = cr
        carry_scr[s, 0:1, ns:] = ci
        hre_ref[s] = cr
        him_ref[s] = ci


def _s5_sample_kernel(u_ref, win_ref, wout_ref, d_ref, tabs_ref, h0re_ref, h0im_ref, gl_ref, hre_ref, him_ref,
                      uf_scr, bu_scr, h_scr, y_scr):
    out, finals = _s5_stream(u_ref, win_ref, wout_ref, d_ref, tabs_ref, uf_scr, bu_scr, h_scr, y_scr,
                             (h0re_ref, h0im_ref), False)
    gl_ref[...] = out
    for m, (hr, hi) in enumerate(finals):
        hre_ref[SUBLANES * m:SUBLANES * (m + 1), :] = hr
        him_ref[SUBLANES * m:SUBLANES * (m + 1), :] = hi


def _s5_weight_specs():
    nst = S5_BLK_STATE
    return [
        pl.BlockSpec((None, LANES, 2 * nst), lambda j, *_: (j, 0, 0)),
        pl.BlockSpec((None, 2 * nst, LANES), lambda j, *_: (j, 0, 0)),
        pl.BlockSpec((1, LANES), lambda j, *_: (0, j)),
        pl.BlockSpec((None, N_TABS, SUBLANES, nst), lambda j, *_: (j, 0, 0, 0)),
    ]


def _s5_scratch(lead, tc):
    nst = S5_BLK_STATE
    return [pltpu.VMEM(lead + (tc, LANES), F32), pltpu.VMEM(lead + (tc, 2 * nst), F32),
            pltpu.VMEM(lead + (tc, 2 * nst), F32), pltpu.VMEM(lead + (tc, LANES), F32)]


def _s5_prompt_ret_sample(z, w_in, w_out, d_skip, tabs, nbatch, seqlen, s0, row_off, nbatch_s, slen, pos0,
                          memb, w_kv):
    tc = 512
    nb = nbatch
    nt = seqlen // tc
    nst = S5_BLK_STATE
    nseq = 32
    steps = S5_NBLK * nt
    assert steps == (nbatch_s // nseq) * RET_HEADS
    ret_args, ret_in_specs, ret_out_specs, ret_out_shape = _retention_sample_operands(
        z, s0, row_off, nbatch_s, slen, pos0, nseq, lambda j, bb, t: divmod(j * nt + t, RET_HEADS))
    assert len(ret_args) == N_RET_SAMPLE_IN
    kern = functools.partial(_s5_prompt_kernel, nb=nb, nt=nt, slen=slen, nseq=nseq)
    u_spec = lambda s: pl.BlockSpec((tc, LANES), lambda j, bb, t: (s * nt + t, Z_U // LANES + j))
    state_spec = lambda: pl.BlockSpec((nb, 1, nst), lambda j, bb, t: (0, 0, j))
    mrows, kdim = memb.shape
    kv_bn = w_kv.shape[1] // S5_NBLK
    gl, hre, him, o_ret, sfin, kv = pl.pallas_call(
        kern,
        grid=(S5_NBLK, 1, nt),
        in_specs=([u_spec(s) for s in range(nb)] + _s5_weight_specs() + ret_in_specs + [
            pl.BlockSpec((mrows, kdim), lambda j, bb, t: (0, 0), pipeline_mode=pl.Buffered(1)),
            pl.BlockSpec((kdim, kv_bn), lambda j, bb, t: (0, j)),
        ]),
        out_specs=([pl.BlockSpec((nb, tc, LANES), lambda j, bb, t: (0, t, j)), state_spec(), state_spec()]
                   + ret_out_specs + [pl.BlockSpec((mrows, kv_bn), lambda j, bb, t: (0, j))]),
        out_shape=[
            jax.ShapeDtypeStruct((nbatch, seqlen, S5_WIDTH), BF16),
            jax.ShapeDtypeStruct((nbatch, 1, S5_GROUPS * S5_STATE), F32),
            jax.ShapeDtypeStruct((nbatch, 1, S5_GROUPS * S5_STATE), F32),
        ] + ret_out_shape + [jax.ShapeDtypeStruct((mrows, w_kv.shape[1]), F32)],
        scratch_shapes=_s5_scratch((nb,), tc) + [pltpu.VMEM((nb, SUBLANES, 2 * nst), F32)],
        compiler_params=_cparams("arbitrary", "arbitrary", "arbitrary"),
        name="s5_prompt_ret_sample",
    )(*([z] * nb), w_in, w_out, d_skip, tabs, *ret_args, memb, w_kv)
    return (gl.reshape(nbatch * seqlen, S5_WIDTH), hre.reshape(nbatch, S5_GROUPS, S5_STATE),
            him.reshape(nbatch, S5_GROUPS, S5_STATE), o_ret, sfin, kv)


def _s5_sample(z, w_in, w_out, d_skip, tabs, h0_re, h0_im, row_off, nbatch, slen):
    assert slen == SUBLANES
    tc = 256
    nseq = tc // slen
    rb = row_off // tc
    nst = S5_BLK_STATE
    state_spec = lambda: pl.BlockSpec((nseq, nst), lambda j, i: (i, j))
    gl, hre, him = pl.pallas_call(
        _s5_sample_kernel,
        grid=(S5_NBLK, nbatch // nseq),
        in_specs=([pl.BlockSpec((tc, LANES), lambda j, i: (rb + i, Z_U // LANES + j))] + _s5_weight_specs()
                  + [state_spec(), state_spec()]),
        out_specs=[pl.BlockSpec((tc, LANES), lambda j, i: (i, j)), state_spec(), state_spec()],
        out_shape=[
            jax.ShapeDtypeStruct((nbatch * slen, S5_WIDTH), BF16),
            jax.ShapeDtypeStruct((nbatch, S5_GROUPS * S5_STATE), F32),
            jax.ShapeDtypeStruct((nbatch, S5_GROUPS * S5_STATE), F32),
        ],
        scratch_shapes=_s5_scratch((), tc),
        compiler_params=_cparams("parallel", "parallel"),
        name="s5_sample",
    )(z, w_in, w_out, d_skip, tabs, h0_re.reshape(nbatch, -1).astype(F32), h0_im.reshape(nbatch, -1).astype(F32))
    return gl, hre.reshape(nbatch, S5_GROUPS, S5_STATE), him.reshape(nbatch, S5_GROUPS, S5_STATE)


def _xattn_kernel(q_ref, g_ref, k_ref, v_ref, o_ref, *, q_per_seq, k_per_seq):
    tq = q_ref.shape[0]
    nk = k_ref.shape[0]
    kb = k_ref[...].astype(BF16)
    vb = v_ref[...].astype(BF16)
    if tq // q_per_seq > 1:
        same = (lax.broadcasted_iota(jnp.int32, (tq, nk), 0) // q_per_seq
                == lax.broadcasted_iota(jnp.int32, (tq, nk), 1) // k_per_seq)
    else:
        same = None
    for h in range(X_HEADS):
        cols = slice(h * X_HD, (h + 1) * X_HD)
        s = lax.dot_general(q_ref[:, cols], kb[:, cols], (((1,), (1,)), ((), ())),
                            preferred_element_type=F32) * (X_HD ** -0.5)
        if same is not None:
            s = jnp.where(same, s, -jnp.inf)
        e = jnp.exp(s - jnp.max(s, axis=-1, keepdims=True))
        oh = jnp.dot(e.astype(BF16), vb[:, cols], preferred_element_type=F32) / jnp.sum(e, axis=-1, keepdims=True)
        g = g_ref[:, cols].astype(F32)
        o_ref[:, cols] = (oh * (g * jax.nn.sigmoid(g))).astype(BF16)


def _xattn(z, mk, mv, *, row_off, nrows, tq, q_per_seq, seqs_per_step, kv_col_blk, name):
    nk = seqs_per_step * MEM_LEN
    steps_per_kv = (seqs_per_step * q_per_seq) // tq if tq < seqs_per_step * q_per_seq else 1
    rb = row_off // tq
    kern = functools.partial(_xattn_kernel, q_per_seq=q_per_seq, k_per_seq=MEM_LEN)
    kmap = lambda cb: (lambda i: (i // steps_per_kv, cb))
    return pl.pallas_call(
        kern,
        grid=(nrows // tq,),
        in_specs=[
            pl.BlockSpec((tq, X_WIDTH), lambda i: (rb + i, Z_QX // X_WIDTH)),
            pl.BlockSpec((tq, X_WIDTH), lambda i: (rb + i, Z_GX // X_WIDTH)),
            pl.BlockSpec((nk, X_WIDTH), kmap(kv_col_blk[0])),
            pl.BlockSpec((nk, X_WIDTH), kmap(kv_col_blk[1])),
        ],
        out_specs=pl.BlockSpec((tq, X_WIDTH), lambda i: (i, 0)),
        out_shape=jax.ShapeDtypeStruct((nrows, X_WIDTH), BF16),
        compiler_params=_cparams("parallel"),
        name=name,
    )(z, z, mk, mv)


def _xattn_sample_guest(z, ck, cv, *, row_off, nrows, q_per_seq, ncols_host, nrows_host):
    seqs = 2
    tq = seqs * q_per_seq
    nk = seqs * MEM_LEN
    assert ncols_host * nrows_host == nrows // tq
    rb = row_off // tq
    blk = lambda j, i: j * nrows_host + i
    in_specs = [
        pl.BlockSpec((tq, X_WIDTH), lambda j, i: (rb + blk(j, i), Z_QX // X_WIDTH)),
        pl.BlockSpec((tq, X_WIDTH), lambda j, i: (rb + blk(j, i), Z_GX // X_WIDTH)),
        pl.BlockSpec((nk, X_WIDTH), lambda j, i: (blk(j, i), 0)),
        pl.BlockSpec((nk, X_WIDTH), lambda j, i: (blk(j, i), 0)),
    ]
    out_specs = [pl.BlockSpec((tq, X_WIDTH), lambda j, i: (blk(j, i), 0))]
    out_shape = [jax.ShapeDtypeStruct((nrows, X_WIDTH), BF16)]
    body = functools.partial(_xattn_kernel, q_per_seq=q_per_seq, k_per_seq=MEM_LEN)
    vmem = 2 * 2 * nk * X_WIDTH * 4 + 2 * nk * X_WIDTH * 2
    return ((z, z, ck, cv), in_specs, out_specs, out_shape, body), vmem


def _cast_epilogue(prods, extras):
    return prods[0]


def _glu_epilogue(prods, extras):
    a, b = prods
    g = extras[0].astype(F32)
    return a * jax.nn.sigmoid(b) * (g * jax.nn.sigmoid(g))


def _glu(gl, w_glu, z, row_off, *, bm, bn, guest=None, guest_bytes=0):
    return _colmm([gl], [(0, w_glu, 0), (0, w_glu, S5_WIDTH)], [(z, row_off // bm, Z_GS5 // bn)], _glu_epilogue,
                  n_out=S5_WIDTH, bm=bm, bn=bn, out_dtype=BF16, name="glu", guest=guest, guest_bytes=guest_bytes)


def _merge_epilogue(prods, extras):
    out = jax.nn.sigmoid(extras[0].astype(F32)) * prods[0]
    for p, m in zip(prods[1:], extras[1:]):
        out = out + jax.nn.sigmoid(m.astype(F32)) * p
    return out


def _merge(o_a, o_b, o_c, w_a, w_b, w_c, z, row_off, *, bm, bn):
    rb = row_off // bm
    return _colmm([o_a, o_b, o_c], [(0, w_a, 0), (1, w_b, 0), (2, w_c, 0)],
                  [(z, rb, Z_MA // bn), (z, rb, Z_MB // bn), (z, rb, Z_MC // bn)], _merge_epilogue,
                  n_out=D_MODEL, bm=bm, bn=bn, out_dtype=BF16, name="merge")


def _residual_epilogue(prods, extras):
    return DN_ALPHA * extras[0] + prods[0]


def _layer_norm_kernel(p_ref, g_ref, b_ref, o_ref):
    x = p_ref[...]
    mu = jnp.mean(x, axis=-1, keepdims=True)
    d = x - mu
    var = jnp.mean(d * d, axis=-1, keepdims=True)
    o_ref[...] = d * lax.rsqrt(var + LN_EPS) * g_ref[...] + b_ref[...]


def _out_ln(merged, w_out, x, ln_g, ln_b, *, bm, bn, bm_ln):
    m, d = x.shape
    pre = _colmm([merged], [(0, w_out, 0)], [(x, 0, 0)], _residual_epilogue, n_out=d, bm=bm, bn=bn,
                 out_dtype=F32, name="out_proj")
    row = lambda: pl.BlockSpec((1, d), lambda i: (0, 0))
    return pl.pallas_call(
        _layer_norm_kernel,
        grid=(m // bm_ln,),
        in_specs=[pl.BlockSpec((bm_ln, d), lambda i: (i, 0)), row(), row()],
        out_specs=pl.BlockSpec((bm_ln, d), lambda i: (i, 0)),
        out_shape=jax.ShapeDtypeStruct((m, d), F32),
        compiler_params=_cparams("parallel"),
        name="layer_norm",
    )(pre, ln_g.reshape(1, d).astype(F32), ln_b.reshape(1, d).astype(F32))


def _merge_out(z, row_off, o_ret, o_s5, o_x, x2d, w):
    merged = _merge(o_ret, o_s5, o_x, w["proj_a"], w["proj_b"], w["proj_c"], z, row_off, bm=512, bn=512)
    return _out_ln(merged, w["out"], x2d, w["ln_g"], w["ln_b"], bm=512, bn=1024, bm_ln=256)


def kernel(x_prompt, x_sample, mem_prompt, state_ret, state_s5_re, state_s5_im, cache_mem_k, cache_mem_v, w_in, w_mem_kv, s5_a_re, s5_a_im, s5_log_step, s5_b_re, s5_b_im, s5_c_re, s5_c_im, s5_d, w_glu, w_proj_a, w_proj_b, w_proj_c, w_out, ln_g, ln_b):
    depth = w_in.shape[0]
    assert depth == 1
    l = 0
    n_p = BATCH * SEQ
    n_s = DEC_BATCH * DEC_SEQ
    xp2 = x_prompt.reshape(n_p, D_MODEL)
    xs2 = x_sample.reshape(n_s, D_MODEL)

    w = dict(glu=w_glu[l], proj_a=w_proj_a[l], proj_b=w_proj_b[l], proj_c=w_proj_c[l],
             out=w_out[l], ln_g=ln_g[l], ln_b=ln_b[l])

    xb = _concat_cast(xp2, xs2, bm=512, out_dtype=BF16)
    z = _colmm([xb], [(0, w_in[l], 0)], [], _cast_epilogue, n_out=IN_WIDTH, bm=1024, bn=1024, out_dtype=BF16,
               name="in_proj")

    memb = mem_prompt.reshape(BATCH * MEM_LEN, D_MODEL).astype(BF16)
    s5_win, s5_wout, s5_tabs = _s5_prepare(s5_a_re[l], s5_a_im[l], s5_log_step[l], s5_b_re[l], s5_b_im[l],
                                           s5_c_re[l], s5_c_im[l])
    d_skip = s5_d[l].reshape(1, S5_WIDTH).astype(F32)

    o_ret_p, ret_p = _retention_prompt(z, BATCH, SEQ)
    gl_p, hre_p, him_p, o_ret_s, ret_s, kv = _s5_prompt_ret_sample(
        z, s5_win, s5_wout, d_skip, s5_tabs, BATCH, SEQ, state_ret[l], n_p, DEC_BATCH, DEC_SEQ, PAST_LEN,
        memb, w_mem_kv[l])
    mk, mv = kv[:, :X_WIDTH], kv[:, X_WIDTH:]
    o_x_p = _xattn(z, kv, kv, row_off=0, nrows=n_p, tq=512, q_per_seq=SEQ, seqs_per_step=1,
                   kv_col_blk=(0, 1), name="xattn_prompt")
    ck = cache_mem_k[l].reshape(DEC_BATCH * MEM_LEN, X_WIDTH)
    cv = cache_mem_v[l].reshape(DEC_BATCH * MEM_LEN, X_WIDTH)
    glu_bm, glu_bn = 512, 512
    guest, guest_bytes = _xattn_sample_guest(z, ck, cv, row_off=n_p, nrows=n_s, q_per_seq=DEC_SEQ,
                                             ncols_host=S5_WIDTH // glu_bn, nrows_host=n_p // glu_bm)
    o_s5_p, o_x_s = _glu(gl_p, w["glu"], z, 0, bm=glu_bm, bn=glu_bn, guest=guest, guest_bytes=guest_bytes)
    y_p = _merge_out(z, 0, o_ret_p, o_s5_p, o_x_p, xp2, w)

    gl_s, hre_s, him_s = _s5_sample(z, s5_win, s5_wout, d_skip, s5_tabs, state_s5_re[l], state_s5_im[l],
                                    n_p, DEC_BATCH, DEC_SEQ)
    o_s5_s = _glu(gl_s, w["glu"], z, n_p, bm=1024, bn=512)
    y_s = _merge_out(z, n_p, o_ret_s, o_s5_s, o_x_s, xs2, w)

    return (y_p.reshape(BATCH, SEQ, D_MODEL), y_s.reshape(DEC_BATCH, DEC_SEQ, D_MODEL),
            ret_p[None], hre_p[None], him_p[None],
            mk.reshape(1, BATCH, MEM_LEN, X_HEADS, X_HD), mv.reshape(1, BATCH, MEM_LEN, X_HEADS, X_HD),
            ret_s[None], hre_s[None], him_s[None])
```
